```python
import jax
import jax.numpy as jnp
from jax import lax
import numpy as np

D_MODEL = 2048
BATCH = 2
SEQ = 4096
DEPTH = 4
DEC_BATCH = 8
DEC_SEQ = 4
PAST_LEN = 16384
PAGE_SIZE = 128

D_POOL = D_MODEL // 2
POOL_WINDOWS = (2, 4, 8, 16)
N_POOL_GROUPS = len(POOL_WINDOWS)
D_POOL_GROUP = D_POOL // N_POOL_GROUPS
POOL_HIST = max(POOL_WINDOWS) - 1
D_CONV = D_MODEL // 2
CONV_WIDTH = 3
CONV_HIST = CONV_WIDTH - 1
HEAD_DIM = 128
N_HEADS = D_MODEL // HEAD_DIM
D_ATTN = N_HEADS * HEAD_DIM
DILATED_GROUPS = ((128, 1), (512, 4), (2048, 16))
MAX_WINDOW = max(w for w, _ in DILATED_GROUPS)
Q_BLOCK = 128
ATTN_SCALE = HEAD_DIM ** -0.5
N_AB_LAYERS = (DEPTH + 1) // 2
N_C_LAYERS = DEPTH // 2
D_IN_AB = 2 * D_POOL + 4 * D_CONV
D_IN_C = 4 * D_ATTN
RMS_EPS = 1e-6
NEG_INF = -1e30

kernel_name = 'pool_conv_dilated_hybrid_step'


def _rmsnorm(x, w):
    xf = x.astype(jnp.float32)
    y = xf * lax.rsqrt(jnp.mean(xf * xf, axis=-1, keepdims=True) + RMS_EPS)
    return (y * w).astype(x.dtype)


def _multiscale_pool(u_hist, u, pos0):
    T = u.shape[1]
    ua = jnp.concatenate([u_hist, u], axis=1)
    c = jnp.cumsum(ua.astype(jnp.float32), axis=1)
    c = jnp.concatenate([jnp.zeros_like(c[:, :1]), c], axis=1)
    end = c[:, POOL_HIST + 1:]
    pos = (pos0 + jnp.arange(T)).astype(jnp.float32)
    means = []
    for g, k in enumerate(POOL_WINDOWS):
        sl = slice(g * D_POOL_GROUP, (g + 1) * D_POOL_GROUP)
        start = c[:, POOL_HIST + 1 - k: POOL_HIST + 1 - k + T, sl]
        cnt = jnp.minimum(float(k), pos + 1.0)[None, :, None]
        means.append((end[..., sl] - start) / cnt)
    mixed = jnp.concatenate(means, axis=-1) - u.astype(jnp.float32)
    return mixed.astype(u.dtype), ua[:, -POOL_HIST:]


def _short_conv(z_hist, z, w):
    T = z.shape[1]
    za = jnp.concatenate([z_hist, z], axis=1)
    y = za[:, 0:T] * w[0]
    for i in range(1, CONV_WIDTH):
        y = y + za[:, i:i + T] * w[i]
    return y, za[:, -CONV_HIST:]


def _ab_layer(x_n, pool_hist, conv_hist, pos0, w_in, p_lin, p_scale, c_w, w_out):
    B, T, _ = x_n.shape
    proj = x_n @ w_in
    cuts = [D_POOL, 2 * D_POOL, 2 * D_POOL + D_CONV, 2 * D_POOL + 2 * D_CONV, 2 * D_POOL + 3 * D_CONV]
    u, g_a, b_gate, c_gate, h, g_b = jnp.split(proj, cuts, axis=-1)
    pooled, new_pool = _multiscale_pool(pool_hist, u, pos0)
    pooled = pooled.reshape(B, T, N_POOL_GROUPS, D_POOL_GROUP)
    a_mix = jnp.einsum('btgc,gcd->btgd', pooled, p_lin).reshape(B, T, D_POOL) * p_scale
    a_out = a_mix * jax.nn.silu(g_a)
    conv, new_conv = _short_conv(conv_hist, c_gate * h, c_w)
    b_out = b_gate * conv * jax.nn.silu(g_b)
    y = jnp.concatenate([a_out, b_out], axis=-1) @ w_out
    return y, new_pool, new_conv


def _band_attn(q, k, v, n_back):
    lead = q.shape[:-2]
    L, hd = q.shape[-2:]
    nb = -(-L // Q_BLOCK)
    Lp = nb * Q_BLOCK
    pad = [(0, 0)] * len(lead) + [(0, Lp - L), (0, 0)]

    def blocks(a):
        return jnp.pad(a, pad).reshape(*lead, nb, Q_BLOCK, hd)

    def with_prev(a):
        prev = jnp.concatenate([jnp.zeros_like(a[..., :1, :, :]), a[..., :-1, :, :]], axis=-3)
        return jnp.concatenate([prev, a], axis=-2)

    qb = blocks(q)
    kk = with_prev(blocks(k))
    vv = with_prev(blocks(v))
    s = jnp.einsum('...qd,...kd->...qk', qb, kk).astype(jnp.float32) * ATTN_SCALE
    qi = jnp.arange(Q_BLOCK)[:, None] + Q_BLOCK
    ki = jnp.arange(2 * Q_BLOCK)[None, :]
    dist = qi - ki
    band = (dist >= 0) & (dist <= n_back)
    has_prev = (jnp.arange(nb) > 0)[:, None, None] | (ki >= Q_BLOCK)[None]
    mask = band[None] & has_prev
    s = jnp.where(mask, s, NEG_INF)
    m = jnp.max(s, axis=-1, keepdims=True)
    p = jnp.exp(s - m)
    den = jnp.sum(p, axis=-1, keepdims=True)
    o = jnp.einsum('...qk,...kd->...qd', p, vv.astype(jnp.float32)) / den
    lse = (m + jnp.log(den))[..., 0]
    o = o.reshape(*lead, Lp, hd)[..., :L, :]
    lse = lse.reshape(*lead, Lp)[..., :L]
    return o, lse


def _dilated_prompt(q, k, v, window, dil):
    B, T, H, hd = q.shape
    L = T // dil

    def split(a):
        return a.reshape(B, L, dil, H, hd).transpose(0, 2, 3, 1, 4)

    o, lse = _band_attn(split(q), split(k), split(v), window // dil)
    o = o.transpose(0, 3, 1, 2, 4).reshape(B, T, H, hd)
    lse = lse.transpose(0, 3, 1, 2).reshape(B, T, H)
    return o, lse


def _dilated_sample(q, k_all, v_all, n_buf, window, dil):
    S = q.shape[1]
    n_keys = window // dil + 1
    idx = n_buf + jnp.arange(S)[:, None] - dil * jnp.arange(n_keys)[None, :]
    valid = idx >= 0
    idx = jnp.maximum(idx, 0)
    kg = k_all[:, idx]
    vg = v_all[:, idx]
    s = jnp.einsum('bshd,bsnhd->bshn', q, kg).astype(jnp.float32) * ATTN_SCALE
    s = jnp.where(valid[None, :, None, :], s, NEG_INF)
    m = jnp.max(s, axis=-1, keepdims=True)
    p = jnp.exp(s - m)
    den = jnp.sum(p, axis=-1, keepdims=True)
    o = jnp.einsum('bshn,bsnhd->bshd', p, vg.astype(jnp.float32)) / den
    lse = (m + jnp.log(den))[..., 0]
    return o, lse


def _merge_by_denominator(results):
    outs = jnp.stack([o for o, _ in results], axis=0)
    wts = jax.nn.softmax(jnp.stack([l for _, l in results], axis=0), axis=0)
    return jnp.einsum('gbth,gbthd->bthd', wts, outs)


def _c_project(x_n, w_in):
    B, T, _ = x_n.shape
    q, k, v, g = jnp.split(x_n @ w_in, 4, axis=-1)
    heads = lambda a: a.reshape(B, T, N_HEADS, HEAD_DIM)
    return heads(q), heads(k), heads(v), g


def _c_out(o, g, w_out):
    B, T = g.shape[:2]
    return (o.reshape(B, T, D_ATTN).astype(g.dtype) * jax.nn.silu(g)) @ w_out


def _c_layer_prompt(x_n, w_in, w_out):
    q, k, v, g = _c_project(x_n, w_in)
    o = _merge_by_denominator([_dilated_prompt(q, k, v, w, d) for w, d in DILATED_GROUPS])
    n_keep = min(MAX_WINDOW, x_n.shape[1])
    return _c_out(o, g, w_out), k[:, -n_keep:], v[:, -n_keep:]


def _c_layer_sample(x_n, k_buf, v_buf, w_in, w_out):
    q, k, v, g = _c_project(x_n, w_in)
    n_buf = k_buf.shape[1]
    k_all = jnp.concatenate([k_buf, k], axis=1)
    v_all = jnp.concatenate([v_buf, v], axis=1)
    o = _merge_by_denominator([_dilated_sample(q, k_all, v_all, n_buf, w, d) for w, d in DILATED_GROUPS])
    n_keep = min(MAX_WINDOW, n_buf + x_n.shape[1])
    return _c_out(o, g, w_out), k_all[:, -n_keep:], v_all[:, -n_keep:]


def setup_inputs(seed: int = 0) -> dict:
    key = jax.random.key(seed)
    ks = jax.random.split(key, 16)
    f32 = jnp.float32

    def nrm(k, shape, scale):
        return jax.random.normal(k, shape, f32) * scale

    n_buf = min(MAX_WINDOW, PAST_LEN)
    return {
        'x_prompt': nrm(ks[0], (BATCH, SEQ, D_MODEL), 1.0),
        'x_sample': nrm(ks[1], (DEC_BATCH, DEC_SEQ, D_MODEL), 1.0),
        'state_pool': nrm(ks[2], (N_AB_LAYERS, DEC_BATCH, POOL_HIST, D_POOL), 1.0),
        'state_conv': nrm(ks[3], (N_AB_LAYERS, DEC_BATCH, CONV_HIST, D_CONV), 1.0),
        'cache_k': nrm(ks[4], (N_C_LAYERS, DEC_BATCH, n_buf, N_HEADS, HEAD_DIM), 1.0),
        'cache_v': nrm(ks[5], (N_C_LAYERS, DEC_BATCH, n_buf, N_HEADS, HEAD_DIM), 1.0),
        'norm_w': 1.0 + nrm(ks[6], (DEPTH, D_MODEL), 0.05),
        'final_norm_w': 1.0 + nrm(ks[7], (D_MODEL,), 0.05),
        'w_in_ab': nrm(ks[8], (N_AB_LAYERS, D_MODEL, D_IN_AB), D_MODEL ** -0.5),
        'pool_lin': nrm(ks[9], (N_AB_LAYERS, N_POOL_GROUPS, D_POOL_GROUP, D_POOL_GROUP), D_POOL_GROUP ** -0.5),
        'pool_scale': 1.0 + nrm(ks[10], (N_AB_LAYERS, D_POOL), 0.05),
        'conv_w': nrm(ks[11], (N_AB_LAYERS, CONV_WIDTH, D_CONV), CONV_WIDTH ** -0.5),
        'w_out_ab': nrm(ks[12], (N_AB_LAYERS, D_POOL + D_CONV, D_MODEL), (D_POOL + D_CONV) ** -0.5),
        'w_in_c': nrm(ks[13], (N_C_LAYERS, D_MODEL, D_IN_C), D_MODEL ** -0.5),
        'w_out_c': nrm(ks[14], (N_C_LAYERS, D_ATTN, D_MODEL), D_ATTN ** -0.5),
    }


def reference(x_prompt, x_sample, state_pool, state_conv, cache_k, cache_v, norm_w, final_norm_w,
              w_in_ab, pool_lin, pool_scale, conv_w, w_out_ab, w_in_c, w_out_c):
    bp = x_prompt.shape[0]
    zero_pool = jnp.zeros((bp, POOL_HIST, D_POOL), x_prompt.dtype)
    zero_conv = jnp.zeros((bp, CONV_HIST, D_CONV), x_prompt.dtype)
    hp, hs = x_prompt, x_sample
    pool_p, pool_s, conv_p, conv_s = [], [], [], []
    k_p, k_s, v_p, v_s = [], [], [], []
    for l in range(DEPTH):
        i = l // 2
        xp_n = _rmsnorm(hp, norm_w[l])
        xs_n = _rmsnorm(hs, norm_w[l])
        if l % 2 == 0:
            yp, pp, cp = _ab_layer(xp_n, zero_pool, zero_conv, 0, w_in_ab[i], pool_lin[i],
                                   pool_scale[i], conv_w[i], w_out_ab[i])
            ys, ps, cs = _ab_layer(xs_n, state_pool[i], state_conv[i], PAST_LEN, w_in_ab[i], pool_lin[i],
                                   pool_scale[i], conv_w[i], w_out_ab[i])
            pool_p.append(pp)
            pool_s.append(ps)
            conv_p.append(cp)
            conv_s.append(cs)
        else:
            yp, kp, vp = _c_layer_prompt(xp_n, w_in_c[i], w_out_c[i])
            ys, ksn, vsn = _c_layer_sample(xs_n, cache_k[i], cache_v[i], w_in_c[i], w_out_c[i])
            k_p.append(kp)
            v_p.append(vp)
            k_s.append(ksn)
            v_s.append(vsn)
        hp = hp + yp
        hs = hs + ys
    y_prompt = _rmsnorm(hp, final_norm_w)
    y_sample = _rmsnorm(hs, final_norm_w)
    return (y_prompt, y_sample, jnp.stack(pool_p), jnp.stack(pool_s), jnp.stack(conv_p), jnp.stack(conv_s),
            jnp.stack(k_p), jnp.stack(k_s), jnp.stack(v_p), jnp.stack(v_s))
```

```python
import functools

import jax
import jax.numpy as jnp
from jax import lax
from jax.experimental import pallas as pl
from jax.experimental.pallas import tpu as pltpu

F32 = jnp.float32
BF16 = jnp.bfloat16

D_MODEL = 2048
DEPTH = 4
PAST_LEN = 16384
D_POOL = D_MODEL // 2
POOL_WINDOWS = (2, 4, 8, 16)
D_POOL_GROUP = D_POOL // len(POOL_WINDOWS)
POOL_HIST = max(POOL_WINDOWS) - 1
D_CONV = D_MODEL // 2
CONV_WIDTH = 3
CONV_HIST = CONV_WIDTH - 1
HEAD_DIM = 128
N_HEADS = D_MODEL // HEAD_DIM
D_ATTN = N_HEADS * HEAD_DIM
DILATED_GROUPS = ((128, 1), (512, 4), (2048, 16))
MAX_WINDOW = max(w for w, _ in DILATED_GROUPS)
Q_BLOCK = 128
ATTN_SCALE = HEAD_DIM ** -0.5
RMS_EPS = 1e-6
NEG_INF = -1e30

SUBLANES = 8
POOL_PAD = 16
CONV_PAD = 8
VMEM_LIMIT = 48 * 1024 * 1024


def _silu(x):
    return x * jax.nn.sigmoid(x)


def _params(*semantics):
    return pltpu.CompilerParams(dimension_semantics=semantics, vmem_limit_bytes=VMEM_LIMIT)


def _norm_matmul_kernel(x_ref, nw_ref, w_ref, o_ref, xn_ref):
    @pl.when(pl.program_id(1) == 0)
    def _():
        x = x_ref[...]
        ms = jnp.mean(x * x, axis=-1, keepdims=True)
        xn_ref[...] = (x * lax.rsqrt(ms + RMS_EPS) * nw_ref[...]).astype(BF16)

    o_ref[...] = jnp.dot(xn_ref[...], w_ref[...], preferred_element_type=F32)


def _norm_matmul(x, nw, w, tm, tn):
    m, d = x.shape
    n = w.shape[1]
    return pl.pallas_call(
        _norm_matmul_kernel,
        grid=(m // tm, n // tn),
        in_specs=[
            pl.BlockSpec((tm, d), lambda i, j: (i, 0)),
            pl.BlockSpec((1, d), lambda i, j: (0, 0)),
            pl.BlockSpec((d, tn), lambda i, j: (0, j)),
        ],
        out_specs=pl.BlockSpec((tm, tn), lambda i, j: (i, j)),
        out_shape=jax.ShapeDtypeStruct((m, n), F32),
        scratch_shapes=[pltpu.VMEM((tm, d), BF16)],
        compiler_params=_params("parallel", "arbitrary"),
        name="norm_matmul",
    )(x, nw, w)


def _matmul_residual_kernel(a_ref, w_ref, h_ref, o_ref):
    o_ref[...] = h_ref[...] + jnp.dot(a_ref[...].astype(BF16), w_ref[...], preferred_element_type=F32)


def _matmul_residual(a, w, h, tm, tn):
    m, k = a.shape
    n = w.shape[1]
    return pl.pallas_call(
        _matmul_residual_kernel,
        grid=(m // tm, n // tn),
        in_specs=[
            pl.BlockSpec((tm, k), lambda i, j: (i, 0)),
            pl.BlockSpec((k, tn), lambda i, j: (0, j)),
            pl.BlockSpec((tm, tn), lambda i, j: (i, j)),
        ],
        out_specs=pl.BlockSpec((tm, tn), lambda i, j: (i, j)),
        out_shape=jax.ShapeDtypeStruct((m, n), F32),
        compiler_params=_params("parallel", "arbitrary"),
        name="matmul_residual",
    )(a, w, h)


def _rmsnorm_kernel(x_ref, nw_ref, o_ref):
    x = x_ref[...]
    ms = jnp.mean(x * x, axis=-1, keepdims=True)
    o_ref[...] = x * lax.rsqrt(ms + RMS_EPS) * nw_ref[...]


def _rmsnorm(x, nw, tm):
    m, d = x.shape
    return pl.pallas_call(
        _rmsnorm_kernel,
        grid=(m // tm,),
        in_specs=[pl.BlockSpec((tm, d), lambda i: (i, 0)), pl.BlockSpec((1, d), lambda i: (0, 0))],
        out_specs=pl.BlockSpec((tm, d), lambda i: (i, 0)),
        out_shape=jax.ShapeDtypeStruct((m, d), F32),
        compiler_params=_params("parallel"),
        name="final_rmsnorm",
    )(x, nw)


def _ab_mix_kernel(proj_ref, ph_ref, ch_ref, plin_ref, pscale_ref, cw_ref,
                   mix_ref, ptail_ref, ctail_ref, uext_ref, zext_ref, *, pos0, t_last):
    j = pl.program_id(1)
    tm = proj_ref.shape[1]
    gw = D_POOL_GROUP

    @pl.when(j == 0)
    def _():
        uext_ref[POOL_PAD - POOL_HIST:POOL_PAD, :] = ph_ref[0]
        zext_ref[CONV_PAD - CONV_HIST:CONV_PAD, :] = ch_ref[0]

    @pl.when(j > 0)
    def _():
        uext_ref[0:POOL_PAD, :] = uext_ref[tm:tm + POOL_PAD, :]
        zext_ref[0:CONV_PAD, :] = zext_ref[tm:tm + CONV_PAD, :]

    uext_ref[POOL_PAD:POOL_PAD + tm, :] = proj_ref[0, :, 0:D_POOL]
    c_off = 2 * D_POOL + D_CONV
    zext_ref[CONV_PAD:CONV_PAD + tm, :] = (
        proj_ref[0, :, c_off:c_off + D_CONV] * proj_ref[0, :, c_off + D_CONV:c_off + 2 * D_CONV])

    pos = (pos0 + j * tm + lax.broadcasted_iota(jnp.int32, (tm, 1), 0)).astype(F32)

    for g, k in enumerate(POOL_WINDOWS):
        c0 = g * gw
        u_g = uext_ref[POOL_PAD:POOL_PAD + tm, c0:c0 + gw]
        s = u_g
        for i in range(1, k):
            s = s + uext_ref[POOL_PAD - i:POOL_PAD - i + tm, c0:c0 + gw]
        cnt = jnp.minimum(float(k), pos + 1.0)
        pooled = s / cnt - u_g
        a = jnp.dot(pooled.astype(BF16), plin_ref[g], preferred_element_type=F32)
        a = a * pscale_ref[:, c0:c0 + gw]
        gate = proj_ref[0, :, D_POOL + c0:D_POOL + c0 + gw]
        mix_ref[0, :, c0:c0 + gw] = (a * _silu(gate)).astype(mix_ref.dtype)

    for c in range(D_CONV // gw):
        c0 = c * gw
        conv = zext_ref[CONV_PAD - 2:CONV_PAD - 2 + tm, c0:c0 + gw] * cw_ref[0:1, c0:c0 + gw]
        conv = conv + zext_ref[CONV_PAD - 1:CONV_PAD - 1 + tm, c0:c0 + gw] * cw_ref[1:2, c0:c0 + gw]
        conv = conv + zext_ref[CONV_PAD:CONV_PAD + tm, c0:c0 + gw] * cw_ref[2:3, c0:c0 + gw]
        b_gate = proj_ref[0, :, 2 * D_POOL + c0:2 * D_POOL + c0 + gw]
        gate = proj_ref[0, :, 2 * D_POOL + 3 * D_CONV + c0:2 * D_POOL + 3 * D_CONV + c0 + gw]
        mix_ref[0, :, D_POOL + c0:D_POOL + c0 + gw] = (b_gate * conv * _silu(gate)).astype(mix_ref.dtype)

    @pl.when(j == pl.num_programs(1) - 1)
    def _():
        ptail_ref[0] = uext_ref[POOL_PAD + t_last - POOL_HIST:POOL_PAD + t_last, :]
        ctail_ref[0] = zext_ref[CONV_PAD + t_last - CONV_HIST:CONV_PAD + t_last, :]


def _ab_mix(proj, pool_hist, conv_hist, plin, pscale, cw, tm, pos0, t_valid, mix_dtype):
    b, t, n = proj.shape
    nj = t // tm
    t_last = t_valid - (nj - 1) * tm
    kern = functools.partial(_ab_mix_kernel, pos0=pos0, t_last=t_last)
    return pl.pallas_call(
        kern,
        grid=(b, nj),
        in_specs=[
            pl.BlockSpec((1, tm, n), lambda i, j: (i, j, 0)),
            pl.BlockSpec((1, POOL_HIST, D_POOL), lambda i, j: (i, 0, 0)),
            pl.BlockSpec((1, CONV_HIST, D_CONV), lambda i, j: (i, 0, 0)),
            pl.BlockSpec(plin.shape, lambda i, j: (0, 0, 0)),
            pl.BlockSpec((1, D_POOL), lambda i, j: (0, 0)),
            pl.BlockSpec((CONV_WIDTH, D_CONV), lambda i, j: (0, 0)),
        ],
        out_specs=[
            pl.BlockSpec((1, tm, D_POOL + D_CONV), lambda i, j: (i, j, 0)),
            pl.BlockSpec((1, POOL_HIST, D_POOL), lambda i, j: (i, 0, 0)),
            pl.BlockSpec((1, CONV_HIST, D_CONV), lambda i, j: (i, 0, 0)),
        ],
        out_shape=[
            jax.ShapeDtypeStruct((b, t, D_POOL + D_CONV), mix_dtype),
            jax.ShapeDtypeStruct((b, POOL_HIST, D_POOL), F32),
            jax.ShapeDtypeStruct((b, CONV_HIST, D_CONV), F32),
        ],
        scratch_shapes=[pltpu.VMEM((tm + POOL_PAD, D_POOL), F32), pltpu.VMEM((tm + CONV_PAD, D_CONV), F32)],
        compiler_params=_params("parallel", "arbitrary"),
        name="ab_mix",
    )(proj, pool_hist, conv_hist, plin, pscale, cw)


def _attn_prompt_kernel(q_ref, k_ref, v_ref, g_ref, o_ref, acc_ref, m_ref, l_ref):
    t = q_ref.shape[0]
    qi = lax.broadcasted_iota(jnp.int32, (Q_BLOCK, 2 * Q_BLOCK), 0)
    ki = lax.broadcasted_iota(jnp.int32, (Q_BLOCK, 2 * Q_BLOCK), 1)
    delta = qi - ki
    nt_dims = (((1,), (1,)), ((), ()))

    def rows(start, size, d):
        if d == 1:
            return pl.ds(pl.multiple_of(start, Q_BLOCK), size)
        return pl.ds(start, size, stride=d)

    def run_group(d, first, last):
        nb = t // d // Q_BLOCK

        def body(idx, carry):
            r = idx // nb
            mb = idx % nb
            kb = jnp.maximum(mb - 1, 0)
            off = (mb - kb) * Q_BLOCK
            q_rows = rows(r + d * Q_BLOCK * mb, Q_BLOCK, d)
            k_rows = rows(r + d * Q_BLOCK * kb, 2 * Q_BLOCK, d)
            q = q_ref[q_rows, :].astype(BF16)
            k = k_ref[k_rows, :].astype(BF16)
            v = v_ref[k_rows, :].astype(BF16)
            s = lax.dot_general(q, k, nt_dims, preferred_element_type=F32) * ATTN_SCALE
            dist = delta + off
            s = jnp.where((dist >= 0) & (dist <= Q_BLOCK), s, NEG_INF)
            m_blk = jnp.max(s, axis=-1, keepdims=True)
            if first:
                m_new = m_blk
            else:
                m_prev = m_ref[q_rows, :]
                m_new = jnp.maximum(m_prev[:, 0:1], m_blk)
            p = jnp.exp(s - m_new)
            l_new = jnp.sum(p, axis=-1, keepdims=True)
            acc = jnp.dot(p.astype(BF16), v, preferred_element_type=F32)
            if not first:
                alpha = jnp.exp(m_prev - m_new)
                acc = alpha * acc_ref[q_rows, :] + acc
                l_new = alpha * l_ref[q_rows, :] + l_new
            if last:
                o = acc / l_new
                o_ref[q_rows, :] = (o * _silu(g_ref[q_rows, :])).astype(o_ref.dtype)
            else:
                acc_ref[q_rows, :] = acc
                m_ref[q_rows, :] = jnp.broadcast_to(m_new, (Q_BLOCK, HEAD_DIM))
                l_ref[q_rows, :] = jnp.broadcast_to(l_new, (Q_BLOCK, HEAD_DIM))
            return carry

        lax.fori_loop(0, d * nb, body, 0)

    run_group(16, True, False)
    run_group(4, False, False)
    run_group(1, False, True)


def _attn_prompt(proj, out_dtype):
    b, t, _ = proj.shape
    blk = (None, t, HEAD_DIM)
    return pl.pallas_call(
        _attn_prompt_kernel,
        grid=(b, N_HEADS),
        in_specs=[
            pl.BlockSpec(blk, lambda i, h: (i, 0, h)),
            pl.BlockSpec(blk, lambda i, h: (i, 0, N_HEADS + h)),
            pl.BlockSpec(blk, lambda i, h: (i, 0, 2 * N_HEADS + h)),
            pl.BlockSpec(blk, lambda i, h: (i, 0, 3 * N_HEADS + h)),
        ],
        out_specs=pl.BlockSpec(blk, lambda i, h: (i, 0, h)),
        out_shape=jax.ShapeDtypeStruct((b, t, D_ATTN), out_dtype),
        scratch_shapes=[pltpu.VMEM((t, HEAD_DIM), F32)] * 3,
        compiler_params=_params("parallel", "arbitrary"),
        name="attn_prompt",
    )(proj, proj, proj, proj)


def _group_count(dist):
    cnt = jnp.zeros(dist.shape, F32)
    for window, dil in DILATED_GROUPS:
        hit = (dist >= 0) & (dist <= window) & ((dist & (dil - 1)) == 0)
        cnt = cnt + hit.astype(F32)
    return cnt


def _attn_sample_kernel(q_ref, kn_ref, vn_ref, g_ref, kc_ref, vc_ref, o_ref, *, n_new):
    s_rows = q_ref.shape[1]
    n_buf = kc_ref.shape[1]
    nt_dims = (((1,), (1,)), ((), ()))
    dist_c = (n_buf + lax.broadcasted_iota(jnp.int32, (s_rows, n_buf), 0)
              - lax.broadcasted_iota(jnp.int32, (s_rows, n_buf), 1))
    cnt_c = _group_count(dist_c)
    row = lax.broadcasted_iota(jnp.int32, (s_rows, 1), 0)
    cnt_n = [_group_count(row - jn) for jn in range(n_new)]
    for hh in range(q_ref.shape[2] // HEAD_DIM):
        cs = slice(hh * HEAD_DIM, (hh + 1) * HEAD_DIM)
        q = q_ref[0, :, cs]
        kn = kn_ref[0, :, cs]
        vn = vn_ref[0, :, cs]
        s_c = lax.dot_general(q.astype(BF16), kc_ref[0, :, cs].astype(BF16), nt_dims,
                              preferred_element_type=F32) * ATTN_SCALE
        s_c = jnp.where(cnt_c > 0, s_c, NEG_INF)
        m = jnp.max(s_c, axis=-1, keepdims=True)
        s_n = []
        for jn in range(n_new):
            sj = jnp.sum(q * kn[jn:jn + 1, :], axis=-1, keepdims=True) * ATTN_SCALE
            sj = jnp.where(cnt_n[jn] > 0, sj, NEG_INF)
            s_n.append(sj)
            m = jnp.maximum(m, sj)
        p_c = cnt_c * jnp.exp(s_c - m)
        l = jnp.sum(p_c, axis=-1, keepdims=True)
        acc = jnp.dot(p_c.astype(BF16), vc_ref[0, :, cs].astype(BF16), preferred_element_type=F32)
        for jn in range(n_new):
            pj = cnt_n[jn] * jnp.exp(s_n[jn] - m)
            l = l + pj
            acc = acc + pj * vn[jn:jn + 1, :]
        o_ref[0, :, cs] = ((acc / l) * _silu(g_ref[0, :, cs])).astype(o_ref.dtype)


def _attn_sample(proj, cache_k, cache_v, n_new, heads_per_step=4):
    b, s, _ = proj.shape
    n_buf = cache_k.shape[1]
    w = heads_per_step * HEAD_DIM
    nh = D_ATTN // w
    pblk = (1, s, w)
    cblk = (1, n_buf, w)
    return pl.pallas_call(
        functools.partial(_attn_sample_kernel, n_new=n_new),
        grid=(b, nh),
        in_specs=[
            pl.BlockSpec(pblk, lambda i, h: (i, 0, h)),
            pl.BlockSpec(pblk, lambda i, h: (i, 0, nh + h)),
            pl.BlockSpec(pblk, lambda i, h: (i, 0, 2 * nh + h)),
            pl.BlockSpec(pblk, lambda i, h: (i, 0, 3 * nh + h)),
            pl.BlockSpec(cblk, lambda i, h: (i, 0, h)),
            pl.BlockSpec(cblk, lambda i, h: (i, 0, h)),
        ],
        out_specs=pl.BlockSpec(pblk, lambda i, h: (i, 0, h)),
        out_shape=jax.ShapeDtypeStruct((b, s, D_ATTN), F32),
        compiler_params=_params("parallel", "arbitrary"),
        name="attn_sample",
    )(proj, proj, proj, proj, cache_k, cache_v)


def kernel(x_prompt, x_sample, state_pool, state_conv, cache_k, cache_v, norm_w, final_norm_w,
           w_in_ab, pool_lin, pool_scale, conv_w, w_out_ab, w_in_c, w_out_c):
    bp, t, d = x_prompt.shape
    bs, ts, _ = x_sample.shape
    ts_pad = -(-ts // SUBLANES) * SUBLANES
    n_buf = cache_k.shape[2]
    n_keep_p = min(MAX_WINDOW, t)

    hp = x_prompt.reshape(bp * t, d)
    hs = jnp.pad(x_sample, ((0, 0), (0, ts_pad - ts), (0, 0))).reshape(bs * ts_pad, d)
    zero_pool = jnp.zeros((bp, POOL_HIST, D_POOL), F32)
    zero_conv = jnp.zeros((bp, CONV_HIST, D_CONV), F32)

    tm_p, tn = 1024, 1024
    tm_s = bs * ts_pad

    pool_p, pool_s, conv_p, conv_s = [], [], [], []
    k_p, k_s, v_p, v_s = [], [], [], []
    for l in range(DEPTH):
        i = l // 2
        nw = norm_w[l].reshape(1, d)
        if l % 2 == 0:
            w_in = w_in_ab[i].astype(BF16)
            w_out = w_out_ab[i].astype(BF16)
            plin = pool_lin[i].astype(BF16)
            pscale = pool_scale[i].reshape(1, D_POOL)
            proj_p = _norm_matmul(hp, nw, w_in, tm_p, tn).reshape(bp, t, -1)
            proj_s = _norm_matmul(hs, nw, w_in, tm_s, tn).reshape(bs, ts_pad, -1)
            mix_p, pp, cp = _ab_mix(proj_p, zero_pool, zero_conv, plin, pscale, conv_w[i],
                                    256, 0, t, BF16)
            mix_s, ps, cs = _ab_mix(proj_s, state_pool[i], state_conv[i], plin, pscale, conv_w[i],
                                    ts_pad, PAST_LEN, ts, F32)
            hp = _matmul_residual(mix_p.reshape(bp * t, -1), w_out, hp, tm_p, tn)
            hs = _matmul_residual(mix_s.reshape(bs * ts_pad, -1), w_out, hs, tm_s, tn)
            pool_p.append(pp)
            pool_s.append(ps)
            conv_p.append(cp)
            conv_s.append(cs)
        else:
            w_in = w_in_c[i].astype(BF16)
            w_out = w_out_c[i].astype(BF16)
            proj_p = _norm_matmul(hp, nw, w_in, tm_p, tn).reshape(bp, t, -1)
            proj_s = _norm_matmul(hs, nw, w_in, tm_s, tn).reshape(bs, ts_pad, -1)
            og_p = _attn_prompt(proj_p, BF16)
            og_s = _attn_sample(proj_s, cache_k[i].reshape(bs, n_buf, D_ATTN),
                                cache_v[i].reshape(bs, n_buf, D_ATTN), ts)
            hp = _matmul_residual(og_p.reshape(bp * t, -1), w_out, hp, tm_p, tn)
            hs = _matmul_residual(og_s.reshape(bs * ts_pad, -1), w_out, hs, tm_s, tn)
            heads = lambda a: a.reshape(a.shape[0], a.shape[1], N_HEADS, HEAD_DIM)
            k_p.append(heads(proj_p[:, t - n_keep_p:, D_ATTN:2 * D_ATTN]))
            v_p.append(heads(proj_p[:, t - n_keep_p:, 2 * D_ATTN:3 * D_ATTN]))
            n_keep_s = min(MAX_WINDOW, n_buf + ts)
            k_new = heads(proj_s[:, :ts, D_ATTN:2 * D_ATTN])
            v_new = heads(proj_s[:, :ts, 2 * D_ATTN:3 * D_ATTN])
            k_s.append(jnp.concatenate([cache_k[i], k_new], axis=1)[:, n_buf + ts - n_keep_s:])
            v_s.append(jnp.concatenate([cache_v[i], v_new], axis=1)[:, n_buf + ts - n_keep_s:])

    fw = final_norm_w.reshape(1, d)
    y_prompt = _rmsnorm(hp, fw, 512).reshape(bp, t, d)
    y_sample = _rmsnorm(hs, fw, tm_s).reshape(bs, ts_pad, d)[:, :ts]
    return (y_prompt, y_sample, jnp.stack(pool_p), jnp.stack(pool_s), jnp.stack(conv_p), jnp.stack(conv_s),
            jnp.stack(k_p), jnp.stack(k_s), jnp.stack(v_p), jnp.stack(v_s))
```

```python
import functools

import jax
import jax.numpy as jnp
from jax import lax
from jax.experimental import pallas as pl
from jax.experimental.pallas import tpu as pltpu

F32 = jnp.float32
BF16 = jnp.bfloat16

D_MODEL = 2048
DEPTH = 4
PAST_LEN = 16384
D_POOL = D_MODEL // 2
POOL_WINDOWS = (2, 4, 8, 16)
D_POOL_GROUP = D_POOL // len(POOL_WINDOWS)
POOL_HIST = max(POOL_WINDOWS) - 1
D_CONV = D_MODEL // 2
CONV_WIDTH = 3
CONV_HIST = CONV_WIDTH - 1
HEAD_DIM = 128
N_HEADS = D_MODEL // HEAD_DIM
D_ATTN = N_HEADS * HEAD_DIM
DILATED_GROUPS = ((128, 1), (512, 4), (2048, 16))
MAX_WINDOW = max(w for w, _ in DILATED_GROUPS)
Q_BLOCK = 128
ATTN_SCALE = HEAD_DIM ** -0.5
LOG2_E = 1.4426950408889634
RMS_EPS = 1e-6
NEG_INF = -1e30

SUBLANES = 8
POOL_PAD = 16
CONV_PAD = 8
DEINT_CHUNK = 2 * Q_BLOCK
SAMPLE_KEY_TILE = 512
BLOCKS_PER_TRIP = 8
VMEM_LIMIT = 48 * 1024 * 1024


def _silu(x):
    return x * jax.nn.sigmoid(x)


def _params(*semantics):
    return pltpu.CompilerParams(dimension_semantics=semantics, vmem_limit_bytes=VMEM_LIMIT)


def _norm_matmul_kernel(x_ref, nw_ref, w_ref, o_ref, xn_ref):
    @pl.when(pl.program_id(1) == 0)
    def _():
        x = x_ref[...]
        ms = jnp.mean(x * x, axis=-1, keepdims=True)
        xn_ref[...] = (x * lax.rsqrt(ms + RMS_EPS) * nw_ref[...]).astype(BF16)

    o_ref[...] = jnp.dot(xn_ref[...], w_ref[...], preferred_element_type=F32)


def _norm_matmul(x, nw, w, layer, tm, tn):
    m, d = x.shape
    n = w.shape[2]
    return pl.pallas_call(
        _norm_matmul_kernel,
        grid=(m // tm, n // tn),
        in_specs=[
            pl.BlockSpec((tm, d), lambda i, j: (i, 0)),
            pl.BlockSpec((1, d), lambda i, j: (0, 0)),
            pl.BlockSpec((None, d, tn), lambda i, j: (layer, 0, j)),
        ],
        out_specs=pl.BlockSpec((tm, tn), lambda i, j: (i, j)),
        out_shape=jax.ShapeDtypeStruct((m, n), F32),
        scratch_shapes=[pltpu.VMEM((tm, d), BF16)],
        compiler_params=_params("parallel", "arbitrary"),
        name="norm_matmul",
    )(x, nw, w)


def _matmul_residual_kernel(a_ref, w_ref, h_ref, o_ref):
    o_ref[...] = h_ref[...] + jnp.dot(a_ref[...].astype(BF16), w_ref[...], preferred_element_type=F32)


def _matmul_residual(a, w, layer, h, tm, tn):
    m, k = a.shape
    n = w.shape[2]
    return pl.pallas_call(
        _matmul_residual_kernel,
        grid=(m // tm, n // tn),
        in_specs=[
            pl.BlockSpec((tm, k), lambda i, j: (i, 0)),
            pl.BlockSpec((None, k, tn), lambda i, j: (layer, 0, j)),
            pl.BlockSpec((tm, tn), lambda i, j: (i, j)),
        ],
        out_specs=pl.BlockSpec((tm, tn), lambda i, j: (i, j)),
        out_shape=jax.ShapeDtypeStruct((m, n), F32),
        compiler_params=_params("parallel", "arbitrary"),
        name="matmul_residual",
    )(a, w, h)


def _rmsnorm_kernel(x_ref, nw_ref, o_ref):
    x = x_ref[...]
    ms = jnp.mean(x * x, axis=-1, keepdims=True)
    o_ref[...] = x * lax.rsqrt(ms + RMS_EPS) * nw_ref[...]


def _rmsnorm(x, nw, tm):
    m, d = x.shape
    return pl.pallas_call(
        _rmsnorm_kernel,
        grid=(m // tm,),
        in_specs=[pl.BlockSpec((tm, d), lambda i: (i, 0)), pl.BlockSpec((1, d), lambda i: (0, 0))],
        out_specs=pl.BlockSpec((tm, d), lambda i: (i, 0)),
        out_shape=jax.ShapeDtypeStruct((m, d), F32),
        compiler_params=_params("parallel"),
        name="final_rmsnorm",
    )(x, nw)


def _ab_mix_kernel(proj_ref, ph_ref, ch_ref, plin_ref, pscale_ref, cw_ref,
                   mix_ref, ptail_ref, ctail_ref, uext_ref, zext_ref, *, pos0, t_last):
    j = pl.program_id(1)
    tm = proj_ref.shape[1]
    gw = D_POOL_GROUP

    @pl.when(j == 0)
    def _():
        uext_ref[POOL_PAD - POOL_HIST:POOL_PAD, :] = ph_ref[0]
        zext_ref[CONV_PAD - CONV_HIST:CONV_PAD, :] = ch_ref[0]

    @pl.when(j > 0)
    def _():
        uext_ref[0:POOL_PAD, :] = uext_ref[tm:tm + POOL_PAD, :]
        zext_ref[0:CONV_PAD, :] = zext_ref[tm:tm + CONV_PAD, :]

    uext_ref[POOL_PAD:POOL_PAD + tm, :] = proj_ref[0, :, 0:D_POOL]
    c_off = 2 * D_POOL + D_CONV
    zext_ref[CONV_PAD:CONV_PAD + tm, :] = (
        proj_ref[0, :, c_off:c_off + D_CONV] * proj_ref[0, :, c_off + D_CONV:c_off + 2 * D_CONV])

    pos = (pos0 + j * tm + lax.broadcasted_iota(jnp.int32, (tm, 1), 0)).astype(F32)

    for g, k in enumerate(POOL_WINDOWS):
        c0 = g * gw
        u_g = uext_ref[POOL_PAD:POOL_PAD + tm, c0:c0 + gw]
        s = u_g
        for i in range(1, k):
            s = s + uext_ref[POOL_PAD - i:POOL_PAD - i + tm, c0:c0 + gw]
        cnt = jnp.minimum(float(k), pos + 1.0)
        pooled = s / cnt - u_g
        a = jnp.dot(pooled.astype(BF16), plin_ref[g], preferred_element_type=F32)
        a = a * pscale_ref[:, c0:c0 + gw]
        gate = proj_ref[0, :, D_POOL + c0:D_POOL + c0 + gw]
        mix_ref[0, :, c0:c0 + gw] = (a * _silu(gate)).astype(mix_ref.dtype)

    for c in range(D_CONV // gw):
        c0 = c * gw
        conv = zext_ref[CONV_PAD - 2:CONV_PAD - 2 + tm, c0:c0 + gw] * cw_ref[0:1, c0:c0 + gw]
        conv = conv + zext_ref[CONV_PAD - 1:CONV_PAD - 1 + tm, c0:c0 + gw] * cw_ref[1:2, c0:c0 + gw]
        conv = conv + zext_ref[CONV_PAD:CONV_PAD + tm, c0:c0 + gw] * cw_ref[2:3, c0:c0 + gw]
        b_gate = proj_ref[0, :, 2 * D_POOL + c0:2 * D_POOL + c0 + gw]
        gate = proj_ref[0, :, 2 * D_POOL + 3 * D_CONV + c0:2 * D_POOL + 3 * D_CONV + c0 + gw]
        mix_ref[0, :, D_POOL + c0:D_POOL + c0 + gw] = (b_gate * conv * _silu(gate)).astype(mix_ref.dtype)

    @pl.when(j == pl.num_programs(1) - 1)
    def _():
        ptail_ref[0] = uext_ref[POOL_PAD + t_last - POOL_HIST:POOL_PAD + t_last, :]
        ctail_ref[0] = zext_ref[CONV_PAD + t_last - CONV_HIST:CONV_PAD + t_last, :]


def _ab_mix(proj, pool_hist, conv_hist, plin, pscale, cw, tm, pos0, t_valid, mix_dtype):
    b, t, n = proj.shape
    nj = t // tm
    t_last = t_valid - (nj - 1) * tm
    kern = functools.partial(_ab_mix_kernel, pos0=pos0, t_last=t_last)
    return pl.pallas_call(
        kern,
        grid=(b, nj),
        in_specs=[
            pl.BlockSpec((1, tm, n), lambda i, j: (i, j, 0)),
            pl.BlockSpec((1, POOL_HIST, D_POOL), lambda i, j: (i, 0, 0)),
            pl.BlockSpec((1, CONV_HIST, D_CONV), lambda i, j: (i, 0, 0)),
            pl.BlockSpec(plin.shape, lambda i, j: (0, 0, 0)),
            pl.BlockSpec((1, D_POOL), lambda i, j: (0, 0)),
            pl.BlockSpec((CONV_WIDTH, D_CONV), lambda i, j: (0, 0)),
        ],
        out_specs=[
            pl.BlockSpec((1, tm, D_POOL + D_CONV), lambda i, j: (i, j, 0)),
            pl.BlockSpec((1, POOL_HIST, D_POOL), lambda i, j: (i, 0, 0)),
            pl.BlockSpec((1, CONV_HIST, D_CONV), lambda i, j: (i, 0, 0)),
        ],
        out_shape=[
            jax.ShapeDtypeStruct((b, t, D_POOL + D_CONV), mix_dtype),
            jax.ShapeDtypeStruct((b, POOL_HIST, D_POOL), F32),
            jax.ShapeDtypeStruct((b, CONV_HIST, D_CONV), F32),
        ],
        scratch_shapes=[pltpu.VMEM((tm + POOL_PAD, D_POOL), F32), pltpu.VMEM((tm + CONV_PAD, D_CONV), F32)],
        compiler_params=_params("parallel", "arbitrary"),
        name="ab_mix",
    )(proj, pool_hist, conv_hist, plin, pscale, cw)


def _rows(start, size, stride):
    if stride == 1:
        return pl.ds(start, size)
    return pl.ds(start, size, stride=stride)


def _attn_prompt_kernel(q_ref, k_ref, v_ref, g_ref, o_ref, kd_ref, vd_ref, bias_ref, s_ref, *state_refs):
    t = q_ref.shape[0]
    n_groups = len(DILATED_GROUPS)
    acc_refs = state_refs[:n_groups]
    stat_refs = state_refs[n_groups:]
    nt_dims = (((1,), (1,)), ((), ()))
    half = HEAD_DIM // 2

    delta = (lax.broadcasted_iota(jnp.int32, (Q_BLOCK, 2 * Q_BLOCK), 0)
             - lax.broadcasted_iota(jnp.int32, (Q_BLOCK, 2 * Q_BLOCK), 1))
    for sel in range(2):
        dist = delta + sel * Q_BLOCK
        bias_ref[sel] = jnp.where((dist >= 0) & (dist <= Q_BLOCK), 0.0, NEG_INF)
    lane = lax.broadcasted_iota(jnp.int32, (Q_BLOCK, HEAD_DIM), 1)
    vd_ref[:, :, HEAD_DIM:] = jnp.ones((n_groups, t, HEAD_DIM), BF16)

    def deinterleave(gi, d):
        stream_len = t // d

        def body(c, carry):
            dst = pl.multiple_of(c * DEINT_CHUNK, DEINT_CHUNK)
            r = dst // stream_len
            m0 = dst % stream_len
            src = _rows(r + d * m0, DEINT_CHUNK, d)
            kd_ref[gi, pl.ds(dst, DEINT_CHUNK), :] = k_ref[src, :].astype(BF16)
            vd_ref[gi, pl.ds(dst, DEINT_CHUNK), 0:HEAD_DIM] = v_ref[src, :].astype(BF16)
            return carry

        lax.fori_loop(0, t // DEINT_CHUNK, body, 0, unroll=2)

    def run_group(gi, d):
        stream_len = t // d
        nb = stream_len // Q_BLOCK

        def block_rows(idx):
            r = idx // nb
            mb = idx % nb
            kb = jnp.maximum(mb - 1, 0)
            q_rows = _rows(r + d * Q_BLOCK * mb, Q_BLOCK, d)
            k_rows = pl.ds(pl.multiple_of(r * stream_len + kb * Q_BLOCK, Q_BLOCK), 2 * Q_BLOCK)
            return q_rows, k_rows, mb - kb

        def scores(it, carry):
            s = []
            for u in range(BLOCKS_PER_TRIP):
                q_rows, k_rows, sel = block_rows(it * BLOCKS_PER_TRIP + u)
                q = (q_ref[q_rows, :] * (ATTN_SCALE * LOG2_E)).astype(BF16)
                s.append(lax.dot_general(q, kd_ref[gi, k_rows, :], nt_dims, preferred_element_type=F32)
                         + bias_ref[sel])
            for u in range(BLOCKS_PER_TRIP):
                s_ref[it * BLOCKS_PER_TRIP + u] = s[u]
            return carry

        def values(it, carry):
            rows, s, v = [], [], []
            for u in range(BLOCKS_PER_TRIP):
                q_rows, k_rows, _ = block_rows(it * BLOCKS_PER_TRIP + u)
                rows.append(q_rows)
                s.append(s_ref[it * BLOCKS_PER_TRIP + u])
                v.append(vd_ref[gi, k_rows, :])
            acc, stat = [], []
            for u in range(BLOCKS_PER_TRIP):
                m_blk = jnp.max(s[u], axis=-1, keepdims=True)
                p = jnp.exp2(s[u] - m_blk).astype(BF16)
                pv = jnp.dot(p, v[u], preferred_element_type=F32)
                acc.append(pv[:, 0:HEAD_DIM])
                stat.append(jnp.where(lane < half, m_blk, pv[:, HEAD_DIM:]))
            for u in range(BLOCKS_PER_TRIP):
                acc_refs[gi][rows[u], :] = acc[u]
                stat_refs[gi][rows[u], :] = stat[u]
            return carry

        lax.fori_loop(0, d * nb // BLOCKS_PER_TRIP, scores, 0)
        lax.fori_loop(0, d * nb // BLOCKS_PER_TRIP, values, 0)

    for gi, (_, d) in enumerate(DILATED_GROUPS):
        deinterleave(gi, d)
    for gi, (_, d) in enumerate(DILATED_GROUPS):
        run_group(gi, d)

    def merge(c, carry):
        rws = pl.ds(pl.multiple_of(c * Q_BLOCK, Q_BLOCK), Q_BLOCK)
        stats = [ref[rws, :] for ref in stat_refs]
        swapped = [pltpu.roll(st, half, axis=1) for st in stats]
        ms = [jnp.where(lane < half, st, sw) for st, sw in zip(stats, swapped)]
        ls = [jnp.where(lane < half, sw, st) for st, sw in zip(stats, swapped)]
        m_all = functools.reduce(jnp.maximum, ms)
        num = None
        den = None
        for gi in range(n_groups):
            w = jnp.exp2(ms[gi] - m_all)
            num_g = w * acc_refs[gi][rws, :]
            den_g = w * ls[gi]
            num = num_g if num is None else num + num_g
            den = den_g if den is None else den + den_g
        o_ref[rws, :] = ((num / den) * _silu(g_ref[rws, :])).astype(o_ref.dtype)
        return carry

    lax.fori_loop(0, t // Q_BLOCK, merge, 0, unroll=2)


def _attn_prompt(proj, out_dtype):
    b, t, _ = proj.shape
    n_groups = len(DILATED_GROUPS)
    for window, dil in DILATED_GROUPS:
        assert window // dil == Q_BLOCK and (t // dil) % DEINT_CHUNK == 0
    blk = (None, t, HEAD_DIM)
    return pl.pallas_call(
        _attn_prompt_kernel,
        grid=(b, N_HEADS),
        in_specs=[
            pl.BlockSpec(blk, lambda i, h: (i, 0, h)),
            pl.BlockSpec(blk, lambda i, h: (i, 0, N_HEADS + h)),
            pl.BlockSpec(blk, lambda i, h: (i, 0, 2 * N_HEADS + h)),
            pl.BlockSpec(blk, lambda i, h: (i, 0, 3 * N_HEADS + h)),
        ],
        out_specs=pl.BlockSpec(blk, lambda i, h: (i, 0, h)),
        out_shape=jax.ShapeDtypeStruct((b, t, D_ATTN), out_dtype),
        scratch_shapes=(
            [pltpu.VMEM((n_groups, t, HEAD_DIM), BF16), pltpu.VMEM((n_groups, t, 2 * HEAD_DIM), BF16)]
            + [pltpu.VMEM((2, Q_BLOCK, 2 * Q_BLOCK), F32)]
            + [pltpu.VMEM((t // Q_BLOCK, Q_BLOCK, 2 * Q_BLOCK), F32)]
            + [pltpu.VMEM((t, HEAD_DIM), F32)] * (2 * n_groups)),
        compiler_params=_params("parallel", "arbitrary"),
        name="attn_prompt",
    )(proj, proj, proj, proj)


def _group_count(dist):
    cnt = jnp.zeros(dist.shape, F32)
    for window, dil in DILATED_GROUPS:
        hit = (dist >= 0) & (dist <= window) & ((dist & (dil - 1)) == 0)
        cnt = cnt + hit.astype(F32)
    return cnt


def _attn_sample_kernel(q_ref, g_ref, kn_ref, vn_ref, kc_ref, vc_ref, o_ref, m_ref, l_ref, acc_ref):
    j = pl.program_id(1)
    nj = pl.num_programs(1)
    tk = kc_ref.shape[0]
    n_buf = tk * nj
    n_rows = q_ref.shape[0]
    nt_dims = (((1,), (1,)), ((), ()))
    head_bits = N_HEADS.bit_length() - 1

    @pl.when(j == 0)
    def _():
        m_ref[...] = jnp.full(m_ref.shape, NEG_INF, F32)
        l_ref[...] = jnp.zeros(l_ref.shape, F32)
        acc_ref[...] = jnp.zeros(acc_ref.shape, F32)

    q = (q_ref[...] * ATTN_SCALE).astype(BF16)

    def accumulate(k2, v2, key_pos0):
        n_cols = k2.shape[0]
        row = lax.broadcasted_iota(jnp.int32, (n_rows, n_cols), 0)
        col = lax.broadcasted_iota(jnp.int32, (n_rows, n_cols), 1)
        same_head = (row & (N_HEADS - 1)) == (col & (N_HEADS - 1))
        dist = (n_buf + (row >> head_bits)) - (key_pos0 + (col >> head_bits))
        cnt = jnp.where(same_head, _group_count(dist), 0.0)
        s = lax.dot_general(q, k2.astype(BF16), nt_dims, preferred_element_type=F32)
        s = jnp.where(cnt > 0, s, NEG_INF)
        m_prev = m_ref[...]
        m_new = jnp.maximum(m_prev, jnp.max(s, axis=-1, keepdims=True))
        alpha = jnp.exp(m_prev - m_new)
        p = cnt * jnp.exp(s - m_new)
        l_ref[...] = alpha * l_ref[...] + jnp.sum(p, axis=-1, keepdims=True)
        acc_ref[...] = alpha * acc_ref[...] + jnp.dot(p.astype(BF16), v2.astype(BF16),
                                                      preferred_element_type=F32)
        m_ref[...] = m_new

    accumulate(kc_ref[...].reshape(tk * N_HEADS, HEAD_DIM), vc_ref[...].reshape(tk * N_HEADS, HEAD_DIM),
               j * tk)

    @pl.when(j == nj - 1)
    def _():
        accumulate(kn_ref[...], vn_ref[...], n_buf)
        o_ref[...] = (acc_ref[...] / l_ref[...]) * _silu(g_ref[...])


def _attn_sample(q2, g2, kn2, vn2, cache_k, cache_v, layer, tk):
    b, n_rows, _ = q2.shape
    n_buf = cache_k.shape[2]
    rblk = (None, n_rows, HEAD_DIM)
    cblk = (None, None, tk, N_HEADS, HEAD_DIM)
    return pl.pallas_call(
        _attn_sample_kernel,
        grid=(b, n_buf // tk),
        in_specs=[pl.BlockSpec(rblk, lambda i, j: (i, 0, 0))] * 4
        + [pl.BlockSpec(cblk, lambda i, j: (layer, i, j, 0, 0))] * 2,
        out_specs=pl.BlockSpec(rblk, lambda i, j: (i, 0, 0)),
        out_shape=jax.ShapeDtypeStruct((b, n_rows, HEAD_DIM), F32),
        scratch_shapes=[pltpu.VMEM((n_rows, 1), F32), pltpu.VMEM((n_rows, 1), F32),
                        pltpu.VMEM((n_rows, HEAD_DIM), F32)],
        compiler_params=_params("parallel", "arbitrary"),
        name="attn_sample",
    )(q2, g2, kn2, vn2, cache_k, cache_v)


def _cache_roll_kernel(ck_ref, cv_ref, kn_ref, vn_ref, ko_ref, vo_ref, sem, *, n_keep):
    n_layers, n_batch, n_buf = ck_ref.shape[:3]
    n_new = kn_ref.shape[2]
    drop = n_buf + n_new - n_keep
    copies = []
    for cache, new, out in ((ck_ref, kn_ref, ko_ref), (cv_ref, vn_ref, vo_ref)):
        for l in range(n_layers):
            for b in range(n_batch):
                copies.append((cache.at[l, b, pl.ds(drop, n_buf - drop)], out.at[l, b, pl.ds(0, n_buf - drop)]))
                copies.append((new.at[l, b], out.at[l, b, pl.ds(n_buf - drop, n_new)]))
    dmas = [pltpu.make_async_copy(src, dst, sem.at[n]) for n, (src, dst) in enumerate(copies)]
    for dma in dmas:
        dma.start()
    for dma in dmas:
        dma.wait()


def _cache_roll(cache_k, cache_v, k_new, v_new, n_keep):
    n_layers, n_batch, n_buf = cache_k.shape[:3]
    assert n_keep <= n_buf
    out = jax.ShapeDtypeStruct((n_layers, n_batch, n_keep) + cache_k.shape[3:], cache_k.dtype)
    any_spec = pl.BlockSpec(memory_space=pl.ANY)
    return pl.pallas_call(
        functools.partial(_cache_roll_kernel, n_keep=n_keep),
        in_specs=[any_spec] * 4,
        out_specs=[any_spec] * 2,
        out_shape=[out, out],
        scratch_shapes=[pltpu.SemaphoreType.DMA((4 * n_layers * n_batch,))],
        name="cache_roll",
    )(cache_k, cache_v, k_new, v_new)


def _cast_kernel(x_ref, o_ref):
    o_ref[...] = x_ref[...].astype(o_ref.dtype)


def _cast_bf16(w, rows):
    n_layers, r, c = w.shape
    return pl.pallas_call(
        _cast_kernel,
        grid=(n_layers, r // rows),
        in_specs=[pl.BlockSpec((1, rows, c), lambda l, i: (l, i, 0))],
        out_specs=pl.BlockSpec((1, rows, c), lambda l, i: (l, i, 0)),
        out_shape=jax.ShapeDtypeStruct(w.shape, BF16),
        compiler_params=_params("parallel", "parallel"),
        name="cast_bf16",
    )(w)


def kernel(x_prompt, x_sample, state_pool, state_conv, cache_k, cache_v, norm_w, final_norm_w,
           w_in_ab, pool_lin, pool_scale, conv_w, w_out_ab, w_in_c, w_out_c):
    bp, t, d = x_prompt.shape
    bs, ts, _ = x_sample.shape
    ts_pad = -(-ts // SUBLANES) * SUBLANES
    n_buf = cache_k.shape[2]
    n_keep_p = min(MAX_WINDOW, t)

    hp = x_prompt.reshape(bp * t, d)
    hs = jnp.pad(x_sample, ((0, 0), (0, ts_pad - ts), (0, 0))).reshape(bs * ts_pad, d)
    zero_pool = jnp.zeros((bp, POOL_HIST, D_POOL), F32)
    zero_conv = jnp.zeros((bp, CONV_HIST, D_CONV), F32)

    tm_p, tn = 1024, 1024
    tm_s = bs * ts_pad
    cast_rows = 256

    w_in_ab = _cast_bf16(w_in_ab, cast_rows)
    w_out_ab = _cast_bf16(w_out_ab, cast_rows)
    w_in_c = _cast_bf16(w_in_c, cast_rows)
    w_out_c = _cast_bf16(w_out_c, cast_rows)
    pool_lin = _cast_bf16(pool_lin.reshape(pool_lin.shape[0], -1, D_POOL_GROUP), cast_rows).reshape(pool_lin.shape)

    def heads(a):
        return a.reshape(a.shape[:-1] + (N_HEADS, HEAD_DIM))

    def head_rows(a):
        return a.reshape(a.shape[0], a.shape[1] * N_HEADS, HEAD_DIM)

    pool_p, pool_s, conv_p, conv_s = [], [], [], []
    k_p, v_p, k_new, v_new = [], [], [], []
    for l in range(DEPTH):
        i = l // 2
        nw = norm_w[l].reshape(1, d)
        if l % 2 == 0:
            pscale = pool_scale[i].reshape(1, D_POOL)
            proj_p = _norm_matmul(hp, nw, w_in_ab, i, tm_p, tn).reshape(bp, t, -1)
            proj_s = _norm_matmul(hs, nw, w_in_ab, i, tm_s, tn).reshape(bs, ts_pad, -1)
            mix_p, pp, cp = _ab_mix(proj_p, zero_pool, zero_conv, pool_lin[i], pscale, conv_w[i],
                                    256, 0, t, BF16)
            mix_s, ps, cs = _ab_mix(proj_s, state_pool[i], state_conv[i], pool_lin[i], pscale, conv_w[i],
                                    ts_pad, PAST_LEN, ts, F32)
            hp = _matmul_residual(mix_p.reshape(bp * t, -1), w_out_ab, i, hp, tm_p, tn)
            hs = _matmul_residual(mix_s.reshape(bs * ts_pad, -1), w_out_ab, i, hs, tm_s, tn)
            pool_p.append(pp)
            pool_s.append(ps)
            conv_p.append(cp)
            conv_s.append(cs)
        else:
            proj_p = _norm_matmul(hp, nw, w_in_c, i, tm_p, tn).reshape(bp, t, -1)
            proj_s = _norm_matmul(hs, nw, w_in_c, i, tm_s, tn).reshape(bs, ts_pad, -1)[:, :ts]
            og_p = _attn_prompt(proj_p, BF16)
            q_s, kn_s, vn_s, g_s = (head_rows(proj_s[..., c * D_ATTN:(c + 1) * D_ATTN]) for c in range(4))
            og_s = _attn_sample(q_s, g_s, kn_s, vn_s, cache_k, cache_v, i, SAMPLE_KEY_TILE)
            og_s = jnp.pad(og_s.reshape(bs, ts, D_ATTN), ((0, 0), (0, ts_pad - ts), (0, 0)))
            hp = _matmul_residual(og_p.reshape(bp * t, -1), w_out_c, i, hp, tm_p, tn)
            hs = _matmul_residual(og_s.reshape(bs * ts_pad, -1), w_out_c, i, hs, tm_s, tn)
            k_p.append(heads(proj_p[:, t - n_keep_p:, D_ATTN:2 * D_ATTN]))
            v_p.append(heads(proj_p[:, t - n_keep_p:, 2 * D_ATTN:3 * D_ATTN]))
            k_new.append(kn_s.reshape(bs, ts, N_HEADS, HEAD_DIM))
            v_new.append(vn_s.reshape(bs, ts, N_HEADS, HEAD_DIM))

    k_s, v_s = _cache_roll(cache_k, cache_v, jnp.stack(k_new), jnp.stack(v_new), min(MAX_WINDOW, n_buf + ts))

    fw = final_norm_w.reshape(1, d)
    y_prompt = _rmsnorm(hp, fw, 512).reshape(bp, t, d)
    y_sample = _rmsnorm(hs, fw, tm_s).reshape(bs, ts_pad, d)[:, :ts]
    return (y_prompt, y_sample, jnp.stack(pool_p), jnp.stack(pool_s), jnp.stack(conv_p), jnp.stack(conv_s),
            jnp.stack(k_p), k_s, jnp.stack(v_p), v_s)
```

```python
import functools

import jax
import jax.numpy as jnp
from jax import lax
from jax.experimental import pallas as pl
from jax.experimental.pallas import tpu as pltpu

F32 = jnp.float32
BF16 = jnp.bfloat16

D_MODEL = 2048
DEPTH = 4
PAST_LEN = 16384
D_POOL = D_MODEL // 2
POOL_WINDOWS = (2, 4, 8, 16)
D_POOL_GROUP = D_POOL // len(POOL_WINDOWS)
POOL_HIST = max(POOL_WINDOWS) - 1
D_CONV = D_MODEL // 2
CONV_WIDTH = 3
CONV_HIST = CONV_WIDTH - 1
HEAD_DIM = 128
N_HEADS = D_MODEL // HEAD_DIM
D_ATTN = N_HEADS * HEAD_DIM
DILATED_GROUPS = ((128, 1), (512, 4), (2048, 16))
MAX_WINDOW = max(w for w, _ in DILATED_GROUPS)
Q_BLOCK = 128
ATTN_SCALE = HEAD_DIM ** -0.5
LOG2_E = 1.4426950408889634
RMS_EPS = 1e-6
NEG_INF = -1e30

SUBLANES = 8
POOL_PAD = 16
CONV_PAD = 8
DEINT_CHUNK = 2 * Q_BLOCK
SAMPLE_KEY_TILE = 512
ROLL_TILE = 256
BLOCKS_PER_TRIP = 8
VMEM_LIMIT = 48 * 1024 * 1024


def _silu(x):
    return x * jax.nn.sigmoid(x)


def _params(*semantics):
    return pltpu.CompilerParams(dimension_semantics=semantics, vmem_limit_bytes=VMEM_LIMIT)


def _norm_matmul_kernel(x_ref, nw_ref, w_ref, o_ref, xn_ref):
    @pl.when(pl.program_id(1) == 0)
    def _():
        x = x_ref[...]
        ms = jnp.mean(x * x, axis=-1, keepdims=True)
        xn_ref[...] = (x * lax.rsqrt(ms + RMS_EPS) * nw_ref[...]).astype(BF16)

    o_ref[...] = jnp.dot(xn_ref[...], w_ref[...], preferred_element_type=F32)


def _norm_matmul(x, nw, w, layer, tm, tn):
    m, d = x.shape
    n = w.shape[2]
    return pl.pallas_call(
        _norm_matmul_kernel,
        grid=(m // tm, n // tn),
        in_specs=[
            pl.BlockSpec((tm, d), lambda i, j: (i, 0)),
            pl.BlockSpec((1, d), lambda i, j: (0, 0)),
            pl.BlockSpec((None, d, tn), lambda i, j: (layer, 0, j)),
        ],
        out_specs=pl.BlockSpec((tm, tn), lambda i, j: (i, j)),
        out_shape=jax.ShapeDtypeStruct((m, n), F32),
        scratch_shapes=[pltpu.VMEM((tm, d), BF16)],
        compiler_params=_params("parallel", "arbitrary"),
        name="norm_matmul",
    )(x, nw, w)


def _matmul_residual_kernel(a_ref, w_ref, h_ref, o_ref):
    o_ref[...] = h_ref[...] + jnp.dot(a_ref[...].astype(BF16), w_ref[...], preferred_element_type=F32)


def _matmul_residual(a, w, layer, h, tm, tn):
    m, k = a.shape
    n = w.shape[2]
    return pl.pallas_call(
        _matmul_residual_kernel,
        grid=(m // tm, n // tn),
        in_specs=[
            pl.BlockSpec((tm, k), lambda i, j: (i, 0)),
            pl.BlockSpec((None, k, tn), lambda i, j: (layer, 0, j)),
            pl.BlockSpec((tm, tn), lambda i, j: (i, j)),
        ],
        out_specs=pl.BlockSpec((tm, tn), lambda i, j: (i, j)),
        out_shape=jax.ShapeDtypeStruct((m, n), F32),
        compiler_params=_params("parallel", "arbitrary"),
        name="matmul_residual",
    )(a, w, h)


def _rmsnorm_kernel(x_ref, nw_ref, o_ref):
    x = x_ref[...]
    ms = jnp.mean(x * x, axis=-1, keepdims=True)
    o_ref[...] = x * lax.rsqrt(ms + RMS_EPS) * nw_ref[...]


def _rmsnorm(x, nw, tm):
    m, d = x.shape
    return pl.pallas_call(
        _rmsnorm_kernel,
        grid=(m // tm,),
        in_specs=[pl.BlockSpec((tm, d), lambda i: (i, 0)), pl.BlockSpec((1, d), lambda i: (0, 0))],
        out_specs=pl.BlockSpec((tm, d), lambda i: (i, 0)),
        out_shape=jax.ShapeDtypeStruct((m, d), F32),
        compiler_params=_params("parallel"),
        name="final_rmsnorm",
    )(x, nw)


def _ab_mix_kernel(proj_ref, ph_ref, ch_ref, plin_ref, pscale_ref, cw_ref,
                   mix_ref, ptail_ref, ctail_ref, uext_ref, zext_ref, *, pos0, t_last):
    j = pl.program_id(1)
    tm = proj_ref.shape[1]
    gw = D_POOL_GROUP

    @pl.when(j == 0)
    def _():
        uext_ref[POOL_PAD - POOL_HIST:POOL_PAD, :] = ph_ref[0]
        zext_ref[CONV_PAD - CONV_HIST:CONV_PAD, :] = ch_ref[0]

    @pl.when(j > 0)
    def _():
        uext_ref[0:POOL_PAD, :] = uext_ref[tm:tm + POOL_PAD, :]
        zext_ref[0:CONV_PAD, :] = zext_ref[tm:tm + CONV_PAD, :]

    uext_ref[POOL_PAD:POOL_PAD + tm, :] = proj_ref[0, :, 0:D_POOL]
    c_off = 2 * D_POOL + D_CONV
    zext_ref[CONV_PAD:CONV_PAD + tm, :] = (
        proj_ref[0, :, c_off:c_off + D_CONV] * proj_ref[0, :, c_off + D_CONV:c_off + 2 * D_CONV])

    pos = (pos0 + j * tm + lax.broadcasted_iota(jnp.int32, (tm, 1), 0)).astype(F32)

    for g, k in enumerate(POOL_WINDOWS):
        c0 = g * gw
        u_g = uext_ref[POOL_PAD:POOL_PAD + tm, c0:c0 + gw]
        s = u_g
        for i in range(1, k):
            s = s + uext_ref[POOL_PAD - i:POOL_PAD - i + tm, c0:c0 + gw]
        cnt = jnp.minimum(float(k), pos + 1.0)
        pooled = s / cnt - u_g
        a = jnp.dot(pooled.astype(BF16), plin_ref[g], preferred_element_type=F32)
        a = a * pscale_ref[:, c0:c0 + gw]
        gate = proj_ref[0, :, D_POOL + c0:D_POOL + c0 + gw]
        mix_ref[0, :, c0:c0 + gw] = (a * _silu(gate)).astype(mix_ref.dtype)

    for c in range(D_CONV // gw):
        c0 = c * gw
        conv = zext_ref[CONV_PAD - 2:CONV_PAD - 2 + tm, c0:c0 + gw] * cw_ref[0:1, c0:c0 + gw]
        conv = conv + zext_ref[CONV_PAD - 1:CONV_PAD - 1 + tm, c0:c0 + gw] * cw_ref[1:2, c0:c0 + gw]
        conv = conv + zext_ref[CONV_PAD:CONV_PAD + tm, c0:c0 + gw] * cw_ref[2:3, c0:c0 + gw]
        b_gate = proj_ref[0, :, 2 * D_POOL + c0:2 * D_POOL + c0 + gw]
        gate = proj_ref[0, :, 2 * D_POOL + 3 * D_CONV + c0:2 * D_POOL + 3 * D_CONV + c0 + gw]
        mix_ref[0, :, D_POOL + c0:D_POOL + c0 + gw] = (b_gate * conv * _silu(gate)).astype(mix_ref.dtype)

    @pl.when(j == pl.num_programs(1) - 1)
    def _():
        ptail_ref[0] = uext_ref[POOL_PAD + t_last - POOL_HIST:POOL_PAD + t_last, :]
        ctail_ref[0] = zext_ref[CONV_PAD + t_last - CONV_HIST:CONV_PAD + t_last, :]


def _ab_mix(proj, pool_hist, conv_hist, plin, pscale, cw, tm, pos0, t_valid, mix_dtype):
    b, t, n = proj.shape
    nj = t // tm
    t_last = t_valid - (nj - 1) * tm
    kern = functools.partial(_ab_mix_kernel, pos0=pos0, t_last=t_last)
    return pl.pallas_call(
        kern,
        grid=(b, nj),
        in_specs=[
            pl.BlockSpec((1, tm, n), lambda i, j: (i, j, 0)),
            pl.BlockSpec((1, POOL_HIST, D_POOL), lambda i, j: (i, 0, 0)),
            pl.BlockSpec((1, CONV_HIST, D_CONV), lambda i, j: (i, 0, 0)),
            pl.BlockSpec(plin.shape, lambda i, j: (0, 0, 0)),
            pl.BlockSpec((1, D_POOL), lambda i, j: (0, 0)),
            pl.BlockSpec((CONV_WIDTH, D_CONV), lambda i, j: (0, 0)),
        ],
        out_specs=[
            pl.BlockSpec((1, tm, D_POOL + D_CONV), lambda i, j: (i, j, 0)),
            pl.BlockSpec((1, POOL_HIST, D_POOL), lambda i, j: (i, 0, 0)),
            pl.BlockSpec((1, CONV_HIST, D_CONV), lambda i, j: (i, 0, 0)),
        ],
        out_shape=[
            jax.ShapeDtypeStruct((b, t, D_POOL + D_CONV), mix_dtype),
            jax.ShapeDtypeStruct((b, POOL_HIST, D_POOL), F32),
            jax.ShapeDtypeStruct((b, CONV_HIST, D_CONV), F32),
        ],
        scratch_shapes=[pltpu.VMEM((tm + POOL_PAD, D_POOL), F32), pltpu.VMEM((tm + CONV_PAD, D_CONV), F32)],
        compiler_params=_params("parallel", "arbitrary"),
        name="ab_mix",
    )(proj, pool_hist, conv_hist, plin, pscale, cw)


def _rows(start, size, stride):
    if stride == 1:
        return pl.ds(start, size)
    return pl.ds(start, size, stride=stride)


def _attn_prompt_kernel(q_ref, k_ref, v_ref, g_ref, o_ref, kd_ref, vd_ref, bias_ref, s_ref, *state_refs):
    t = q_ref.shape[0]
    n_groups = len(DILATED_GROUPS)
    acc_refs = state_refs[:n_groups]
    stat_refs = state_refs[n_groups:]
    nt_dims = (((1,), (1,)), ((), ()))
    half = HEAD_DIM // 2

    delta = (lax.broadcasted_iota(jnp.int32, (Q_BLOCK, 2 * Q_BLOCK), 0)
             - lax.broadcasted_iota(jnp.int32, (Q_BLOCK, 2 * Q_BLOCK), 1))
    for sel in range(2):
        dist = delta + sel * Q_BLOCK
        bias_ref[sel] = jnp.where((dist >= 0) & (dist <= Q_BLOCK), 0.0, NEG_INF)
    lane = lax.broadcasted_iota(jnp.int32, (Q_BLOCK, HEAD_DIM), 1)
    vd_ref[:, :, HEAD_DIM:] = jnp.ones((n_groups, t, HEAD_DIM), BF16)

    def deinterleave(gi, d):
        stream_len = t // d

        def body(c, carry):
            dst = pl.multiple_of(c * DEINT_CHUNK, DEINT_CHUNK)
            r = dst // stream_len
            m0 = dst % stream_len
            src = _rows(r + d * m0, DEINT_CHUNK, d)
            kd_ref[gi, pl.ds(dst, DEINT_CHUNK), :] = k_ref[src, :].astype(BF16)
            vd_ref[gi, pl.ds(dst, DEINT_CHUNK), 0:HEAD_DIM] = v_ref[src, :].astype(BF16)
            return carry

        lax.fori_loop(0, t // DEINT_CHUNK, body, 0, unroll=2)

    def run_group(gi, d):
        stream_len = t // d
        nb = stream_len // Q_BLOCK

        def block_rows(idx):
            r = idx // nb
            mb = idx % nb
            kb = jnp.maximum(mb - 1, 0)
            q_rows = _rows(r + d * Q_BLOCK * mb, Q_BLOCK, d)
            k_rows = pl.ds(pl.multiple_of(r * stream_len + kb * Q_BLOCK, Q_BLOCK), 2 * Q_BLOCK)
            return q_rows, k_rows, mb - kb

        def scores(it, carry):
            s = []
            for u in range(BLOCKS_PER_TRIP):
                q_rows, k_rows, sel = block_rows(it * BLOCKS_PER_TRIP + u)
                q = (q_ref[q_rows, :] * (ATTN_SCALE * LOG2_E)).astype(BF16)
                s.append(lax.dot_general(q, kd_ref[gi, k_rows, :], nt_dims, preferred_element_type=F32)
                         + bias_ref[sel])
            for u in range(BLOCKS_PER_TRIP):
                s_ref[it * BLOCKS_PER_TRIP + u] = s[u]
            return carry

        def values(it, carry):
            rows, s, v = [], [], []
            for u in range(BLOCKS_PER_TRIP):
                q_rows, k_rows, _ = block_rows(it * BLOCKS_PER_TRIP + u)
                rows.append(q_rows)
                s.append(s_ref[it * BLOCKS_PER_TRIP + u])
                v.append(vd_ref[gi, k_rows, :])
            acc, stat = [], []
            for u in range(BLOCKS_PER_TRIP):
                m_blk = jnp.max(s[u], axis=-1, keepdims=True)
                p = jnp.exp2(s[u] - m_blk).astype(BF16)
                pv = jnp.dot(p, v[u], preferred_element_type=F32)
                acc.append(pv[:, 0:HEAD_DIM])
                stat.append(jnp.where(lane < half, m_blk, pv[:, HEAD_DIM:]))
            for u in range(BLOCKS_PER_TRIP):
                acc_refs[gi][rows[u], :] = acc[u]
                stat_refs[gi][rows[u], :] = stat[u]
            return carry

        lax.fori_loop(0, d * nb // BLOCKS_PER_TRIP, scores, 0)
        lax.fori_loop(0, d * nb // BLOCKS_PER_TRIP, values, 0)

    for gi, (_, d) in enumerate(DILATED_GROUPS):
        deinterleave(gi, d)
    for gi, (_, d) in enumerate(DILATED_GROUPS):
        run_group(gi, d)

    def merge(c, carry):
        rws = pl.ds(pl.multiple_of(c * Q_BLOCK, Q_BLOCK), Q_BLOCK)
        stats = [ref[rws, :] for ref in stat_refs]
        swapped = [pltpu.roll(st, half, axis=1) for st in stats]
        ms = [jnp.where(lane < half, st, sw) for st, sw in zip(stats, swapped)]
        ls = [jnp.where(lane < half, sw, st) for st, sw in zip(stats, swapped)]
        m_all = functools.reduce(jnp.maximum, ms)
        num = None
        den = None
        for gi in range(n_groups):
            w = jnp.exp2(ms[gi] - m_all)
            num_g = w * acc_refs[gi][rws, :]
            den_g = w * ls[gi]
            num = num_g if num is None else num + num_g
            den = den_g if den is None else den + den_g
        o_ref[rws, :] = ((num / den) * _silu(g_ref[rws, :])).astype(o_ref.dtype)
        return carry

    lax.fori_loop(0, t // Q_BLOCK, merge, 0, unroll=2)


def _attn_prompt(proj, out_dtype):
    b, t, _ = proj.shape
    n_groups = len(DILATED_GROUPS)
    for window, dil in DILATED_GROUPS:
        assert window // dil == Q_BLOCK and (t // dil) % DEINT_CHUNK == 0
    blk = (None, t, HEAD_DIM)
    return pl.pallas_call(
        _attn_prompt_kernel,
        grid=(b, N_HEADS),
        in_specs=[
            pl.BlockSpec(blk, lambda i, h: (i, 0, h)),
            pl.BlockSpec(blk, lambda i, h: (i, 0, N_HEADS + h)),
            pl.BlockSpec(blk, lambda i, h: (i, 0, 2 * N_HEADS + h)),
            pl.BlockSpec(blk, lambda i, h: (i, 0, 3 * N_HEADS + h)),
        ],
        out_specs=pl.BlockSpec(blk, lambda i, h: (i, 0, h)),
        out_shape=jax.ShapeDtypeStruct((b, t, D_ATTN), out_dtype),
        scratch_shapes=(
            [pltpu.VMEM((n_groups, t, HEAD_DIM), BF16), pltpu.VMEM((n_groups, t, 2 * HEAD_DIM), BF16)]
            + [pltpu.VMEM((2, Q_BLOCK, 2 * Q_BLOCK), F32)]
            + [pltpu.VMEM((t // Q_BLOCK, Q_BLOCK, 2 * Q_BLOCK), F32)]
            + [pltpu.VMEM((t, HEAD_DIM), F32)] * (2 * n_groups)),
        compiler_params=_params("parallel", "arbitrary"),
        name="attn_prompt",
    )(proj, proj, proj, proj)


def _group_count(dist):
    cnt = jnp.zeros(dist.shape, F32)
    for window, dil in DILATED_GROUPS:
        hit = (dist >= 0) & (dist <= window) & ((dist & (dil - 1)) == 0)
        cnt = cnt + hit.astype(F32)
    return cnt


def _attn_sample_kernel(q_ref, g_ref, kn_ref, vn_ref, kc_ref, vc_ref, o_ref, m_ref, l_ref, acc_ref):
    j = pl.program_id(1)
    nj = pl.num_programs(1)
    tk = kc_ref.shape[0]
    n_buf = tk * nj
    n_rows = q_ref.shape[0]
    nt_dims = (((1,), (1,)), ((), ()))
    head_bits = N_HEADS.bit_length() - 1

    @pl.when(j == 0)
    def _():
        m_ref[...] = jnp.full(m_ref.shape, NEG_INF, F32)
        l_ref[...] = jnp.zeros(l_ref.shape, F32)
        acc_ref[...] = jnp.zeros(acc_ref.shape, F32)

    q = (q_ref[...] * ATTN_SCALE).astype(BF16)

    def accumulate(k2, v2, key_pos0):
        n_cols = k2.shape[0]
        row = lax.broadcasted_iota(jnp.int32, (n_rows, n_cols), 0)
        col = lax.broadcasted_iota(jnp.int32, (n_rows, n_cols), 1)
        same_head = (row & (N_HEADS - 1)) == (col & (N_HEADS - 1))
        dist = (n_buf + (row >> head_bits)) - (key_pos0 + (col >> head_bits))
        cnt = jnp.where(same_head, _group_count(dist), 0.0)
        s = lax.dot_general(q, k2.astype(BF16), nt_dims, preferred_element_type=F32)
        s = jnp.where(cnt > 0, s, NEG_INF)
        m_prev = m_ref[...]
        m_new = jnp.maximum(m_prev, jnp.max(s, axis=-1, keepdims=True))
        alpha = jnp.exp(m_prev - m_new)
        p = cnt * jnp.exp(s - m_new)
        l_ref[...] = alpha * l_ref[...] + jnp.sum(p, axis=-1, keepdims=True)
        acc_ref[...] = alpha * acc_ref[...] + jnp.dot(p.astype(BF16), v2.astype(BF16),
                                                      preferred_element_type=F32)
        m_ref[...] = m_new

    accumulate(kc_ref[...].reshape(tk * N_HEADS, HEAD_DIM), vc_ref[...].reshape(tk * N_HEADS, HEAD_DIM),
               j * tk)

    @pl.when(j == nj - 1)
    def _():
        accumulate(kn_ref[...], vn_ref[...], n_buf)
        o_ref[...] = (acc_ref[...] / l_ref[...]) * _silu(g_ref[...])


def _attn_sample(q2, g2, kn2, vn2, cache_k, cache_v, layer, tk):
    b, n_rows, _ = q2.shape
    n_buf = cache_k.shape[2]
    rblk = (None, n_rows, HEAD_DIM)
    cblk = (None, None, tk, N_HEADS, HEAD_DIM)
    return pl.pallas_call(
        _attn_sample_kernel,
        grid=(b, n_buf // tk),
        in_specs=[pl.BlockSpec(rblk, lambda i, j: (i, 0, 0))] * 4
        + [pl.BlockSpec(cblk, lambda i, j: (layer, i, j, 0, 0))] * 2,
        out_specs=pl.BlockSpec(rblk, lambda i, j: (i, 0, 0)),
        out_shape=jax.ShapeDtypeStruct((b, n_rows, HEAD_DIM), F32),
        scratch_shapes=[pltpu.VMEM((n_rows, 1), F32), pltpu.VMEM((n_rows, 1), F32),
                        pltpu.VMEM((n_rows, HEAD_DIM), F32)],
        compiler_params=_params("parallel", "arbitrary"),
        name="attn_sample",
    )(q2, g2, kn2, vn2, cache_k, cache_v)


def _cache_roll_kernel(ck_ref, ck_next_ref, kn_ref, cv_ref, cv_next_ref, vn_ref, ko_ref, vo_ref):
    j = pl.program_id(2)
    tk = ck_ref.shape[0]
    n_new = kn_ref.shape[0]
    for cur, nxt, new, out in ((ck_ref, ck_next_ref, kn_ref, ko_ref), (cv_ref, cv_next_ref, vn_ref, vo_ref)):
        out[0:tk - n_new] = cur[n_new:tk]

        @pl.when(j < pl.num_programs(2) - 1)
        def _(nxt=nxt, out=out):
            out[tk - n_new:tk] = nxt[0:n_new]

        @pl.when(j == pl.num_programs(2) - 1)
        def _(new=new, out=out):
            out[tk - n_new:tk] = new[...]


def _cache_roll(cache_k, cache_v, k_new, v_new, tk):
    n_layers, n_batch, n_buf, nh, hd = cache_k.shape
    n_new = k_new.shape[2]
    assert n_new <= SUBLANES and tk % SUBLANES == 0 and n_buf % tk == 0
    last_next = n_buf // SUBLANES - 1
    cur_spec = pl.BlockSpec((None, None, tk, nh, hd), lambda l, b, j: (l, b, j, 0, 0))
    next_spec = pl.BlockSpec((None, None, SUBLANES, nh, hd),
                             lambda l, b, j: (l, b, jnp.minimum((j + 1) * (tk // SUBLANES), last_next), 0, 0))
    new_spec = pl.BlockSpec((None, None, n_new, nh, hd), lambda l, b, j: (l, b, 0, 0, 0))
    out = jax.ShapeDtypeStruct(cache_k.shape, cache_k.dtype)
    return pl.pallas_call(
        _cache_roll_kernel,
        grid=(n_layers, n_batch, n_buf // tk),
        in_specs=[cur_spec, next_spec, new_spec] * 2,
        out_specs=[cur_spec] * 2,
        out_shape=[out, out],
        compiler_params=_params("parallel", "parallel", "arbitrary"),
        name="cache_roll",
    )(cache_k, cache_k, k_new, cache_v, cache_v, v_new)


def _cast_kernel(x_ref, o_ref):
    o_ref[...] = x_ref[...].astype(o_ref.dtype)


def _cast_bf16(w, rows):
    n_layers, r, c = w.shape
    return pl.pallas_call(
        _cast_kernel,
        grid=(n_layers, r // rows),
        in_specs=[pl.BlockSpec((1, rows, c), lambda l, i: (l, i, 0))],
        out_specs=pl.BlockSpec((1, rows, c), lambda l, i: (l, i, 0)),
        out_shape=jax.ShapeDtypeStruct(w.shape, BF16),
        compiler_params=_params("parallel", "parallel"),
        name="cast_bf16",
    )(w)


def kernel(x_prompt, x_sample, state_pool, state_conv, cache_k, cache_v, norm_w, final_norm_w,
           w_in_ab, pool_lin, pool_scale, conv_w, w_out_ab, w_in_c, w_out_c):
    bp, t, d = x_prompt.shape
    bs, ts, _ = x_sample.shape
    ts_pad = -(-ts // SUBLANES) * SUBLANES
    n_buf = cache_k.shape[2]
    n_keep_p = min(MAX_WINDOW, t)

    hp = x_prompt.reshape(bp * t, d)
    hs = jnp.pad(x_sample, ((0, 0), (0, ts_pad - ts), (0, 0))).reshape(bs * ts_pad, d)
    zero_pool = jnp.zeros((bp, POOL_HIST, D_POOL), F32)
    zero_conv = jnp.zeros((bp, CONV_HIST, D_CONV), F32)

    tm_p, tn = 1024, 1024
    tm_s = bs * ts_pad
    cast_rows = 256

    w_in_ab = _cast_bf16(w_in_ab, cast_rows)
    w_out_ab = _cast_bf16(w_out_ab, cast_rows)
    w_in_c = _cast_bf16(w_in_c, cast_rows)
    w_out_c = _cast_bf16(w_out_c, cast_rows)
    pool_lin = _cast_bf16(pool_lin.reshape(pool_lin.shape[0], -1, D_POOL_GROUP), cast_rows).reshape(pool_lin.shape)

    def heads(a):
        return a.reshape(a.shape[:-1] + (N_HEADS, HEAD_DIM))

    def head_rows(a):
        return a.reshape(a.shape[0], a.shape[1] * N_HEADS, HEAD_DIM)

    pool_p, pool_s, conv_p, conv_s = [], [], [], []
    k_p, v_p, k_new, v_new = [], [], [], []
    for l in range(DEPTH):
        i = l // 2
        nw = norm_w[l].reshape(1, d)
        if l % 2 == 0:
            pscale = pool_scale[i].reshape(1, D_POOL)
            proj_p = _norm_matmul(hp, nw, w_in_ab, i, tm_p, tn).reshape(bp, t, -1)
            proj_s = _norm_matmul(hs, nw, w_in_ab, i, tm_s, tn).reshape(bs, ts_pad, -1)
            mix_p, pp, cp = _ab_mix(proj_p, zero_pool, zero_conv, pool_lin[i], pscale, conv_w[i],
                                    256, 0, t, BF16)
            mix_s, ps, cs = _ab_mix(proj_s, state_pool[i], state_conv[i], pool_lin[i], pscale, conv_w[i],
                                    ts_pad, PAST_LEN, ts, F32)
            hp = _matmul_residual(mix_p.reshape(bp * t, -1), w_out_ab, i, hp, tm_p, tn)
            hs = _matmul_residual(mix_s.reshape(bs * ts_pad, -1), w_out_ab, i, hs, tm_s, tn)
            pool_p.append(pp)
            pool_s.append(ps)
            conv_p.append(cp)
            conv_s.append(cs)
        else:
            proj_p = _norm_matmul(hp, nw, w_in_c, i, tm_p, tn).reshape(bp, t, -1)
            proj_s = _norm_matmul(hs, nw, w_in_c, i, tm_s, tn).reshape(bs, ts_pad, -1)[:, :ts]
            og_p = _attn_prompt(proj_p, BF16)
            q_s, kn_s, vn_s, g_s = (head_rows(proj_s[..., c * D_ATTN:(c + 1) * D_ATTN]) for c in range(4))
            og_s = _attn_sample(q_s, g_s, kn_s, vn_s, cache_k, cache_v, i, SAMPLE_KEY_TILE)
            og_s = jnp.pad(og_s.reshape(bs, ts, D_ATTN), ((0, 0), (0, ts_pad - ts), (0, 0)))
            hp = _matmul_residual(og_p.reshape(bp * t, -1), w_out_c, i, hp, tm_p, tn)
            hs = _matmul_residual(og_s.reshape(bs * ts_pad, -1), w_out_c, i, hs, tm_s, tn)
            k_p.append(heads(proj_p[:, t - n_keep_p:, D_ATTN:2 * D_ATTN]))
            v_p.append(heads(proj_p[:, t - n_keep_p:, 2 * D_ATTN:3 * D_ATTN]))
            k_new.append(kn_s.reshape(bs, ts, N_HEADS, HEAD_DIM))
            v_new.append(vn_s.reshape(bs, ts, N_HEADS, HEAD_DIM))

    assert min(MAX_WINDOW, n_buf + ts) == n_buf
    k_s, v_s = _cache_roll(cache_k, cache_v, jnp.stack(k_new), jnp.stack(v_new), ROLL_TILE)

    fw = final_norm_w.reshape(1, d)
    y_prompt = _rmsnorm(hp, fw, 512).reshape(bp, t, d)
    y_sample = _rmsnorm(hs, fw, tm_s).reshape(bs, ts_pad, d)[:, :ts]
    return (y_prompt, y_sample, jnp.stack(pool_p), jnp.stack(pool_s), jnp.stack(conv_p), jnp.stack(conv_s),
            jnp.stack(k_p), k_s, jnp.stack(v_p), v_s)
```

```python
import functools

import jax
import jax.numpy as jnp
from jax import lax
from jax.experimental import pallas as pl
from jax.experimental.pallas import tpu as pltpu

F32 = jnp.float32
BF16 = jnp.bfloat16

D_MODEL = 2048
DEPTH = 4
PAST_LEN = 16384
D_POOL = D_MODEL // 2
POOL_WINDOWS = (2, 4, 8, 16)
D_POOL_GROUP = D_POOL // len(POOL_WINDOWS)
POOL_HIST = max(POOL_WINDOWS) - 1
D_CONV = D_MODEL // 2
CONV_WIDTH = 3
CONV_HIST = CONV_WIDTH - 1
HEAD_DIM = 128
N_HEADS = D_MODEL // HEAD_DIM
D_ATTN = N_HEADS * HEAD_DIM
DILATED_GROUPS = ((128, 1), (512, 4), (2048, 16))
MAX_WINDOW = max(w for w, _ in DILATED_GROUPS)
Q_BLOCK = 128
ATTN_SCALE = HEAD_DIM ** -0.5
LOG2_E = 1.4426950408889634
RMS_EPS = 1e-6
NEG_INF = -1e30

SUBLANES = 8
POOL_PAD = 16
CONV_PAD = 8
DEINT_CHUNK = 2 * Q_BLOCK
SAMPLE_KEY_TILE = 512
ROLL_TILE = 256
SCORE_BLOCKS_PER_TRIP = 16
VALUE_BLOCKS_PER_TRIP = 8
MERGE_BLOCKS_PER_TRIP = 8
VMEM_LIMIT = 48 * 1024 * 1024


def _silu(x):
    return x * jax.nn.sigmoid(x)


def _params(*semantics):
    return pltpu.CompilerParams(dimension_semantics=semantics, vmem_limit_bytes=VMEM_LIMIT)


def _norm_matmul_kernel(x_ref, nw_ref, w_ref, o_ref, xn_ref):
    @pl.when(pl.program_id(1) == 0)
    def _():
        x = x_ref[...]
        ms = jnp.mean(x * x, axis=-1, keepdims=True)
        xn_ref[...] = (x * lax.rsqrt(ms + RMS_EPS) * nw_ref[...]).astype(BF16)

    o_ref[...] = jnp.dot(xn_ref[...], w_ref[...], preferred_element_type=F32)


def _norm_matmul(x, nw, w, layer, tm, tn):
    m, d = x.shape
    n = w.shape[2]
    return pl.pallas_call(
        _norm_matmul_kernel,
        grid=(m // tm, n // tn),
        in_specs=[
            pl.BlockSpec((tm, d), lambda i, j: (i, 0)),
            pl.BlockSpec((1, d), lambda i, j: (0, 0)),
            pl.BlockSpec((None, d, tn), lambda i, j: (layer, 0, j)),
        ],
        out_specs=pl.BlockSpec((tm, tn), lambda i, j: (i, j)),
        out_shape=jax.ShapeDtypeStruct((m, n), F32),
        scratch_shapes=[pltpu.VMEM((tm, d), BF16)],
        compiler_params=_params("parallel", "arbitrary"),
        name="norm_matmul",
    )(x, nw, w)


def _matmul_residual_kernel(a_ref, w_ref, h_ref, o_ref):
    o_ref[...] = h_ref[...] + jnp.dot(a_ref[...].astype(BF16), w_ref[...], preferred_element_type=F32)


def _matmul_residual(a, w, layer, h, tm, tn):
    m, k = a.shape
    n = w.shape[2]
    return pl.pallas_call(
        _matmul_residual_kernel,
        grid=(m // tm, n // tn),
        in_specs=[
            pl.BlockSpec((tm, k), lambda i, j: (i, 0)),
            pl.BlockSpec((None, k, tn), lambda i, j: (layer, 0, j)),
            pl.BlockSpec((tm, tn), lambda i, j: (i, j)),
        ],
        out_specs=pl.BlockSpec((tm, tn), lambda i, j: (i, j)),
        out_shape=jax.ShapeDtypeStruct((m, n), F32),
        compiler_params=_params("parallel", "arbitrary"),
        name="matmul_residual",
    )(a, w, h)


def _rmsnorm_kernel(x_ref, nw_ref, o_ref):
    x = x_ref[...]
    ms = jnp.mean(x * x, axis=-1, keepdims=True)
    o_ref[...] = x * lax.rsqrt(ms + RMS_EPS) * nw_ref[...]


def _rmsnorm(x, nw, tm):
    m, d = x.shape
    return pl.pallas_call(
        _rmsnorm_kernel,
        grid=(m // tm,),
        in_specs=[pl.BlockSpec((tm, d), lambda i: (i, 0)), pl.BlockSpec((1, d), lambda i: (0, 0))],
        out_specs=pl.BlockSpec((tm, d), lambda i: (i, 0)),
        out_shape=jax.ShapeDtypeStruct((m, d), F32),
        compiler_params=_params("parallel"),
        name="final_rmsnorm",
    )(x, nw)


def _ab_mix_kernel(proj_ref, ph_ref, ch_ref, plin_ref, pscale_ref, cw_ref,
                   mix_ref, ptail_ref, ctail_ref, uext_ref, zext_ref, *, pos0, t_last):
    j = pl.program_id(1)
    tm = proj_ref.shape[1]
    gw = D_POOL_GROUP

    @pl.when(j == 0)
    def _():
        uext_ref[POOL_PAD - POOL_HIST:POOL_PAD, :] = ph_ref[0]
        zext_ref[CONV_PAD - CONV_HIST:CONV_PAD, :] = ch_ref[0]

    @pl.when(j > 0)
    def _():
        uext_ref[0:POOL_PAD, :] = uext_ref[tm:tm + POOL_PAD, :]
        zext_ref[0:CONV_PAD, :] = zext_ref[tm:tm + CONV_PAD, :]

    uext_ref[POOL_PAD:POOL_PAD + tm, :] = proj_ref[0, :, 0:D_POOL]
    c_off = 2 * D_POOL + D_CONV
    zext_ref[CONV_PAD:CONV_PAD + tm, :] = (
        proj_ref[0, :, c_off:c_off + D_CONV] * proj_ref[0, :, c_off + D_CONV:c_off + 2 * D_CONV])

    pos = (pos0 + j * tm + lax.broadcasted_iota(jnp.int32, (tm, 1), 0)).astype(F32)

    for g, k in enumerate(POOL_WINDOWS):
        c0 = g * gw
        u_g = uext_ref[POOL_PAD:POOL_PAD + tm, c0:c0 + gw]
        s = u_g
        for i in range(1, k):
            s = s + uext_ref[POOL_PAD - i:POOL_PAD - i + tm, c0:c0 + gw]
        cnt = jnp.minimum(float(k), pos + 1.0)
        pooled = s / cnt - u_g
        a = jnp.dot(pooled.astype(BF16), plin_ref[g], preferred_element_type=F32)
        a = a * pscale_ref[:, c0:c0 + gw]
        gate = proj_ref[0, :, D_POOL + c0:D_POOL + c0 + gw]
        mix_ref[0, :, c0:c0 + gw] = (a * _silu(gate)).astype(mix_ref.dtype)

    for c in range(D_CONV // gw):
        c0 = c * gw
        conv = zext_ref[CONV_PAD - 2:CONV_PAD - 2 + tm, c0:c0 + gw] * cw_ref[0:1, c0:c0 + gw]
        conv = conv + zext_ref[CONV_PAD - 1:CONV_PAD - 1 + tm, c0:c0 + gw] * cw_ref[1:2, c0:c0 + gw]
        conv = conv + zext_ref[CONV_PAD:CONV_PAD + tm, c0:c0 + gw] * cw_ref[2:3, c0:c0 + gw]
        b_gate = proj_ref[0, :, 2 * D_POOL + c0:2 * D_POOL + c0 + gw]
        gate = proj_ref[0, :, 2 * D_POOL + 3 * D_CONV + c0:2 * D_POOL + 3 * D_CONV + c0 + gw]
        mix_ref[0, :, D_POOL + c0:D_POOL + c0 + gw] = (b_gate * conv * _silu(gate)).astype(mix_ref.dtype)

    @pl.when(j == pl.num_programs(1) - 1)
    def _():
        ptail_ref[0] = uext_ref[POOL_PAD + t_last - POOL_HIST:POOL_PAD + t_last, :]
        ctail_ref[0] = zext_ref[CONV_PAD + t_last - CONV_HIST:CONV_PAD + t_last, :]


def _ab_mix(proj, pool_hist, conv_hist, plin, pscale, cw, tm, pos0, t_valid, mix_dtype):
    b, t, n = proj.shape
    nj = t // tm
    t_last = t_valid - (nj - 1) * tm
    kern = functools.partial(_ab_mix_kernel, pos0=pos0, t_last=t_last)
    return pl.pallas_call(
        kern,
        grid=(b, nj),
        in_specs=[
            pl.BlockSpec((1, tm, n), lambda i, j: (i, j, 0)),
            pl.BlockSpec((1, POOL_HIST, D_POOL), lambda i, j: (i, 0, 0)),
            pl.BlockSpec((1, CONV_HIST, D_CONV), lambda i, j: (i, 0, 0)),
            pl.BlockSpec(plin.shape, lambda i, j: (0, 0, 0)),
            pl.BlockSpec((1, D_POOL), lambda i, j: (0, 0)),
            pl.BlockSpec((CONV_WIDTH, D_CONV), lambda i, j: (0, 0)),
        ],
        out_specs=[
            pl.BlockSpec((1, tm, D_POOL + D_CONV), lambda i, j: (i, j, 0)),
            pl.BlockSpec((1, POOL_HIST, D_POOL), lambda i, j: (i, 0, 0)),
            pl.BlockSpec((1, CONV_HIST, D_CONV), lambda i, j: (i, 0, 0)),
        ],
        out_shape=[
            jax.ShapeDtypeStruct((b, t, D_POOL + D_CONV), mix_dtype),
            jax.ShapeDtypeStruct((b, POOL_HIST, D_POOL), F32),
            jax.ShapeDtypeStruct((b, CONV_HIST, D_CONV), F32),
        ],
        scratch_shapes=[pltpu.VMEM((tm + POOL_PAD, D_POOL), F32), pltpu.VMEM((tm + CONV_PAD, D_CONV), F32)],
        compiler_params=_params("parallel", "arbitrary"),
        name="ab_mix",
    )(proj, pool_hist, conv_hist, plin, pscale, cw)


def _rows(start, size, stride):
    if stride == 1:
        return pl.ds(start, size)
    return pl.ds(start, size, stride=stride)


def _attn_prompt_kernel(q_ref, k_ref, v_ref, g_ref, o_ref, kd_ref, vd_ref, bias_ref, s_ref, *state_refs):
    t = q_ref.shape[0]
    n_groups = len(DILATED_GROUPS)
    kf_ref, vf_ref = state_refs[0:2]
    acc_refs = (None,) + tuple(state_refs[2:2 + n_groups - 1])
    stat_refs = (None,) + tuple(state_refs[2 + n_groups - 1:])
    nt_dims = (((1,), (1,)), ((), ()))
    half = HEAD_DIM // 2
    lane = lax.broadcasted_iota(jnp.int32, (Q_BLOCK, HEAD_DIM), 1)

    @pl.when((pl.program_id(0) == 0) & (pl.program_id(1) == 0))
    def _():
        delta = (lax.broadcasted_iota(jnp.int32, (Q_BLOCK, 2 * Q_BLOCK), 0)
                 - lax.broadcasted_iota(jnp.int32, (Q_BLOCK, 2 * Q_BLOCK), 1))
        for sel in range(2):
            dist = delta + sel * Q_BLOCK
            bias_ref[sel] = jnp.where((dist >= 0) & (dist <= Q_BLOCK), 0.0, NEG_INF)
        vd_ref[:, :, HEAD_DIM:] = jnp.ones((n_groups, t, HEAD_DIM), BF16)

    def deinterleave(gi, d, d_src, ksrc_ref, vsrc_ref, keep_f32):
        stream_len = t // d
        ratio = d // d_src

        def body(c, carry):
            dst = pl.multiple_of(c * DEINT_CHUNK, DEINT_CHUNK)
            r = dst // stream_len
            m0 = dst % stream_len
            src = _rows((r % d_src) * (t // d_src) + r // d_src + ratio * m0, DEINT_CHUNK, ratio)
            k = ksrc_ref[src, :]
            v = vsrc_ref[src, :]
            kd_ref[gi, pl.ds(dst, DEINT_CHUNK), :] = k.astype(BF16)
            vd_ref[gi, pl.ds(dst, DEINT_CHUNK), 0:HEAD_DIM] = v.astype(BF16)
            if keep_f32:
                kf_ref[pl.ds(dst, DEINT_CHUNK), :] = k
                vf_ref[pl.ds(dst, DEINT_CHUNK), :] = v
            return carry

        lax.fori_loop(0, t // DEINT_CHUNK, body, 0, unroll=2)

    def merged_output(rows, m0, acc0, l0):
        ms, ls, accs = [m0], [l0], [acc0]
        for gi in range(1, n_groups):
            st = stat_refs[gi][rows, :]
            sw = pltpu.roll(st, half, axis=1)
            ms.append(jnp.where(lane < half, st, sw))
            ls.append(jnp.where(lane < half, sw, st))
            accs.append(acc_refs[gi][rows, :])
        m_all = functools.reduce(jnp.maximum, ms)
        num = None
        den = None
        for m_g, l_g, acc_g in zip(ms, ls, accs):
            w = jnp.exp2(m_g - m_all)
            num = w * acc_g if num is None else num + w * acc_g
            den = w * l_g if den is None else den + w * l_g
        g = g_ref[rows, :]
        return ((num * g) / (den * (1.0 + jnp.exp(-g)))).astype(o_ref.dtype)

    def run_group(gi, d):
        stream_len = t // d
        nb = stream_len // Q_BLOCK

        def block_rows(idx):
            r = idx // nb
            mb = idx % nb
            kb = jnp.maximum(mb - 1, 0)
            q_start = r + d * Q_BLOCK * mb
            q_rows = _rows(pl.multiple_of(q_start, Q_BLOCK) if d == 1 else q_start, Q_BLOCK, d)
            k_rows = pl.ds(pl.multiple_of(r * stream_len + kb * Q_BLOCK, Q_BLOCK), 2 * Q_BLOCK)
            return q_rows, k_rows, mb - kb

        def scores(it, carry):
            s = []
            for u in range(SCORE_BLOCKS_PER_TRIP):
                q_rows, k_rows, sel = block_rows(it * SCORE_BLOCKS_PER_TRIP + u)
                q = (q_ref[q_rows, :] * (ATTN_SCALE * LOG2_E)).astype(BF16)
                s.append(lax.dot_general(q, kd_ref[gi, k_rows, :], nt_dims, preferred_element_type=F32)
                         + bias_ref[sel])
            for u in range(SCORE_BLOCKS_PER_TRIP):
                s_ref[it * SCORE_BLOCKS_PER_TRIP + u] = s[u]
            return carry

        n_val = MERGE_BLOCKS_PER_TRIP if gi == 0 else VALUE_BLOCKS_PER_TRIP

        def values(it, carry):
            rows, s, v = [], [], []
            for u in range(n_val):
                q_rows, k_rows, _ = block_rows(it * n_val + u)
                rows.append(q_rows)
                s.append(s_ref[it * n_val + u])
                v.append(vd_ref[gi, k_rows, :])
            outs = []
            for u in range(n_val):
                m_blk = jnp.max(s[u], axis=-1, keepdims=True)
                p = jnp.exp2(s[u] - m_blk).astype(BF16)
                pv = jnp.dot(p, v[u], preferred_element_type=F32)
                acc, l_blk = pv[:, 0:HEAD_DIM], pv[:, HEAD_DIM:]
                if gi == 0:
                    outs.append((merged_output(rows[u], jnp.broadcast_to(m_blk, acc.shape), acc, l_blk),))
                else:
                    outs.append((acc, jnp.where(lane < half, m_blk, l_blk)))
            for u in range(n_val):
                if gi == 0:
                    o_ref[rows[u], :] = outs[u][0]
                else:
                    acc_refs[gi][rows[u], :] = outs[u][0]
                    stat_refs[gi][rows[u], :] = outs[u][1]
            return carry

        lax.fori_loop(0, d * nb // SCORE_BLOCKS_PER_TRIP, scores, 0)
        lax.fori_loop(0, d * nb // n_val, values, 0)

    ksrc_ref, vsrc_ref, d_src = k_ref, v_ref, 1
    for gi, (_, d) in enumerate(DILATED_GROUPS):
        keep_f32 = 0 < gi < n_groups - 1
        deinterleave(gi, d, d_src, ksrc_ref, vsrc_ref, keep_f32)
        if keep_f32:
            ksrc_ref, vsrc_ref, d_src = kf_ref, vf_ref, d
    for gi in reversed(range(n_groups)):
        run_group(gi, DILATED_GROUPS[gi][1])


def _attn_prompt(proj, out_dtype):
    b, t, _ = proj.shape
    n_groups = len(DILATED_GROUPS)
    dils = [dil for _, dil in DILATED_GROUPS]
    assert n_groups == 3 and dils[0] == 1 and all(hi % lo == 0 for lo, hi in zip(dils, dils[1:]))
    for window, dil in DILATED_GROUPS:
        assert window // dil == Q_BLOCK and (t // dil) % DEINT_CHUNK == 0
        assert (t // Q_BLOCK) % SCORE_BLOCKS_PER_TRIP == 0 and (t // Q_BLOCK) % VALUE_BLOCKS_PER_TRIP == 0
    blk = (None, t, HEAD_DIM)
    return pl.pallas_call(
        _attn_prompt_kernel,
        grid=(b, N_HEADS),
        in_specs=[
            pl.BlockSpec(blk, lambda i, h: (i, 0, h)),
            pl.BlockSpec(blk, lambda i, h: (i, 0, N_HEADS + h)),
            pl.BlockSpec(blk, lambda i, h: (i, 0, 2 * N_HEADS + h)),
            pl.BlockSpec(blk, lambda i, h: (i, 0, 3 * N_HEADS + h)),
        ],
        out_specs=pl.BlockSpec(blk, lambda i, h: (i, 0, h)),
        out_shape=jax.ShapeDtypeStruct((b, t, D_ATTN), out_dtype),
        scratch_shapes=(
            [pltpu.VMEM((n_groups, t, HEAD_DIM), BF16), pltpu.VMEM((n_groups, t, 2 * HEAD_DIM), BF16)]
            + [pltpu.VMEM((2, Q_BLOCK, 2 * Q_BLOCK), F32)]
            + [pltpu.VMEM((t // Q_BLOCK, Q_BLOCK, 2 * Q_BLOCK), F32)]
            + [pltpu.VMEM((t, HEAD_DIM), F32)] * (2 + 2 * (n_groups - 1))),
        compiler_params=_params("arbitrary", "arbitrary"),
        name="attn_prompt",
    )(proj, proj, proj, proj)


def _group_count(dist):
    cnt = jnp.zeros(dist.shape, F32)
    for window, dil in DILATED_GROUPS:
        hit = (dist >= 0) & (dist <= window) & ((dist & (dil - 1)) == 0)
        cnt = cnt + hit.astype(F32)
    return cnt


def _attn_sample_kernel(q_ref, g_ref, kn_ref, vn_ref, kc_ref, vc_ref, o_ref, m_ref, l_ref, acc_ref):
    j = pl.program_id(1)
    nj = pl.num_programs(1)
    tk = kc_ref.shape[0]
    n_buf = tk * nj
    n_rows = q_ref.shape[0]
    nt_dims = (((1,), (1,)), ((), ()))
    head_bits = N_HEADS.bit_length() - 1

    @pl.when(j == 0)
    def _():
        m_ref[...] = jnp.full(m_ref.shape, NEG_INF, F32)
        l_ref[...] = jnp.zeros(l_ref.shape, F32)
        acc_ref[...] = jnp.zeros(acc_ref.shape, F32)

    q = (q_ref[...] * ATTN_SCALE).astype(BF16)

    def accumulate(k2, v2, key_pos0):
        n_cols = k2.shape[0]
        row = lax.broadcasted_iota(jnp.int32, (n_rows, n_cols), 0)
        col = lax.broadcasted_iota(jnp.int32, (n_rows, n_cols), 1)
        same_head = (row & (N_HEADS - 1)) == (col & (N_HEADS - 1))
        dist = (n_buf + (row >> head_bits)) - (key_pos0 + (col >> head_bits))
        cnt = jnp.where(same_head, _group_count(dist), 0.0)
        s = lax.dot_general(q, k2.astype(BF16), nt_dims, preferred_element_type=F32)
        s = jnp.where(cnt > 0, s, NEG_INF)
        m_prev = m_ref[...]
        m_new = jnp.maximum(m_prev, jnp.max(s, axis=-1, keepdims=True))
        alpha = jnp.exp(m_prev - m_new)
        p = cnt * jnp.exp(s - m_new)
        l_ref[...] = alpha * l_ref[...] + jnp.sum(p, axis=-1, keepdims=True)
        acc_ref[...] = alpha * acc_ref[...] + jnp.dot(p.astype(BF16), v2.astype(BF16),
                                                      preferred_element_type=F32)
        m_ref[...] = m_new

    accumulate(kc_ref[...].reshape(tk * N_HEADS, HEAD_DIM), vc_ref[...].reshape(tk * N_HEADS, HEAD_DIM),
               j * tk)

    @pl.when(j == nj - 1)
    def _():
        accumulate(kn_ref[...], vn_ref[...], n_buf)
        o_ref[...] = (acc_ref[...] / l_ref[...]) * _silu(g_ref[...])


def _attn_sample(q2, g2, kn2, vn2, cache_k, cache_v, layer, tk):
    b, n_rows, _ = q2.shape
    n_buf = cache_k.shape[2]
    rblk = (None, n_rows, HEAD_DIM)
    cblk = (None, None, tk, N_HEADS, HEAD_DIM)
    return pl.pallas_call(
        _attn_sample_kernel,
        grid=(b, n_buf // tk),
        in_specs=[pl.BlockSpec(rblk, lambda i, j: (i, 0, 0))] * 4
        + [pl.BlockSpec(cblk, lambda i, j: (layer, i, j, 0, 0))] * 2,
        out_specs=pl.BlockSpec(rblk, lambda i, j: (i, 0, 0)),
        out_shape=jax.ShapeDtypeStruct((b, n_rows, HEAD_DIM), F32),
        scratch_shapes=[pltpu.VMEM((n_rows, 1), F32), pltpu.VMEM((n_rows, 1), F32),
                        pltpu.VMEM((n_rows, HEAD_DIM), F32)],
        compiler_params=_params("parallel", "arbitrary"),
        name="attn_sample",
    )(q2, g2, kn2, vn2, cache_k, cache_v)


def _cache_roll_kernel(ck_ref, ck_next_ref, kn_ref, cv_ref, cv_next_ref, vn_ref, ko_ref, vo_ref):
    j = pl.program_id(2)
    tk = ck_ref.shape[0]
    n_new = kn_ref.shape[0]
    for cur, nxt, new, out in ((ck_ref, ck_next_ref, kn_ref, ko_ref), (cv_ref, cv_next_ref, vn_ref, vo_ref)):
        out[0:tk - n_new] = cur[n_new:tk]

        @pl.when(j < pl.num_programs(2) - 1)
        def _(nxt=nxt, out=out):
            out[tk - n_new:tk] = nxt[0:n_new]

        @pl.when(j == pl.num_programs(2) - 1)
        def _(new=new, out=out):
            out[tk - n_new:tk] = new[...]


def _cache_roll(cache_k, cache_v, k_new, v_new, tk):
    n_layers, n_batch, n_buf, nh, hd = cache_k.shape
    n_new = k_new.shape[2]
    assert n_new <= SUBLANES and tk % SUBLANES == 0 and n_buf % tk == 0
    last_next = n_buf // SUBLANES - 1
    cur_spec = pl.BlockSpec((None, None, tk, nh, hd), lambda l, b, j: (l, b, j, 0, 0))
    next_spec = pl.BlockSpec((None, None, SUBLANES, nh, hd),
                             lambda l, b, j: (l, b, jnp.minimum((j + 1) * (tk // SUBLANES), last_next), 0, 0))
    new_spec = pl.BlockSpec((None, None, n_new, nh, hd), lambda l, b, j: (l, b, 0, 0, 0))
    out = jax.ShapeDtypeStruct(cache_k.shape, cache_k.dtype)
    return pl.pallas_call(
        _cache_roll_kernel,
        grid=(n_layers, n_batch, n_buf // tk),
        in_specs=[cur_spec, next_spec, new_spec] * 2,
        out_specs=[cur_spec] * 2,
        out_shape=[out, out],
        compiler_params=_params("parallel", "parallel", "arbitrary"),
        name="cache_roll",
    )(cache_k, cache_k, k_new, cache_v, cache_v, v_new)


def _head_major_kernel(*refs, n_layers):
    ko_ref, vo_ref = refs[2 * n_layers:]
    tq = refs[0].shape[0]
    for layer in range(n_layers):
        @pl.when(pl.program_id(0) == layer)
        def _(layer=layer):
            for src, dst in ((refs[2 * layer], ko_ref), (refs[2 * layer + 1], vo_ref)):
                for h in range(N_HEADS):
                    dst[pl.ds(h, tq, stride=N_HEADS), :] = src[:, h * HEAD_DIM:(h + 1) * HEAD_DIM]


def _head_major_kv(projs, n_keep, tq):
    n_layers = len(projs)
    b, t, _ = projs[0].shape
    first = (t - n_keep) // tq
    nj = n_keep // tq
    in_specs = []
    for layer in range(n_layers):
        for col in (1, 2):
            in_specs.append(pl.BlockSpec(
                (None, tq, D_ATTN),
                lambda l, i, j, layer=layer, col=col: (jnp.where(l == layer, i, 0),
                                                       jnp.where(l == layer, first + j, 0), col)))
    out_spec = pl.BlockSpec((None, None, tq * N_HEADS, HEAD_DIM), lambda l, i, j: (l, i, j, 0))
    out = jax.ShapeDtypeStruct((n_layers, b, n_keep * N_HEADS, HEAD_DIM), F32)
    k_out, v_out = pl.pallas_call(
        functools.partial(_head_major_kernel, n_layers=n_layers),
        grid=(n_layers, b, nj),
        in_specs=in_specs,
        out_specs=[out_spec, out_spec],
        out_shape=[out, out],
        compiler_params=_params("arbitrary", "arbitrary", "arbitrary"),
        name="head_major_kv",
    )(*[p for p in projs for _ in (1, 2)])
    shape = (n_layers, b, n_keep, N_HEADS, HEAD_DIM)
    return k_out.reshape(shape), v_out.reshape(shape)


def _cast_kernel(x_ref, o_ref):
    o_ref[...] = x_ref[...].astype(o_ref.dtype)


def _cast_bf16(w, rows):
    n_layers, r, c = w.shape
    return pl.pallas_call(
        _cast_kernel,
        grid=(n_layers, r // rows),
        in_specs=[pl.BlockSpec((1, rows, c), lambda l, i: (l, i, 0))],
        out_specs=pl.BlockSpec((1, rows, c), lambda l, i: (l, i, 0)),
        out_shape=jax.ShapeDtypeStruct(w.shape, BF16),
        compiler_params=_params("parallel", "parallel"),
        name="cast_bf16",
    )(w)


def kernel(x_prompt, x_sample, state_pool, state_conv, cache_k, cache_v, norm_w, final_norm_w,
           w_in_ab, pool_lin, pool_scale, conv_w, w_out_ab, w_in_c, w_out_c):
    bp, t, d = x_prompt.shape
    bs, ts, _ = x_sample.shape
    ts_pad = -(-ts // SUBLANES) * SUBLANES
    n_buf = cache_k.shape[2]
    n_keep_p = min(MAX_WINDOW, t)

    hp = x_prompt.reshape(bp * t, d)
    hs = jnp.pad(x_sample, ((0, 0), (0, ts_pad - ts), (0, 0))).reshape(bs * ts_pad, d)
    zero_pool = jnp.zeros((bp, POOL_HIST, D_POOL), F32)
    zero_conv = jnp.zeros((bp, CONV_HIST, D_CONV), F32)

    tm_p, tn = 1024, 1024
    tm_s = bs * ts_pad
    cast_rows = 256

    w_in_ab = _cast_bf16(w_in_ab, cast_rows)
    w_out_ab = _cast_bf16(w_out_ab, cast_rows)
    w_in_c = _cast_bf16(w_in_c, cast_rows)
    w_out_c = _cast_bf16(w_out_c, cast_rows)
    pool_lin = _cast_bf16(pool_lin.reshape(pool_lin.shape[0], -1, D_POOL_GROUP), cast_rows).reshape(pool_lin.shape)

    def head_rows(a):
        return a.reshape(a.shape[0], a.shape[1] * N_HEADS, HEAD_DIM)

    pool_p, pool_s, conv_p, conv_s = [], [], [], []
    projs_c, k_new, v_new = [], [], []
    for l in range(DEPTH):
        i = l // 2
        nw = norm_w[l].reshape(1, d)
        if l % 2 == 0:
            pscale = pool_scale[i].reshape(1, D_POOL)
            proj_p = _norm_matmul(hp, nw, w_in_ab, i, tm_p, tn).reshape(bp, t, -1)
            proj_s = _norm_matmul(hs, nw, w_in_ab, i, tm_s, tn).reshape(bs, ts_pad, -1)
            mix_p, pp, cp = _ab_mix(proj_p, zero_pool, zero_conv, pool_lin[i], pscale, conv_w[i],
                                    256, 0, t, BF16)
            mix_s, ps, cs = _ab_mix(proj_s, state_pool[i], state_conv[i], pool_lin[i], pscale, conv_w[i],
                                    ts_pad, PAST_LEN, ts, F32)
            hp = _matmul_residual(mix_p.reshape(bp * t, -1), w_out_ab, i, hp, tm_p, tn)
            hs = _matmul_residual(mix_s.reshape(bs * ts_pad, -1), w_out_ab, i, hs, tm_s, tn)
            pool_p.append(pp)
            pool_s.append(ps)
            conv_p.append(cp)
            conv_s.append(cs)
        else:
            proj_p = _norm_matmul(hp, nw, w_in_c, i, tm_p, tn).reshape(bp, t, -1)
            proj_s = _norm_matmul(hs, nw, w_in_c, i, tm_s, tn).reshape(bs, ts_pad, -1)[:, :ts]
            og_p = _attn_prompt(proj_p, BF16)
            q_s, kn_s, vn_s, g_s = (head_rows(proj_s[..., c * D_ATTN:(c + 1) * D_ATTN]) for c in range(4))
            og_s = _attn_sample(q_s, g_s, kn_s, vn_s, cache_k, cache_v, i, SAMPLE_KEY_TILE)
            og_s = jnp.pad(og_s.reshape(bs, ts, D_ATTN), ((0, 0), (0, ts_pad - ts), (0, 0)))
            hp = _matmul_residual(og_p.reshape(bp * t, -1), w_out_c, i, hp, tm_p, tn)
            hs = _matmul_residual(og_s.reshape(bs * ts_pad, -1), w_out_c, i, hs, tm_s, tn)
            projs_c.append(proj_p)
            k_new.append(kn_s.reshape(bs, ts, N_HEADS, HEAD_DIM))
            v_new.append(vn_s.reshape(bs, ts, N_HEADS, HEAD_DIM))

    assert min(MAX_WINDOW, n_buf + ts) == n_buf
    k_s, v_s = _cache_roll(cache_k, cache_v, jnp.stack(k_new), jnp.stack(v_new), ROLL_TILE)
    k_p, v_p = _head_major_kv(projs_c, n_keep_p, 256)

    fw = final_norm_w.reshape(1, d)
    y_prompt = _rmsnorm(hp, fw, 512).reshape(bp, t, d)
    y_sample = _rmsnorm(hs, fw, tm_s).reshape(bs, ts_pad, d)[:, :ts]
    return (y_prompt, y_sample, jnp.stack(pool_p), jnp.stack(pool_s), jnp.stack(conv_p), jnp.stack(conv_s),
            k_p, k_s, v_p, v_s)
```

```python
import functools

import jax
import jax.numpy as jnp
from jax import lax
from jax.experimental import pallas as pl
from jax.experimental.pallas import tpu as pltpu

F32 = jnp.float32
BF16 = jnp.bfloat16

D_MODEL = 2048
DEPTH = 4
PAST_LEN = 16384
D_POOL = D_MODEL // 2
POOL_WINDOWS = (2, 4, 8, 16)
D_POOL_GROUP = D_POOL // len(POOL_WINDOWS)
POOL_HIST = max(POOL_WINDOWS) - 1
D_CONV = D_MODEL // 2
CONV_WIDTH = 3
CONV_HIST = CONV_WIDTH - 1
HEAD_DIM = 128
N_HEADS = D_MODEL // HEAD_DIM
D_ATTN = N_HEADS * HEAD_DIM
DILATED_GROUPS = ((128, 1), (512, 4), (2048, 16))
MAX_WINDOW = max(w for w, _ in DILATED_GROUPS)
Q_BLOCK = 128
ATTN_SCALE = HEAD_DIM ** -0.5
LOG2_E = 1.4426950408889634
RMS_EPS = 1e-6
NEG_INF = -1e30

SUBLANES = 8
POOL_PAD = 16
CONV_PAD = 8
DEINT_CHUNK = 2 * Q_BLOCK
SAMPLE_KEY_TILE = 512
ROLL_TILE = 256
SCORE_BLOCKS_PER_TRIP = 16
VALUE_BLOCKS_PER_TRIP = 8
MERGE_BLOCKS_PER_TRIP = 8
VMEM_LIMIT = 48 * 1024 * 1024
AB_LAYER_VMEM_LIMIT = 56 * 1024 * 1024
AB_ROW_TILE = 256
AB_PROJ_CHUNK = 1024


def _silu(x):
    return x * jax.nn.sigmoid(x)


def _params(*semantics):
    return pltpu.CompilerParams(dimension_semantics=semantics, vmem_limit_bytes=VMEM_LIMIT)


def _norm_matmul_kernel(x_ref, nw_ref, w_ref, o_ref, xn_ref):
    @pl.when(pl.program_id(1) == 0)
    def _():
        x = x_ref[...]
        ms = jnp.mean(x * x, axis=-1, keepdims=True)
        xn_ref[...] = (x * lax.rsqrt(ms + RMS_EPS) * nw_ref[...]).astype(BF16)

    o_ref[...] = jnp.dot(xn_ref[...], w_ref[...], preferred_element_type=F32)


def _norm_matmul(x, nw, w, layer, tm, tn):
    m, d = x.shape
    n = w.shape[2]
    return pl.pallas_call(
        _norm_matmul_kernel,
        grid=(m // tm, n // tn),
        in_specs=[
            pl.BlockSpec((tm, d), lambda i, j: (i, 0)),
            pl.BlockSpec((1, d), lambda i, j: (0, 0)),
            pl.BlockSpec((None, d, tn), lambda i, j: (layer, 0, j)),
        ],
        out_specs=pl.BlockSpec((tm, tn), lambda i, j: (i, j)),
        out_shape=jax.ShapeDtypeStruct((m, n), F32),
        scratch_shapes=[pltpu.VMEM((tm, d), BF16)],
        compiler_params=_params("parallel", "arbitrary"),
        name="norm_matmul",
    )(x, nw, w)


def _matmul_residual_kernel(a_ref, w_ref, h_ref, o_ref):
    o_ref[...] = h_ref[...] + jnp.dot(a_ref[...].astype(BF16), w_ref[...], preferred_element_type=F32)


def _matmul_residual(a, w, layer, h, tm, tn):
    m, k = a.shape
    n = w.shape[2]
    return pl.pallas_call(
        _matmul_residual_kernel,
        grid=(m // tm, n // tn),
        in_specs=[
            pl.BlockSpec((tm, k), lambda i, j: (i, 0)),
            pl.BlockSpec((None, k, tn), lambda i, j: (layer, 0, j)),
            pl.BlockSpec((tm, tn), lambda i, j: (i, j)),
        ],
        out_specs=pl.BlockSpec((tm, tn), lambda i, j: (i, j)),
        out_shape=jax.ShapeDtypeStruct((m, n), F32),
        compiler_params=_params("parallel", "arbitrary"),
        name="matmul_residual",
    )(a, w, h)


def _rmsnorm_kernel(x_ref, nw_ref, o_ref):
    x = x_ref[...]
    ms = jnp.mean(x * x, axis=-1, keepdims=True)
    o_ref[...] = x * lax.rsqrt(ms + RMS_EPS) * nw_ref[...]


def _rmsnorm(x, nw, tm):
    m, d = x.shape
    return pl.pallas_call(
        _rmsnorm_kernel,
        grid=(m // tm,),
        in_specs=[pl.BlockSpec((tm, d), lambda i: (i, 0)), pl.BlockSpec((1, d), lambda i: (0, 0))],
        out_specs=pl.BlockSpec((tm, d), lambda i: (i, 0)),
        out_shape=jax.ShapeDtypeStruct((m, d), F32),
        compiler_params=_params("parallel"),
        name="final_rmsnorm",
    )(x, nw)


def _ab_mix_kernel(proj_ref, ph_ref, ch_ref, plin_ref, pscale_ref, cw_ref,
                   mix_ref, ptail_ref, ctail_ref, uext_ref, zext_ref, *, pos0, t_last):
    j = pl.program_id(1)
    tm = proj_ref.shape[1]
    gw = D_POOL_GROUP

    @pl.when(j == 0)
    def _():
        uext_ref[POOL_PAD - POOL_HIST:POOL_PAD, :] = ph_ref[0]
        zext_ref[CONV_PAD - CONV_HIST:CONV_PAD, :] = ch_ref[0]

    @pl.when(j > 0)
    def _():
        uext_ref[0:POOL_PAD, :] = uext_ref[tm:tm + POOL_PAD, :]
        zext_ref[0:CONV_PAD, :] = zext_ref[tm:tm + CONV_PAD, :]

    uext_ref[POOL_PAD:POOL_PAD + tm, :] = proj_ref[0, :, 0:D_POOL]
    c_off = 2 * D_POOL + D_CONV
    zext_ref[CONV_PAD:CONV_PAD + tm, :] = (
        proj_ref[0, :, c_off:c_off + D_CONV] * proj_ref[0, :, c_off + D_CONV:c_off + 2 * D_CONV])

    pos = (pos0 + j * tm + lax.broadcasted_iota(jnp.int32, (tm, 1), 0)).astype(F32)

    for g, k in enumerate(POOL_WINDOWS):
        c0 = g * gw
        u_g = uext_ref[POOL_PAD:POOL_PAD + tm, c0:c0 + gw]
        s = u_g
        for i in range(1, k):
            s = s + uext_ref[POOL_PAD - i:POOL_PAD - i + tm, c0:c0 + gw]
        cnt = jnp.minimum(float(k), pos + 1.0)
        pooled = s / cnt - u_g
        a = jnp.dot(pooled.astype(BF16), plin_ref[g], preferred_element_type=F32)
        a = a * pscale_ref[:, c0:c0 + gw]
        gate = proj_ref[0, :, D_POOL + c0:D_POOL + c0 + gw]
        mix_ref[0, :, c0:c0 + gw] = (a * _silu(gate)).astype(mix_ref.dtype)

    for c in range(D_CONV // gw):
        c0 = c * gw
        conv = zext_ref[CONV_PAD - 2:CONV_PAD - 2 + tm, c0:c0 + gw] * cw_ref[0:1, c0:c0 + gw]
        conv = conv + zext_ref[CONV_PAD - 1:CONV_PAD - 1 + tm, c0:c0 + gw] * cw_ref[1:2, c0:c0 + gw]
        conv = conv + zext_ref[CONV_PAD:CONV_PAD + tm, c0:c0 + gw] * cw_ref[2:3, c0:c0 + gw]
        b_gate = proj_ref[0, :, 2 * D_POOL + c0:2 * D_POOL + c0 + gw]
        gate = proj_ref[0, :, 2 * D_POOL + 3 * D_CONV + c0:2 * D_POOL + 3 * D_CONV + c0 + gw]
        mix_ref[0, :, D_POOL + c0:D_POOL + c0 + gw] = (b_gate * conv * _silu(gate)).astype(mix_ref.dtype)

    @pl.when(j == pl.num_programs(1) - 1)
    def _():
        ptail_ref[0] = uext_ref[POOL_PAD + t_last - POOL_HIST:POOL_PAD + t_last, :]
        ctail_ref[0] = zext_ref[CONV_PAD + t_last - CONV_HIST:CONV_PAD + t_last, :]


def _ab_mix(proj, pool_hist, conv_hist, plin, pscale, cw, tm, pos0, t_valid, mix_dtype):
    b, t, n = proj.shape
    nj = t // tm
    t_last = t_valid - (nj - 1) * tm
    kern = functools.partial(_ab_mix_kernel, pos0=pos0, t_last=t_last)
    return pl.pallas_call(
        kern,
        grid=(b, nj),
        in_specs=[
            pl.BlockSpec((1, tm, n), lambda i, j: (i, j, 0)),
            pl.BlockSpec((1, POOL_HIST, D_POOL), lambda i, j: (i, 0, 0)),
            pl.BlockSpec((1, CONV_HIST, D_CONV), lambda i, j: (i, 0, 0)),
            pl.BlockSpec(plin.shape, lambda i, j: (0, 0, 0)),
            pl.BlockSpec((1, D_POOL), lambda i, j: (0, 0)),
            pl.BlockSpec((CONV_WIDTH, D_CONV), lambda i, j: (0, 0)),
        ],
        out_specs=[
            pl.BlockSpec((1, tm, D_POOL + D_CONV), lambda i, j: (i, j, 0)),
            pl.BlockSpec((1, POOL_HIST, D_POOL), lambda i, j: (i, 0, 0)),
            pl.BlockSpec((1, CONV_HIST, D_CONV), lambda i, j: (i, 0, 0)),
        ],
        out_shape=[
            jax.ShapeDtypeStruct((b, t, D_POOL + D_CONV), mix_dtype),
            jax.ShapeDtypeStruct((b, POOL_HIST, D_POOL), F32),
            jax.ShapeDtypeStruct((b, CONV_HIST, D_CONV), F32),
        ],
        scratch_shapes=[pltpu.VMEM((tm + POOL_PAD, D_POOL), F32), pltpu.VMEM((tm + CONV_PAD, D_CONV), F32)],
        compiler_params=_params("parallel", "arbitrary"),
        name="ab_mix",
    )(proj, pool_hist, conv_hist, plin, pscale, cw)


def _ab_layer_kernel(x_ref, nw_ref, win_ref, ph_ref, ch_ref, plin_ref, pscale_ref, cw_ref, wout_ref,
                     o_ref, ptail_ref, ctail_ref, proj_ref, mix_ref, uext_ref, zext_ref, *, pos0, t_last):
    x = x_ref[0]
    ms = jnp.mean(x * x, axis=-1, keepdims=True)
    xn = (x * lax.rsqrt(ms + RMS_EPS) * nw_ref[...]).astype(BF16)
    n_in = win_ref.shape[1]
    for c0 in range(0, n_in, AB_PROJ_CHUNK):
        proj_ref[0, :, c0:c0 + AB_PROJ_CHUNK] = jnp.dot(xn, win_ref[:, c0:c0 + AB_PROJ_CHUNK],
                                                        preferred_element_type=F32)
    _ab_mix_kernel(proj_ref, ph_ref, ch_ref, plin_ref, pscale_ref, cw_ref, mix_ref, ptail_ref, ctail_ref,
                   uext_ref, zext_ref, pos0=pos0, t_last=t_last)
    o_ref[0] = x + jnp.dot(mix_ref[0].astype(BF16), wout_ref[...], preferred_element_type=F32)


def _ab_layer(x, nw, w_in, w_out, layer, pool_hist, conv_hist, plin, pscale, cw, tm, pos0, t_valid):
    b, t, d = x.shape
    n_in = w_in.shape[2]
    nj = t // tm
    t_last = t_valid - (nj - 1) * tm
    resident = pl.Buffered(1)
    return pl.pallas_call(
        functools.partial(_ab_layer_kernel, pos0=pos0, t_last=t_last),
        grid=(b, nj),
        in_specs=[
            pl.BlockSpec((1, tm, d), lambda i, j: (i, j, 0)),
            pl.BlockSpec((1, d), lambda i, j: (0, 0)),
            pl.BlockSpec((None, d, n_in), lambda i, j: (layer, 0, 0), pipeline_mode=resident),
            pl.BlockSpec((1, POOL_HIST, D_POOL), lambda i, j: (i, 0, 0)),
            pl.BlockSpec((1, CONV_HIST, D_CONV), lambda i, j: (i, 0, 0)),
            pl.BlockSpec(plin.shape, lambda i, j: (0, 0, 0)),
            pl.BlockSpec((1, D_POOL), lambda i, j: (0, 0)),
            pl.BlockSpec((CONV_WIDTH, D_CONV), lambda i, j: (0, 0)),
            pl.BlockSpec((None, D_POOL + D_CONV, d), lambda i, j: (layer, 0, 0), pipeline_mode=resident),
        ],
        out_specs=[
            pl.BlockSpec((1, tm, d), lambda i, j: (i, j, 0)),
            pl.BlockSpec((1, POOL_HIST, D_POOL), lambda i, j: (i, 0, 0)),
            pl.BlockSpec((1, CONV_HIST, D_CONV), lambda i, j: (i, 0, 0)),
        ],
        out_shape=[
            jax.ShapeDtypeStruct((b, t, d), F32),
            jax.ShapeDtypeStruct((b, POOL_HIST, D_POOL), F32),
            jax.ShapeDtypeStruct((b, CONV_HIST, D_CONV), F32),
        ],
        scratch_shapes=[
            pltpu.VMEM((1, tm, n_in), F32),
            pltpu.VMEM((1, tm, D_POOL + D_CONV), BF16 if tm % (2 * SUBLANES) == 0 else F32),
            pltpu.VMEM((tm + POOL_PAD, D_POOL), F32),
            pltpu.VMEM((tm + CONV_PAD, D_CONV), F32),
        ],
        compiler_params=pltpu.CompilerParams(dimension_semantics=("parallel", "arbitrary"),
                                             vmem_limit_bytes=AB_LAYER_VMEM_LIMIT),
        name="ab_layer",
    )(x, nw, w_in, pool_hist, conv_hist, plin, pscale, cw, w_out)


def _rows(start, size, stride):
    if stride == 1:
        return pl.ds(start, size)
    return pl.ds(start, size, stride=stride)


def _attn_prompt_kernel(q_ref, k_ref, v_ref, g_ref, o_ref, kd_ref, vd_ref, bias_ref, s_ref, *state_refs):
    t = q_ref.shape[0]
    n_groups = len(DILATED_GROUPS)
    kf_ref, vf_ref = state_refs[0:2]
    acc_refs = (None,) + tuple(state_refs[2:2 + n_groups - 1])
    stat_refs = (None,) + tuple(state_refs[2 + n_groups - 1:])
    nt_dims = (((1,), (1,)), ((), ()))
    half = HEAD_DIM // 2
    lane = lax.broadcasted_iota(jnp.int32, (Q_BLOCK, HEAD_DIM), 1)

    @pl.when((pl.program_id(0) == 0) & (pl.program_id(1) == 0))
    def _():
        delta = (lax.broadcasted_iota(jnp.int32, (Q_BLOCK, 2 * Q_BLOCK), 0)
                 - lax.broadcasted_iota(jnp.int32, (Q_BLOCK, 2 * Q_BLOCK), 1))
        for sel in range(2):
            dist = delta + sel * Q_BLOCK
            bias_ref[sel] = jnp.where((dist >= 0) & (dist <= Q_BLOCK), 0.0, NEG_INF)
        vd_ref[:, :, HEAD_DIM:] = jnp.ones((n_groups, t, HEAD_DIM), BF16)

    def deinterleave(gi, d, d_src, ksrc_ref, vsrc_ref, keep_f32):
        stream_len = t // d
        ratio = d // d_src

        def body(c, carry):
            dst = pl.multiple_of(c * DEINT_CHUNK, DEINT_CHUNK)
            r = dst // stream_len
            m0 = dst % stream_len
            src = _rows((r % d_src) * (t // d_src) + r // d_src + ratio * m0, DEINT_CHUNK, ratio)
            k = ksrc_ref[src, :]
            v = vsrc_ref[src, :]
            kd_ref[gi, pl.ds(dst, DEINT_CHUNK), :] = k.astype(BF16)
            vd_ref[gi, pl.ds(dst, DEINT_CHUNK), 0:HEAD_DIM] = v.astype(BF16)
            if keep_f32:
                kf_ref[pl.ds(dst, DEINT_CHUNK), :] = k
                vf_ref[pl.ds(dst, DEINT_CHUNK), :] = v
            return carry

        lax.fori_loop(0, t // DEINT_CHUNK, body, 0, unroll=2)

    def merged_output(rows, m0, acc0, l0):
        ms, ls, accs = [m0], [l0], [acc0]
        for gi in range(1, n_groups):
            st = stat_refs[gi][rows, :]
            sw = pltpu.roll(st, half, axis=1)
            ms.append(jnp.where(lane < half, st, sw))
            ls.append(jnp.where(lane < half, sw, st))
            accs.append(acc_refs[gi][rows, :])
        m_all = functools.reduce(jnp.maximum, ms)
        num = None
        den = None
        for m_g, l_g, acc_g in zip(ms, ls, accs):
            w = jnp.exp2(m_g - m_all)
            num = w * acc_g if num is None else num + w * acc_g
            den = w * l_g if den is None else den + w * l_g
        g = g_ref[rows, :]
        return ((num * g) / (den * (1.0 + jnp.exp(-g)))).astype(o_ref.dtype)

    def run_group(gi, d):
        stream_len = t // d
        nb = stream_len // Q_BLOCK

        def block_rows(idx):
            r = idx // nb
            mb = idx % nb
            kb = jnp.maximum(mb - 1, 0)
            q_start = r + d * Q_BLOCK * mb
            q_rows = _rows(pl.multiple_of(q_start, Q_BLOCK) if d == 1 else q_start, Q_BLOCK, d)
            k_rows = pl.ds(pl.multiple_of(r * stream_len + kb * Q_BLOCK, Q_BLOCK), 2 * Q_BLOCK)
            return q_rows, k_rows, mb - kb

        def scores(it, carry):
            s = []
            for u in range(SCORE_BLOCKS_PER_TRIP):
                q_rows, k_rows, sel = block_rows(it * SCORE_BLOCKS_PER_TRIP + u)
                q = (q_ref[q_rows, :] * (ATTN_SCALE * LOG2_E)).astype(BF16)
                s.append(lax.dot_general(q, kd_ref[gi, k_rows, :], nt_dims, preferred_element_type=F32)
                         + bias_ref[sel])
            for u in range(SCORE_BLOCKS_PER_TRIP):
                s_ref[it * SCORE_BLOCKS_PER_TRIP + u] = s[u]
            return carry

        n_val = MERGE_BLOCKS_PER_TRIP if gi == 0 else VALUE_BLOCKS_PER_TRIP

        def values(it, carry):
            rows, s, v = [], [], []
            for u in range(n_val):
                q_rows, k_rows, _ = block_rows(it * n_val + u)
                rows.append(q_rows)
                s.append(s_ref[it * n_val + u])
                v.append(vd_ref[gi, k_rows, :])
            outs = []
            for u in range(n_val):
                m_blk = jnp.max(s[u], axis=-1, keepdims=True)
                p = jnp.exp2(s[u] - m_blk).astype(BF16)
                pv = jnp.dot(p, v[u], preferred_element_type=F32)
                acc, l_blk = pv[:, 0:HEAD_DIM], pv[:, HEAD_DIM:]
                if gi == 0:
                    outs.append((merged_output(rows[u], jnp.broadcast_to(m_blk, acc.shape), acc, l_blk),))
                else:
                    outs.append((acc, jnp.where(lane < half, m_blk, l_blk)))
            for u in range(n_val):
                if gi == 0:
                    o_ref[rows[u], :] = outs[u][0]
                else:
                    acc_refs[gi][rows[u], :] = outs[u][0]
                    stat_refs[gi][rows[u], :] = outs[u][1]
            return carry

        lax.fori_loop(0, d * nb // SCORE_BLOCKS_PER_TRIP, scores, 0)
        lax.fori_loop(0, d * nb // n_val, values, 0)

    ksrc_ref, vsrc_ref, d_src = k_ref, v_ref, 1
    for gi, (_, d) in enumerate(DILATED_GROUPS):
        keep_f32 = 0 < gi < n_groups - 1
        deinterleave(gi, d, d_src, ksrc_ref, vsrc_ref, keep_f32)
        if keep_f32:
            ksrc_ref, vsrc_ref, d_src = kf_ref, vf_ref, d
    for gi in reversed(range(n_groups)):
        run_group(gi, DILATED_GROUPS[gi][1])


def _attn_prompt(proj, out_dtype):
    b, t, _ = proj.shape
    n_groups = len(DILATED_GROUPS)
    dils = [dil for _, dil in DILATED_GROUPS]
    assert n_groups == 3 and dils[0] == 1 and all(hi % lo == 0 for lo, hi in zip(dils, dils[1:]))
    for window, dil in DILATED_GROUPS:
        assert window // dil == Q_BLOCK and (t // dil) % DEINT_CHUNK == 0
        assert (t // Q_BLOCK) % SCORE_BLOCKS_PER_TRIP == 0 and (t // Q_BLOCK) % VALUE_BLOCKS_PER_TRIP == 0
    blk = (None, t, HEAD_DIM)
    return pl.pallas_call(
        _attn_prompt_kernel,
        grid=(b, N_HEADS),
        in_specs=[
            pl.BlockSpec(blk, lambda i, h: (i, 0, h)),
            pl.BlockSpec(blk, lambda i, h: (i, 0, N_HEADS + h)),
            pl.BlockSpec(blk, lambda i, h: (i, 0, 2 * N_HEADS + h)),
            pl.BlockSpec(blk, lambda i, h: (i, 0, 3 * N_HEADS + h)),
        ],
        out_specs=pl.BlockSpec(blk, lambda i, h: (i, 0, h)),
        out_shape=jax.ShapeDtypeStruct((b, t, D_ATTN), out_dtype),
        scratch_shapes=(
            [pltpu.VMEM((n_groups, t, HEAD_DIM), BF16), pltpu.VMEM((n_groups, t, 2 * HEAD_DIM), BF16)]
            + [pltpu.VMEM((2, Q_BLOCK, 2 * Q_BLOCK), F32)]
            + [pltpu.VMEM((t // Q_BLOCK, Q_BLOCK, 2 * Q_BLOCK), F32)]
            + [pltpu.VMEM((t, HEAD_DIM), F32)] * (2 + 2 * (n_groups - 1))),
        compiler_params=_params("arbitrary", "arbitrary"),
        name="attn_prompt",
    )(proj, proj, proj, proj)


def _group_count(dist):
    cnt = jnp.zeros(dist.shape, F32)
    for window, dil in DILATED_GROUPS:
        hit = (dist >= 0) & (dist <= window) & ((dist & (dil - 1)) == 0)
        cnt = cnt + hit.astype(F32)
    return cnt


def _attn_sample_kernel(q_ref, g_ref, kn_ref, vn_ref, kc_ref, vc_ref, o_ref, m_ref, l_ref, acc_ref):
    j = pl.program_id(1)
    nj = pl.num_programs(1)
    tk = kc_ref.shape[0]
    n_buf = tk * nj
    n_rows = q_ref.shape[0]
    nt_dims = (((1,), (1,)), ((), ()))
    head_bits = N_HEADS.bit_length() - 1

    @pl.when(j == 0)
    def _():
        m_ref[...] = jnp.full(m_ref.shape, NEG_INF, F32)
        l_ref[...] = jnp.zeros(l_ref.shape, F32)
        acc_ref[...] = jnp.zeros(acc_ref.shape, F32)

    q = (q_ref[...] * ATTN_SCALE).astype(BF16)

    def accumulate(k2, v2, key_pos0):
        n_cols = k2.shape[0]
        row = lax.broadcasted_iota(jnp.int32, (n_rows, n_cols), 0)
        col = lax.broadcasted_iota(jnp.int32, (n_rows, n_cols), 1)
        same_head = (row & (N_HEADS - 1)) == (col & (N_HEADS - 1))
        dist = (n_buf + (row >> head_bits)) - (key_pos0 + (col >> head_bits))
        cnt = jnp.where(same_head, _group_count(dist), 0.0)
        s = lax.dot_general(q, k2.astype(BF16), nt_dims, preferred_element_type=F32)
        s = jnp.where(cnt > 0, s, NEG_INF)
        m_prev = m_ref[...]
        m_new = jnp.maximum(m_prev, jnp.max(s, axis=-1, keepdims=True))
        alpha = jnp.exp(m_prev - m_new)
        p = cnt * jnp.exp(s - m_new)
        l_ref[...] = alpha * l_ref[...] + jnp.sum(p, axis=-1, keepdims=True)
        acc_ref[...] = alpha * acc_ref[...] + jnp.dot(p.astype(BF16), v2.astype(BF16),
                                                      preferred_element_type=F32)
        m_ref[...] = m_new

    accumulate(kc_ref[...].reshape(tk * N_HEADS, HEAD_DIM), vc_ref[...].reshape(tk * N_HEADS, HEAD_DIM),
               j * tk)

    @pl.when(j == nj - 1)
    def _():
        accumulate(kn_ref[...], vn_ref[...], n_buf)
        o_ref[...] = (acc_ref[...] / l_ref[...]) * _silu(g_ref[...])


def _attn_sample(q2, g2, kn2, vn2, cache_k, cache_v, layer, tk):
    b, n_rows, _ = q2.shape
    n_buf = cache_k.shape[2]
    rblk = (None, n_rows, HEAD_DIM)
    cblk = (None, None, tk, N_HEADS, HEAD_DIM)
    return pl.pallas_call(
        _attn_sample_kernel,
        grid=(b, n_buf // tk),
        in_specs=[pl.BlockSpec(rblk, lambda i, j: (i, 0, 0))] * 4
        + [pl.BlockSpec(cblk, lambda i, j: (layer, i, j, 0, 0))] * 2,
        out_specs=pl.BlockSpec(rblk, lambda i, j: (i, 0, 0)),
        out_shape=jax.ShapeDtypeStruct((b, n_rows, HEAD_DIM), F32),
        scratch_shapes=[pltpu.VMEM((n_rows, 1), F32), pltpu.VMEM((n_rows, 1), F32),
                        pltpu.VMEM((n_rows, HEAD_DIM), F32)],
        compiler_params=_params("parallel", "arbitrary"),
        name="attn_sample",
    )(q2, g2, kn2, vn2, cache_k, cache_v)


def _cache_roll_kernel(ck_ref, ck_next_ref, kn_ref, cv_ref, cv_next_ref, vn_ref, ko_ref, vo_ref):
    j = pl.program_id(2)
    tk = ck_ref.shape[0]
    n_new = kn_ref.shape[0]
    for cur, nxt, new, out in ((ck_ref, ck_next_ref, kn_ref, ko_ref), (cv_ref, cv_next_ref, vn_ref, vo_ref)):
        out[0:tk - n_new] = cur[n_new:tk]

        @pl.when(j < pl.num_programs(2) - 1)
        def _(nxt=nxt, out=out):
            out[tk - n_new:tk] = nxt[0:n_new]

        @pl.when(j == pl.num_programs(2) - 1)
        def _(new=new, out=out):
            out[tk - n_new:tk] = new[...]


def _cache_roll(cache_k, cache_v, k_new, v_new, tk):
    n_layers, n_batch, n_buf, nh, hd = cache_k.shape
    n_new = k_new.shape[2]
    assert n_new <= SUBLANES and tk % SUBLANES == 0 and n_buf % tk == 0
    last_next = n_buf // SUBLANES - 1
    cur_spec = pl.BlockSpec((None, None, tk, nh, hd), lambda l, b, j: (l, b, j, 0, 0))
    next_spec = pl.BlockSpec((None, None, SUBLANES, nh, hd),
                             lambda l, b, j: (l, b, jnp.minimum((j + 1) * (tk // SUBLANES), last_next), 0, 0))
    new_spec = pl.BlockSpec((None, None, n_new, nh, hd), lambda l, b, j: (l, b, 0, 0, 0))
    out = jax.ShapeDtypeStruct(cache_k.shape, cache_k.dtype)
    return pl.pallas_call(
        _cache_roll_kernel,
        grid=(n_layers, n_batch, n_buf // tk),
        in_specs=[cur_spec, next_spec, new_spec] * 2,
        out_specs=[cur_spec] * 2,
        out_shape=[out, out],
        compiler_params=_params("parallel", "parallel", "arbitrary"),
        name="cache_roll",
    )(cache_k, cache_k, k_new, cache_v, cache_v, v_new)


def _head_major_kernel(*refs, n_layers):
    ko_ref, vo_ref = refs[2 * n_layers:]
    tq = refs[0].shape[0]
    for layer in range(n_layers):
        @pl.when(pl.program_id(0) == layer)
        def _(layer=layer):
            for src, dst in ((refs[2 * layer], ko_ref), (refs[2 * layer + 1], vo_ref)):
                for h in range(N_HEADS):
                    dst[pl.ds(h, tq, stride=N_HEADS), :] = src[:, h * HEAD_DIM:(h + 1) * HEAD_DIM]


def _head_major_kv(projs, n_keep, tq):
    n_layers = len(projs)
    b, t, _ = projs[0].shape
    first = (t - n_keep) // tq
    nj = n_keep // tq
    in_specs = []
    for layer in range(n_layers):
        for col in (1, 2):
            in_specs.append(pl.BlockSpec(
                (None, tq, D_ATTN),
                lambda l, i, j, layer=layer, col=col: (jnp.where(l == layer, i, 0),
                                                       jnp.where(l == layer, first + j, 0), col)))
    out_spec = pl.BlockSpec((None, None, tq * N_HEADS, HEAD_DIM), lambda l, i, j: (l, i, j, 0))
    out = jax.ShapeDtypeStruct((n_layers, b, n_keep * N_HEADS, HEAD_DIM), F32)
    k_out, v_out = pl.pallas_call(
        functools.partial(_head_major_kernel, n_layers=n_layers),
        grid=(n_layers, b, nj),
        in_specs=in_specs,
        out_specs=[out_spec, out_spec],
        out_shape=[out, out],
        compiler_params=_params("arbitrary", "arbitrary", "arbitrary"),
        name="head_major_kv",
    )(*[p for p in projs for _ in (1, 2)])
    shape = (n_layers, b, n_keep, N_HEADS, HEAD_DIM)
    return k_out.reshape(shape), v_out.reshape(shape)


def _cast_kernel(x_ref, o_ref):
    o_ref[...] = x_ref[...].astype(o_ref.dtype)


def _cast_bf16(w, rows):
    n_layers, r, c = w.shape
    return pl.pallas_call(
        _cast_kernel,
        grid=(n_layers, r // rows),
        in_specs=[pl.BlockSpec((1, rows, c), lambda l, i: (l, i, 0))],
        out_specs=pl.BlockSpec((1, rows, c), lambda l, i: (l, i, 0)),
        out_shape=jax.ShapeDtypeStruct(w.shape, BF16),
        compiler_params=_params("parallel", "parallel"),
        name="cast_bf16",
    )(w)


def kernel(x_prompt, x_sample, state_pool, state_conv, cache_k, cache_v, norm_w, final_norm_w,
           w_in_ab, pool_lin, pool_scale, conv_w, w_out_ab, w_in_c, w_out_c):
    bp, t, d = x_prompt.shape
    bs, ts, _ = x_sample.shape
    ts_pad = -(-ts // SUBLANES) * SUBLANES
    n_buf = cache_k.shape[2]
    n_keep_p = min(MAX_WINDOW, t)

    hp = x_prompt.reshape(bp * t, d)
    hs = jnp.pad(x_sample, ((0, 0), (0, ts_pad - ts), (0, 0))).reshape(bs * ts_pad, d)
    zero_pool = jnp.zeros((bp, POOL_HIST, D_POOL), F32)
    zero_conv = jnp.zeros((bp, CONV_HIST, D_CONV), F32)

    tm_p, tn = 1024, 1024
    tm_s = bs * ts_pad
    cast_rows = 256

    w_in_ab = _cast_bf16(w_in_ab, cast_rows)
    w_out_ab = _cast_bf16(w_out_ab, cast_rows)
    w_in_c = _cast_bf16(w_in_c, cast_rows)
    w_out_c = _cast_bf16(w_out_c, cast_rows)
    pool_lin = _cast_bf16(pool_lin.reshape(pool_lin.shape[0], -1, D_POOL_GROUP), cast_rows).reshape(pool_lin.shape)

    def head_rows(a):
        return a.reshape(a.shape[0], a.shape[1] * N_HEADS, HEAD_DIM)

    pool_p, pool_s, conv_p, conv_s = [], [], [], []
    projs_c, k_new, v_new = [], [], []
    for l in range(DEPTH):
        i = l // 2
        nw = norm_w[l].reshape(1, d)
        if l % 2 == 0:
            pscale = pool_scale[i].reshape(1, D_POOL)
            hp, pp, cp = _ab_layer(hp.reshape(bp, t, d), nw, w_in_ab, w_out_ab, i, zero_pool, zero_conv,
                                   pool_lin[i], pscale, conv_w[i], AB_ROW_TILE, 0, t)
            hs, ps, cs = _ab_layer(hs.reshape(bs, ts_pad, d), nw, w_in_ab, w_out_ab, i, state_pool[i],
                                   state_conv[i], pool_lin[i], pscale, conv_w[i], ts_pad, PAST_LEN, ts)
            hp = hp.reshape(bp * t, d)
            hs = hs.reshape(bs * ts_pad, d)
            pool_p.append(pp)
            pool_s.append(ps)
            conv_p.append(cp)
            conv_s.append(cs)
        else:
            proj_p = _norm_matmul(hp, nw, w_in_c, i, tm_p, tn).reshape(bp, t, -1)
            proj_s = _norm_matmul(hs, nw, w_in_c, i, tm_s, tn).reshape(bs, ts_pad, -1)[:, :ts]
            og_p = _attn_prompt(proj_p, BF16)
            q_s, kn_s, vn_s, g_s = (head_rows(proj_s[..., c * D_ATTN:(c + 1) * D_ATTN]) for c in range(4))
            og_s = _attn_sample(q_s, g_s, kn_s, vn_s, cache_k, cache_v, i, SAMPLE_KEY_TILE)
            og_s = jnp.pad(og_s.reshape(bs, ts, D_ATTN), ((0, 0), (0, ts_pad - ts), (0, 0)))
            hp = _matmul_residual(og_p.reshape(bp * t, -1), w_out_c, i, hp, tm_p, tn)
            hs = _matmul_residual(og_s.reshape(bs * ts_pad, -1), w_out_c, i, hs, tm_s, tn)
            projs_c.append(proj_p)
            k_new.append(kn_s.reshape(bs, ts, N_HEADS, HEAD_DIM))
            v_new.append(vn_s.reshape(bs, ts, N_HEADS, HEAD_DIM))

    assert min(MAX_WINDOW, n_buf + ts) == n_buf
    k_s, v_s = _cache_roll(cache_k, cache_v, jnp.stack(k_new), jnp.stack(v_new), ROLL_TILE)
    k_p, v_p = _head_major_kv(projs_c, n_keep_p, 256)

    fw = final_norm_w.reshape(1, d)
    y_prompt = _rmsnorm(hp, fw, 512).reshape(bp, t, d)
    y_sample = _rmsnorm(hs, fw, tm_s).reshape(bs, ts_pad, d)[:, :ts]
    return (y_prompt, y_sample, jnp.stack(pool_p), jnp.stack(pool_s), jnp.stack(conv_p), jnp.stack(conv_s),
            k_p, k_s, v_p, v_s)
```

```python
import functools

import jax
import jax.numpy as jnp
from jax import lax
from jax.experimental import pallas as pl
from jax.experimental.pallas import tpu as pltpu

F32 = jnp.float32
BF16 = jnp.bfloat16

D_MODEL = 2048
DEPTH = 4
PAST_LEN = 16384
D_POOL = D_MODEL // 2
POOL_WINDOWS = (2, 4, 8, 16)
D_POOL_GROUP = D_POOL // len(POOL_WINDOWS)
POOL_HIST = max(POOL_WINDOWS) - 1
D_CONV = D_MODEL // 2
CONV_WIDTH = 3
CONV_HIST = CONV_WIDTH - 1
HEAD_DIM = 128
N_HEADS = D_MODEL // HEAD_DIM
D_ATTN = N_HEADS * HEAD_DIM
DILATED_GROUPS = ((128, 1), (512, 4), (2048, 16))
MAX_WINDOW = max(w for w, _ in DILATED_GROUPS)
Q_BLOCK = 128
ATTN_SCALE = HEAD_DIM ** -0.5
LOG2_E = 1.4426950408889634
RMS_EPS = 1e-6
NEG_INF = -1e30

SUBLANES = 8
POOL_PAD = 16
CONV_PAD = 8
DEINT_CHUNK = 2 * Q_BLOCK
SAMPLE_KEY_TILE = 512
ROLL_TILE = 256
SCORE_BLOCKS_PER_TRIP = 16
VALUE_BLOCKS_PER_TRIP = 8
MERGE_BLOCKS_PER_TRIP = 8
VMEM_LIMIT = 48 * 1024 * 1024
AB_LAYER_VMEM_LIMIT = 56 * 1024 * 1024
AB_ROW_TILE = 256
AB_OUT_CHUNK = 512
AB_PROJ_CHUNK = 1024


def _silu(x):
    return x * jax.nn.sigmoid(x)


def _params(*semantics):
    return pltpu.CompilerParams(dimension_semantics=semantics, vmem_limit_bytes=VMEM_LIMIT)


def _norm_matmul_kernel(x_ref, nw_ref, w_ref, o_ref, xn_ref):
    @pl.when(pl.program_id(1) == 0)
    def _():
        x = x_ref[...]
        ms = jnp.mean(x * x, axis=-1, keepdims=True)
        xn_ref[...] = (x * lax.rsqrt(ms + RMS_EPS) * nw_ref[...]).astype(BF16)

    o_ref[...] = jnp.dot(xn_ref[...], w_ref[...], preferred_element_type=F32)


def _norm_matmul(x, nw, w, layer, tm, tn):
    m, d = x.shape
    n = w.shape[2]
    return pl.pallas_call(
        _norm_matmul_kernel,
        grid=(m // tm, n // tn),
        in_specs=[
            pl.BlockSpec((tm, d), lambda i, j: (i, 0)),
            pl.BlockSpec((1, d), lambda i, j: (0, 0)),
            pl.BlockSpec((None, d, tn), lambda i, j: (layer, 0, j)),
        ],
        out_specs=pl.BlockSpec((tm, tn), lambda i, j: (i, j)),
        out_shape=jax.ShapeDtypeStruct((m, n), F32),
        scratch_shapes=[pltpu.VMEM((tm, d), BF16)],
        compiler_params=_params("parallel", "arbitrary"),
        name="norm_matmul",
    )(x, nw, w)


def _matmul_residual_kernel(a_ref, w_ref, h_ref, o_ref):
    o_ref[...] = h_ref[...] + jnp.dot(a_ref[...].astype(BF16), w_ref[...], preferred_element_type=F32)


def _matmul_residual(a, w, layer, h, tm, tn):
    m, k = a.shape
    n = w.shape[2]
    return pl.pallas_call(
        _matmul_residual_kernel,
        grid=(m // tm, n // tn),
        in_specs=[
            pl.BlockSpec((tm, k), lambda i, j: (i, 0)),
            pl.BlockSpec((None, k, tn), lambda i, j: (layer, 0, j)),
            pl.BlockSpec((tm, tn), lambda i, j: (i, j)),
        ],
        out_specs=pl.BlockSpec((tm, tn), lambda i, j: (i, j)),
        out_shape=jax.ShapeDtypeStruct((m, n), F32),
        compiler_params=_params("parallel", "arbitrary"),
        name="matmul_residual",
    )(a, w, h)


def _matmul_residual_norm_kernel(a_ref, w_ref, h_ref, nw_ref, o_ref):
    y = h_ref[...] + jnp.dot(a_ref[...].astype(BF16), w_ref[...], preferred_element_type=F32)
    ms = jnp.mean(y * y, axis=-1, keepdims=True)
    o_ref[...] = y * lax.rsqrt(ms + RMS_EPS) * nw_ref[...]


def _matmul_residual_norm(a, w, layer, h, nw, tm):
    m, k = a.shape
    n = w.shape[2]
    return pl.pallas_call(
        _matmul_residual_norm_kernel,
        grid=(m // tm,),
        in_specs=[
            pl.BlockSpec((tm, k), lambda i: (i, 0)),
            pl.BlockSpec((None, k, n), lambda i: (layer, 0, 0)),
            pl.BlockSpec((tm, n), lambda i: (i, 0)),
            pl.BlockSpec((1, n), lambda i: (0, 0)),
        ],
        out_specs=pl.BlockSpec((tm, n), lambda i: (i, 0)),
        out_shape=jax.ShapeDtypeStruct((m, n), F32),
        compiler_params=_params("parallel"),
        name="matmul_residual_norm",
    )(a, w, h, nw)


def _mix_history(ph_ref, ch_ref, uext_ref, zext_ref, tm):
    j = pl.program_id(1)

    @pl.when(j == 0)
    def _():
        uext_ref[POOL_PAD - POOL_HIST:POOL_PAD, :] = ph_ref[0]
        zext_ref[CONV_PAD - CONV_HIST:CONV_PAD, :] = ch_ref[0]

    @pl.when(j > 0)
    def _():
        uext_ref[0:POOL_PAD, :] = uext_ref[tm:tm + POOL_PAD, :]
        zext_ref[0:CONV_PAD, :] = zext_ref[tm:tm + CONV_PAD, :]


def _mix_tails(ptail_ref, ctail_ref, uext_ref, zext_ref, t_last):
    @pl.when(pl.program_id(1) == pl.num_programs(1) - 1)
    def _():
        ptail_ref[0] = uext_ref[POOL_PAD + t_last - POOL_HIST:POOL_PAD + t_last, :]
        ctail_ref[0] = zext_ref[CONV_PAD + t_last - CONV_HIST:CONV_PAD + t_last, :]


def _mix_body(proj_ref, plin_ref, pscale_ref, cw_ref, mix_ref, uext_ref, zext_ref, pos0):
    j = pl.program_id(1)
    tm = proj_ref.shape[1]
    gw = D_POOL_GROUP

    uext_ref[POOL_PAD:POOL_PAD + tm, :] = proj_ref[0, :, 0:D_POOL]
    c_off = 2 * D_POOL + D_CONV
    zext_ref[CONV_PAD:CONV_PAD + tm, :] = (
        proj_ref[0, :, c_off:c_off + D_CONV] * proj_ref[0, :, c_off + D_CONV:c_off + 2 * D_CONV])

    pos = (pos0 + j * tm + lax.broadcasted_iota(jnp.int32, (tm, 1), 0)).astype(F32)

    for g, k in enumerate(POOL_WINDOWS):
        c0 = g * gw
        u_g = uext_ref[POOL_PAD:POOL_PAD + tm, c0:c0 + gw]
        s = u_g
        for i in range(1, k):
            s = s + uext_ref[POOL_PAD - i:POOL_PAD - i + tm, c0:c0 + gw]
        cnt = jnp.minimum(float(k), pos + 1.0)
        pooled = s / cnt - u_g
        a = jnp.dot(pooled.astype(BF16), plin_ref[g], preferred_element_type=F32)
        a = a * pscale_ref[:, c0:c0 + gw]
        gate = proj_ref[0, :, D_POOL + c0:D_POOL + c0 + gw]
        mix_ref[0, :, c0:c0 + gw] = (a * _silu(gate)).astype(mix_ref.dtype)

    for c in range(D_CONV // gw):
        c0 = c * gw
        conv = zext_ref[CONV_PAD - 2:CONV_PAD - 2 + tm, c0:c0 + gw] * cw_ref[0:1, c0:c0 + gw]
        conv = conv + zext_ref[CONV_PAD - 1:CONV_PAD - 1 + tm, c0:c0 + gw] * cw_ref[1:2, c0:c0 + gw]
        conv = conv + zext_ref[CONV_PAD:CONV_PAD + tm, c0:c0 + gw] * cw_ref[2:3, c0:c0 + gw]
        b_gate = proj_ref[0, :, 2 * D_POOL + c0:2 * D_POOL + c0 + gw]
        gate = proj_ref[0, :, 2 * D_POOL + 3 * D_CONV + c0:2 * D_POOL + 3 * D_CONV + c0 + gw]
        mix_ref[0, :, D_POOL + c0:D_POOL + c0 + gw] = (b_gate * conv * _silu(gate)).astype(mix_ref.dtype)


def _ab_mix_kernel(proj_ref, ph_ref, ch_ref, plin_ref, pscale_ref, cw_ref,
                   mix_ref, ptail_ref, ctail_ref, uext_ref, zext_ref, *, pos0, t_last):
    _mix_history(ph_ref, ch_ref, uext_ref, zext_ref, proj_ref.shape[1])
    _mix_body(proj_ref, plin_ref, pscale_ref, cw_ref, mix_ref, uext_ref, zext_ref, pos0)
    _mix_tails(ptail_ref, ctail_ref, uext_ref, zext_ref, t_last)


def _ab_mix(proj, pool_hist, conv_hist, plin, pscale, cw, tm, pos0, t_valid, mix_dtype):
    b, t, n = proj.shape
    nj = t // tm
    t_last = t_valid - (nj - 1) * tm
    kern = functools.partial(_ab_mix_kernel, pos0=pos0, t_last=t_last)
    return pl.pallas_call(
        kern,
        grid=(b, nj),
        in_specs=[
            pl.BlockSpec((1, tm, n), lambda i, j: (i, j, 0)),
            pl.BlockSpec((1, POOL_HIST, D_POOL), lambda i, j: (i, 0, 0)),
            pl.BlockSpec((1, CONV_HIST, D_CONV), lambda i, j: (i, 0, 0)),
            pl.BlockSpec(plin.shape, lambda i, j: (0, 0, 0)),
            pl.BlockSpec((1, D_POOL), lambda i, j: (0, 0)),
            pl.BlockSpec((CONV_WIDTH, D_CONV), lambda i, j: (0, 0)),
        ],
        out_specs=[
            pl.BlockSpec((1, tm, D_POOL + D_CONV), lambda i, j: (i, j, 0)),
            pl.BlockSpec((1, POOL_HIST, D_POOL), lambda i, j: (i, 0, 0)),
            pl.BlockSpec((1, CONV_HIST, D_CONV), lambda i, j: (i, 0, 0)),
        ],
        out_shape=[
            jax.ShapeDtypeStruct((b, t, D_POOL + D_CONV), mix_dtype),
            jax.ShapeDtypeStruct((b, POOL_HIST, D_POOL), F32),
            jax.ShapeDtypeStruct((b, CONV_HIST, D_CONV), F32),
        ],
        scratch_shapes=[pltpu.VMEM((tm + POOL_PAD, D_POOL), F32), pltpu.VMEM((tm + CONV_PAD, D_CONV), F32)],
        compiler_params=_params("parallel", "arbitrary"),
        name="ab_mix",
    )(proj, pool_hist, conv_hist, plin, pscale, cw)


def _ab_layer_kernel(x_ref, nw_ref, win_ref, ph_ref, ch_ref, plin_ref, pscale_ref, cw_ref, wout_ref,
                     o_ref, ptail_ref, ctail_ref, proj_ref, mix_ref, uext_ref, zext_ref, *, pos0, t_last):
    _mix_history(ph_ref, ch_ref, uext_ref, zext_ref, x_ref.shape[1])
    x = x_ref[0]
    ms = jnp.mean(x * x, axis=-1, keepdims=True)
    xn = (x * lax.rsqrt(ms + RMS_EPS) * nw_ref[...]).astype(BF16)
    n_in = win_ref.shape[1]
    for c0 in range(0, n_in, AB_PROJ_CHUNK):
        proj_ref[0, :, c0:c0 + AB_PROJ_CHUNK] = jnp.dot(xn, win_ref[:, c0:c0 + AB_PROJ_CHUNK],
                                                        preferred_element_type=F32)
    _mix_body(proj_ref, plin_ref, pscale_ref, cw_ref, mix_ref, uext_ref, zext_ref, pos0)
    y = x
    for c0 in range(0, wout_ref.shape[0], AB_OUT_CHUNK):
        y = y + jnp.dot(mix_ref[0, :, c0:c0 + AB_OUT_CHUNK].astype(BF16), wout_ref[c0:c0 + AB_OUT_CHUNK, :],
                        preferred_element_type=F32)
    o_ref[0] = y
    _mix_tails(ptail_ref, ctail_ref, uext_ref, zext_ref, t_last)


def _ab_layer(x, nw, w_in, w_out, layer, pool_hist, conv_hist, plin, pscale, cw, tm, pos0, t_valid):
    b, t, d = x.shape
    n_in = w_in.shape[2]
    nj = t // tm
    t_last = t_valid - (nj - 1) * tm
    resident = pl.Buffered(1)
    return pl.pallas_call(
        functools.partial(_ab_layer_kernel, pos0=pos0, t_last=t_last),
        grid=(b, nj),
        in_specs=[
            pl.BlockSpec((1, tm, d), lambda i, j: (i, j, 0)),
            pl.BlockSpec((1, d), lambda i, j: (0, 0)),
            pl.BlockSpec((None, d, n_in), lambda i, j: (layer, 0, 0), pipeline_mode=resident),
            pl.BlockSpec((1, POOL_HIST, D_POOL), lambda i, j: (i, 0, 0)),
            pl.BlockSpec((1, CONV_HIST, D_CONV), lambda i, j: (i, 0, 0)),
            pl.BlockSpec(plin.shape, lambda i, j: (0, 0, 0)),
            pl.BlockSpec((1, D_POOL), lambda i, j: (0, 0)),
            pl.BlockSpec((CONV_WIDTH, D_CONV), lambda i, j: (0, 0)),
            pl.BlockSpec((None, D_POOL + D_CONV, d), lambda i, j: (layer, 0, 0), pipeline_mode=resident),
        ],
        out_specs=[
            pl.BlockSpec((1, tm, d), lambda i, j: (i, j, 0)),
            pl.BlockSpec((1, POOL_HIST, D_POOL), lambda i, j: (i, 0, 0)),
            pl.BlockSpec((1, CONV_HIST, D_CONV), lambda i, j: (i, 0, 0)),
        ],
        out_shape=[
            jax.ShapeDtypeStruct((b, t, d), F32),
            jax.ShapeDtypeStruct((b, POOL_HIST, D_POOL), F32),
            jax.ShapeDtypeStruct((b, CONV_HIST, D_CONV), F32),
        ],
        scratch_shapes=[
            pltpu.VMEM((1, tm, n_in), F32),
            pltpu.VMEM((1, tm, D_POOL + D_CONV), BF16 if tm % (2 * SUBLANES) == 0 else F32),
            pltpu.VMEM((tm + POOL_PAD, D_POOL), F32),
            pltpu.VMEM((tm + CONV_PAD, D_CONV), F32),
        ],
        compiler_params=pltpu.CompilerParams(dimension_semantics=("parallel", "arbitrary"),
                                             vmem_limit_bytes=AB_LAYER_VMEM_LIMIT),
        name="ab_layer",
    )(x, nw, w_in, pool_hist, conv_hist, plin, pscale, cw, w_out)


def _rows(start, size, stride):
    if stride == 1:
        return pl.ds(start, size)
    return pl.ds(start, size, stride=stride)


def _attn_prompt_kernel(q_ref, k_ref, v_ref, g_ref, o_ref, kd_ref, vd_ref, bias_ref, s_ref, *state_refs):
    t = q_ref.shape[0]
    n_groups = len(DILATED_GROUPS)
    kf_ref, vf_ref = state_refs[0:2]
    acc_refs = (None,) + tuple(state_refs[2:2 + n_groups - 1])
    stat_refs = (None,) + tuple(state_refs[2 + n_groups - 1:])
    nt_dims = (((1,), (1,)), ((), ()))
    half = HEAD_DIM // 2
    lane = lax.broadcasted_iota(jnp.int32, (Q_BLOCK, HEAD_DIM), 1)

    @pl.when((pl.program_id(0) == 0) & (pl.program_id(1) == 0))
    def _():
        delta = (lax.broadcasted_iota(jnp.int32, (Q_BLOCK, 2 * Q_BLOCK), 0)
                 - lax.broadcasted_iota(jnp.int32, (Q_BLOCK, 2 * Q_BLOCK), 1))
        for sel in range(2):
            dist = delta + sel * Q_BLOCK
            bias_ref[sel] = jnp.where((dist >= 0) & (dist <= Q_BLOCK), 0.0, NEG_INF)
        vd_ref[:, :, HEAD_DIM:] = jnp.ones((n_groups, t, HEAD_DIM), BF16)

    def deinterleave(gi, d, d_src, ksrc_ref, vsrc_ref, keep_f32):
        stream_len = t // d
        ratio = d // d_src

        def body(c, carry):
            dst = pl.multiple_of(c * DEINT_CHUNK, DEINT_CHUNK)
            r = dst // stream_len
            m0 = dst % stream_len
            src = _rows((r % d_src) * (t // d_src) + r // d_src + ratio * m0, DEINT_CHUNK, ratio)
            k = ksrc_ref[src, :]
            v = vsrc_ref[src, :]
            kd_ref[gi, pl.ds(dst, DEINT_CHUNK), :] = k.astype(BF16)
            vd_ref[gi, pl.ds(dst, DEINT_CHUNK), 0:HEAD_DIM] = v.astype(BF16)
            if keep_f32:
                kf_ref[pl.ds(dst, DEINT_CHUNK), :] = k
                vf_ref[pl.ds(dst, DEINT_CHUNK), :] = v
            return carry

        lax.fori_loop(0, t // DEINT_CHUNK, body, 0, unroll=2)

    def merged_output(rows, m0, acc0, l0):
        ms, ls, accs = [m0], [l0], [acc0]
        for gi in range(1, n_groups):
            st = stat_refs[gi][rows, :]
            sw = pltpu.roll(st, half, axis=1)
            ms.append(jnp.where(lane < half, st, sw))
            ls.append(jnp.where(lane < half, sw, st))
            accs.append(acc_refs[gi][rows, :])
        m_all = functools.reduce(jnp.maximum, ms)
        num = None
        den = None
        for m_g, l_g, acc_g in zip(ms, ls, accs):
            w = jnp.exp2(m_g - m_all)
            num = w * acc_g if num is None else num + w * acc_g
            den = w * l_g if den is None else den + w * l_g
        g = g_ref[rows, :]
        return ((num * g) / (den * (1.0 + jnp.exp(-g)))).astype(o_ref.dtype)

    def run_group(gi, d):
        stream_len = t // d
        nb = stream_len // Q_BLOCK

        def block_rows(idx):
            r = idx // nb
            mb = idx % nb
            kb = jnp.maximum(mb - 1, 0)
            q_start = r + d * Q_BLOCK * mb
            q_rows = _rows(pl.multiple_of(q_start, Q_BLOCK) if d == 1 else q_start, Q_BLOCK, d)
            k_rows = pl.ds(pl.multiple_of(r * stream_len + kb * Q_BLOCK, Q_BLOCK), 2 * Q_BLOCK)
            return q_rows, k_rows, mb - kb

        def scores(it, carry):
            s = []
            for u in range(SCORE_BLOCKS_PER_TRIP):
                q_rows, k_rows, sel = block_rows(it * SCORE_BLOCKS_PER_TRIP + u)
                q = (q_ref[q_rows, :] * (ATTN_SCALE * LOG2_E)).astype(BF16)
                s.append(lax.dot_general(q, kd_ref[gi, k_rows, :], nt_dims, preferred_element_type=F32)
                         + bias_ref[sel])
            for u in range(SCORE_BLOCKS_PER_TRIP):
                s_ref[it * SCORE_BLOCKS_PER_TRIP + u] = s[u]
            return carry

        n_val = MERGE_BLOCKS_PER_TRIP if gi == 0 else VALUE_BLOCKS_PER_TRIP

        def values(it, carry):
            rows, s, v = [], [], []
            for u in range(n_val):
                q_rows, k_rows, _ = block_rows(it * n_val + u)
                rows.append(q_rows)
                s.append(s_ref[it * n_val + u])
                v.append(vd_ref[gi, k_rows, :])
            outs = []
            for u in range(n_val):
                m_blk = jnp.max(s[u], axis=-1, keepdims=True)
                p = jnp.exp2(s[u] - m_blk).astype(BF16)
                pv = jnp.dot(p, v[u], preferred_element_type=F32)
                acc, l_blk = pv[:, 0:HEAD_DIM], pv[:, HEAD_DIM:]
                if gi == 0:
                    outs.append((merged_output(rows[u], jnp.broadcast_to(m_blk, acc.shape), acc, l_blk),))
                else:
                    outs.append((acc, jnp.where(lane < half, m_blk, l_blk)))
            for u in range(n_val):
                if gi == 0:
                    o_ref[rows[u], :] = outs[u][0]
                else:
                    acc_refs[gi][rows[u], :] = outs[u][0]
                    stat_refs[gi][rows[u], :] = outs[u][1]
            return carry

        lax.fori_loop(0, d * nb // SCORE_BLOCKS_PER_TRIP, scores, 0)
        lax.fori_loop(0, d * nb // n_val, values, 0)

    ksrc_ref, vsrc_ref, d_src = k_ref, v_ref, 1
    for gi, (_, d) in enumerate(DILATED_GROUPS):
        keep_f32 = 0 < gi < n_groups - 1
        deinterleave(gi, d, d_src, ksrc_ref, vsrc_ref, keep_f32)
        if keep_f32:
            ksrc_ref, vsrc_ref, d_src = kf_ref, vf_ref, d
    for gi in reversed(range(n_groups)):
        run_group(gi, DILATED_GROUPS[gi][1])


def _attn_prompt(proj, out_dtype):
    b, t, _ = proj.shape
    n_groups = len(DILATED_GROUPS)
    dils = [dil for _, dil in DILATED_GROUPS]
    assert n_groups == 3 and dils[0] == 1 and all(hi % lo == 0 for lo, hi in zip(dils, dils[1:]))
    for window, dil in DILATED_GROUPS:
        assert window // dil == Q_BLOCK and (t // dil) % DEINT_CHUNK == 0
        assert (t // Q_BLOCK) % SCORE_BLOCKS_PER_TRIP == 0 and (t // Q_BLOCK) % VALUE_BLOCKS_PER_TRIP == 0
    blk = (None, t, HEAD_DIM)
    return pl.pallas_call(
        _attn_prompt_kernel,
        grid=(b, N_HEADS),
        in_specs=[
            pl.BlockSpec(blk, lambda i, h: (i, 0, h)),
            pl.BlockSpec(blk, lambda i, h: (i, 0, N_HEADS + h)),
            pl.BlockSpec(blk, lambda i, h: (i, 0, 2 * N_HEADS + h)),
            pl.BlockSpec(blk, lambda i, h: (i, 0, 3 * N_HEADS + h)),
        ],
        out_specs=pl.BlockSpec(blk, lambda i, h: (i, 0, h)),
        out_shape=jax.ShapeDtypeStruct((b, t, D_ATTN), out_dtype),
        scratch_shapes=(
            [pltpu.VMEM((n_groups, t, HEAD_DIM), BF16), pltpu.VMEM((n_groups, t, 2 * HEAD_DIM), BF16)]
            + [pltpu.VMEM((2, Q_BLOCK, 2 * Q_BLOCK), F32)]
            + [pltpu.VMEM((t // Q_BLOCK, Q_BLOCK, 2 * Q_BLOCK), F32)]
            + [pltpu.VMEM((t, HEAD_DIM), F32)] * (2 + 2 * (n_groups - 1))),
        compiler_params=_params("arbitrary", "arbitrary"),
        name="attn_prompt",
    )(proj, proj, proj, proj)


def _group_count(dist):
    cnt = jnp.zeros(dist.shape, F32)
    for window, dil in DILATED_GROUPS:
        hit = (dist >= 0) & (dist <= window) & ((dist & (dil - 1)) == 0)
        cnt = cnt + hit.astype(F32)
    return cnt


def _attn_sample_kernel(q_ref, g_ref, kn_ref, vn_ref, kc_ref, vc_ref, o_ref, m_ref, l_ref, acc_ref):
    j = pl.program_id(1)
    nj = pl.num_programs(1)
    tk = kc_ref.shape[0]
    n_buf = tk * nj
    n_rows = q_ref.shape[0]
    nt_dims = (((1,), (1,)), ((), ()))
    head_bits = N_HEADS.bit_length() - 1

    @pl.when(j == 0)
    def _():
        m_ref[...] = jnp.full(m_ref.shape, NEG_INF, F32)
        l_ref[...] = jnp.zeros(l_ref.shape, F32)
        acc_ref[...] = jnp.zeros(acc_ref.shape, F32)

    q = (q_ref[...] * ATTN_SCALE).astype(BF16)

    def accumulate(k2, v2, key_pos0):
        n_cols = k2.shape[0]
        row = lax.broadcasted_iota(jnp.int32, (n_rows, n_cols), 0)
        col = lax.broadcasted_iota(jnp.int32, (n_rows, n_cols), 1)
        same_head = (row & (N_HEADS - 1)) == (col & (N_HEADS - 1))
        dist = (n_buf + (row >> head_bits)) - (key_pos0 + (col >> head_bits))
        cnt = jnp.where(same_head, _group_count(dist), 0.0)
        s = lax.dot_general(q, k2.astype(BF16), nt_dims, preferred_element_type=F32)
        s = jnp.where(cnt > 0, s, NEG_INF)
        m_prev = m_ref[...]
        m_new = jnp.maximum(m_prev, jnp.max(s, axis=-1, keepdims=True))
        alpha = jnp.exp(m_prev - m_new)
        p = cnt * jnp.exp(s - m_new)
        l_ref[...] = alpha * l_ref[...] + jnp.sum(p, axis=-1, keepdims=True)
        acc_ref[...] = alpha * acc_ref[...] + jnp.dot(p.astype(BF16), v2.astype(BF16),
                                                      preferred_element_type=F32)
        m_ref[...] = m_new

    accumulate(kc_ref[...].reshape(tk * N_HEADS, HEAD_DIM), vc_ref[...].reshape(tk * N_HEADS, HEAD_DIM),
               j * tk)

    @pl.when(j == nj - 1)
    def _():
        accumulate(kn_ref[...], vn_ref[...], n_buf)
        o_ref[...] = (acc_ref[...] / l_ref[...]) * _silu(g_ref[...])


def _attn_sample(q2, g2, kn2, vn2, cache_k, cache_v, layer, tk):
    b, n_rows, _ = q2.shape
    n_buf = cache_k.shape[2]
    rblk = (None, n_rows, HEAD_DIM)
    cblk = (None, None, tk, N_HEADS, HEAD_DIM)
    return pl.pallas_call(
        _attn_sample_kernel,
        grid=(b, n_buf // tk),
        in_specs=[pl.BlockSpec(rblk, lambda i, j: (i, 0, 0))] * 4
        + [pl.BlockSpec(cblk, lambda i, j: (layer, i, j, 0, 0))] * 2,
        out_specs=pl.BlockSpec(rblk, lambda i, j: (i, 0, 0)),
        out_shape=jax.ShapeDtypeStruct((b, n_rows, HEAD_DIM), F32),
        scratch_shapes=[pltpu.VMEM((n_rows, 1), F32), pltpu.VMEM((n_rows, 1), F32),
                        pltpu.VMEM((n_rows, HEAD_DIM), F32)],
        compiler_params=_params("parallel", "arbitrary"),
        name="attn_sample",
    )(q2, g2, kn2, vn2, cache_k, cache_v)


def _cache_roll_kernel(ck_ref, ck_next_ref, kn_ref, cv_ref, cv_next_ref, vn_ref, ko_ref, vo_ref):
    j = pl.program_id(2)
    tk = ck_ref.shape[0]
    n_new = kn_ref.shape[0]
    for cur, nxt, new, out in ((ck_ref, ck_next_ref, kn_ref, ko_ref), (cv_ref, cv_next_ref, vn_ref, vo_ref)):
        out[0:tk - n_new] = cur[n_new:tk]

        @pl.when(j < pl.num_programs(2) - 1)
        def _(nxt=nxt, out=out):
            out[tk - n_new:tk] = nxt[0:n_new]

        @pl.when(j == pl.num_programs(2) - 1)
        def _(new=new, out=out):
            out[tk - n_new:tk] = new[...]


def _cache_roll(cache_k, cache_v, k_new, v_new, tk):
    n_layers, n_batch, n_buf, nh, hd = cache_k.shape
    n_new = k_new.shape[2]
    assert n_new <= SUBLANES and tk % SUBLANES == 0 and n_buf % tk == 0
    last_next = n_buf // SUBLANES - 1
    cur_spec = pl.BlockSpec((None, None, tk, nh, hd), lambda l, b, j: (l, b, j, 0, 0))
    next_spec = pl.BlockSpec((None, None, SUBLANES, nh, hd),
                             lambda l, b, j: (l, b, jnp.minimum((j + 1) * (tk // SUBLANES), last_next), 0, 0))
    new_spec = pl.BlockSpec((None, None, n_new, nh, hd), lambda l, b, j: (l, b, 0, 0, 0))
    out = jax.ShapeDtypeStruct(cache_k.shape, cache_k.dtype)
    return pl.pallas_call(
        _cache_roll_kernel,
        grid=(n_layers, n_batch, n_buf // tk),
        in_specs=[cur_spec, next_spec, new_spec] * 2,
        out_specs=[cur_spec] * 2,
        out_shape=[out, out],
        compiler_params=_params("parallel", "parallel", "arbitrary"),
        name="cache_roll",
    )(cache_k, cache_k, k_new, cache_v, cache_v, v_new)


def _head_major_kernel(*refs, n_layers):
    ko_ref, vo_ref = refs[2 * n_layers:]
    tq = refs[0].shape[0]
    for layer in range(n_layers):
        @pl.when(pl.program_id(0) == layer)
        def _(layer=layer):
            for src, dst in ((refs[2 * layer], ko_ref), (refs[2 * layer + 1], vo_ref)):
                for h in range(N_HEADS):
                    dst[pl.ds(h, tq, stride=N_HEADS), :] = src[:, h * HEAD_DIM:(h + 1) * HEAD_DIM]


def _head_major_kv(projs, n_keep, tq):
    n_layers = len(projs)
    b, t, _ = projs[0].shape
    first = (t - n_keep) // tq
    nj = n_keep // tq
    in_specs = []
    for layer in range(n_layers):
        for col in (1, 2):
            in_specs.append(pl.BlockSpec(
                (None, tq, D_ATTN),
                lambda l, i, j, layer=layer, col=col: (jnp.where(l == layer, i, 0),
                                                       jnp.where(l == layer, first + j, 0), col)))
    out_spec = pl.BlockSpec((None, None, tq * N_HEADS, HEAD_DIM), lambda l, i, j: (l, i, j, 0))
    out = jax.ShapeDtypeStruct((n_layers, b, n_keep * N_HEADS, HEAD_DIM), F32)
    k_out, v_out = pl.pallas_call(
        functools.partial(_head_major_kernel, n_layers=n_layers),
        grid=(n_layers, b, nj),
        in_specs=in_specs,
        out_specs=[out_spec, out_spec],
        out_shape=[out, out],
        compiler_params=_params("arbitrary", "arbitrary", "arbitrary"),
        name="head_major_kv",
    )(*[p for p in projs for _ in (1, 2)])
    shape = (n_layers, b, n_keep, N_HEADS, HEAD_DIM)
    return k_out.reshape(shape), v_out.reshape(shape)


def _cast_kernel(x_ref, o_ref):
    o_ref[...] = x_ref[...].astype(o_ref.dtype)


def _cast_bf16(w, rows):
    n_layers, r, c = w.shape
    return pl.pallas_call(
        _cast_kernel,
        grid=(n_layers, r // rows),
        in_specs=[pl.BlockSpec((1, rows, c), lambda l, i: (l, i, 0))],
        out_specs=pl.BlockSpec((1, rows, c), lambda l, i: (l, i, 0)),
        out_shape=jax.ShapeDtypeStruct(w.shape, BF16),
        compiler_params=_params("parallel", "parallel"),
        name="cast_bf16",
    )(w)


def kernel(x_prompt, x_sample, state_pool, state_conv, cache_k, cache_v, norm_w, final_norm_w,
           w_in_ab, pool_lin, pool_scale, conv_w, w_out_ab, w_in_c, w_out_c):
    bp, t, d = x_prompt.shape
    bs, ts, _ = x_sample.shape
    ts_pad = -(-ts // SUBLANES) * SUBLANES
    n_buf = cache_k.shape[2]
    n_keep_p = min(MAX_WINDOW, t)

    hp = x_prompt.reshape(bp * t, d)
    hs = jnp.pad(x_sample, ((0, 0), (0, ts_pad - ts), (0, 0))).reshape(bs * ts_pad, d)
    zero_pool = jnp.zeros((bp, POOL_HIST, D_POOL), F32)
    zero_conv = jnp.zeros((bp, CONV_HIST, D_CONV), F32)

    tm_p, tn = 1024, 1024
    tm_s = bs * ts_pad
    cast_rows = 256

    w_in_ab = _cast_bf16(w_in_ab, cast_rows)
    w_out_ab = _cast_bf16(w_out_ab, cast_rows)
    w_in_c = _cast_bf16(w_in_c, cast_rows)
    w_out_c = _cast_bf16(w_out_c, cast_rows)
    pool_lin = _cast_bf16(pool_lin.reshape(pool_lin.shape[0], -1, D_POOL_GROUP), cast_rows).reshape(pool_lin.shape)

    def head_rows(a):
        return a.reshape(a.shape[0], a.shape[1] * N_HEADS, HEAD_DIM)

    assert DEPTH % 2 == 0
    fw = final_norm_w.reshape(1, d)
    pool_p, pool_s, conv_p, conv_s = [], [], [], []
    projs_c, k_new, v_new = [], [], []
    for l in range(DEPTH):
        i = l // 2
        nw = norm_w[l].reshape(1, d)
        if l % 2 == 0:
            pscale = pool_scale[i].reshape(1, D_POOL)
            hp, pp, cp = _ab_layer(hp.reshape(bp, t, d), nw, w_in_ab, w_out_ab, i, zero_pool, zero_conv,
                                   pool_lin[i], pscale, conv_w[i], AB_ROW_TILE, 0, t)
            hp = hp.reshape(bp * t, d)
            proj_s = _norm_matmul(hs, nw, w_in_ab, i, tm_s, tn).reshape(bs, ts_pad, -1)
            mix_s, ps, cs = _ab_mix(proj_s, state_pool[i], state_conv[i], pool_lin[i], pscale, conv_w[i],
                                    ts_pad, PAST_LEN, ts, F32)
            hs = _matmul_residual(mix_s.reshape(bs * ts_pad, -1), w_out_ab, i, hs, tm_s, tn)
            pool_p.append(pp)
            pool_s.append(ps)
            conv_p.append(cp)
            conv_s.append(cs)
        else:
            proj_p = _norm_matmul(hp, nw, w_in_c, i, tm_p, tn).reshape(bp, t, -1)
            proj_s = _norm_matmul(hs, nw, w_in_c, i, tm_s, tn).reshape(bs, ts_pad, -1)[:, :ts]
            og_p = _attn_prompt(proj_p, BF16)
            q_s, kn_s, vn_s, g_s = (head_rows(proj_s[..., c * D_ATTN:(c + 1) * D_ATTN]) for c in range(4))
            og_s = _attn_sample(q_s, g_s, kn_s, vn_s, cache_k, cache_v, i, SAMPLE_KEY_TILE)
            og_s = jnp.pad(og_s.reshape(bs, ts, D_ATTN), ((0, 0), (0, ts_pad - ts), (0, 0)))
            if l == DEPTH - 1:
                hp = _matmul_residual_norm(og_p.reshape(bp * t, -1), w_out_c, i, hp, fw, 512)
                hs = _matmul_residual_norm(og_s.reshape(bs * ts_pad, -1), w_out_c, i, hs, fw, tm_s)
            else:
                hp = _matmul_residual(og_p.reshape(bp * t, -1), w_out_c, i, hp, tm_p, tn)
                hs = _matmul_residual(og_s.reshape(bs * ts_pad, -1), w_out_c, i, hs, tm_s, tn)
            projs_c.append(proj_p)
            k_new.append(kn_s.reshape(bs, ts, N_HEADS, HEAD_DIM))
            v_new.append(vn_s.reshape(bs, ts, N_HEADS, HEAD_DIM))

    assert min(MAX_WINDOW, n_buf + ts) == n_buf
    k_s, v_s = _cache_roll(cache_k, cache_v, jnp.stack(k_new), jnp.stack(v_new), ROLL_TILE)
    k_p, v_p = _head_major_kv(projs_c, n_keep_p, 256)

    y_prompt = hp.reshape(bp, t, d)
    y_sample = hs.reshape(bs, ts_pad, d)[:, :ts]
    return (y_prompt, y_sample, jnp.stack(pool_p), jnp.stack(pool_s), jnp.stack(conv_p), jnp.stack(conv_s),
            k_p, k_s, v_p, v_s)
```

```python
import functools

import jax
import jax.numpy as jnp
from jax import lax
from jax.experimental import pallas as pl
from jax.experimental.pallas import tpu as pltpu

F32 = jnp.float32
BF16 = jnp.bfloat16

D_MODEL = 2048
DEPTH = 4
PAST_LEN = 16384
D_POOL = D_MODEL // 2
POOL_WINDOWS = (2, 4, 8, 16)
D_POOL_GROUP = D_POOL // len(POOL_WINDOWS)
POOL_HIST = max(POOL_WINDOWS) - 1
D_CONV = D_MODEL // 2
CONV_WIDTH = 3
CONV_HIST = CONV_WIDTH - 1
HEAD_DIM = 128
N_HEADS = D_MODEL // HEAD_DIM
D_ATTN = N_HEADS * HEAD_DIM
DILATED_GROUPS = ((128, 1), (512, 4), (2048, 16))
MAX_WINDOW = max(w for w, _ in DILATED_GROUPS)
Q_BLOCK = 128
ATTN_SCALE = HEAD_DIM ** -0.5
LOG2_E = 1.4426950408889634
RMS_EPS = 1e-6
NEG_INF = -1e30

SUBLANES = 8
POOL_PAD = 16
CONV_PAD = 8
DEINT_CHUNK = 2 * Q_BLOCK
SAMPLE_KEY_TILE = 512
ROLL_CHUNK = 128
ROLL_RING = 4
SCORE_BLOCKS_PER_TRIP = 16
VALUE_BLOCKS_PER_TRIP = 8
MERGE_BLOCKS_PER_TRIP = 8
VMEM_LIMIT = 48 * 1024 * 1024
AB_LAYER_VMEM_LIMIT = 56 * 1024 * 1024
ATTN_VMEM_LIMIT = 54 * 1024 * 1024
AB_ROW_TILE = 256
AB_OUT_CHUNK = 512
AB_PROJ_CHUNK = 1024


def _silu(x):
    return x * jax.nn.sigmoid(x)


def _params(*semantics):
    return pltpu.CompilerParams(dimension_semantics=semantics, vmem_limit_bytes=VMEM_LIMIT)


def _norm_matmul_kernel(x_ref, nw_ref, w_ref, o_ref, xn_ref):
    @pl.when(pl.program_id(1) == 0)
    def _():
        x = x_ref[...]
        ms = jnp.mean(x * x, axis=-1, keepdims=True)
        xn_ref[...] = (x * lax.rsqrt(ms + RMS_EPS) * nw_ref[...]).astype(BF16)

    o_ref[...] = jnp.dot(xn_ref[...], w_ref[...], preferred_element_type=F32)


def _norm_matmul(x, nw, w, layer, tm, tn):
    m, d = x.shape
    n = w.shape[2]
    return pl.pallas_call(
        _norm_matmul_kernel,
        grid=(m // tm, n // tn),
        in_specs=[
            pl.BlockSpec((tm, d), lambda i, j: (i, 0)),
            pl.BlockSpec((1, d), lambda i, j: (0, 0)),
            pl.BlockSpec((None, d, tn), lambda i, j: (layer, 0, j)),
        ],
        out_specs=pl.BlockSpec((tm, tn), lambda i, j: (i, j)),
        out_shape=jax.ShapeDtypeStruct((m, n), F32),
        scratch_shapes=[pltpu.VMEM((tm, d), BF16)],
        compiler_params=_params("parallel", "arbitrary"),
        name="norm_matmul",
    )(x, nw, w)


def _matmul_residual_kernel(a_ref, w_ref, h_ref, o_ref):
    o_ref[...] = h_ref[...] + jnp.dot(a_ref[...].astype(BF16), w_ref[...], preferred_element_type=F32)


def _matmul_residual(a, w, layer, h, tm, tn):
    m, k = a.shape
    n = w.shape[2]
    return pl.pallas_call(
        _matmul_residual_kernel,
        grid=(m // tm, n // tn),
        in_specs=[
            pl.BlockSpec((tm, k), lambda i, j: (i, 0)),
            pl.BlockSpec((None, k, tn), lambda i, j: (layer, 0, j)),
            pl.BlockSpec((tm, tn), lambda i, j: (i, j)),
        ],
        out_specs=pl.BlockSpec((tm, tn), lambda i, j: (i, j)),
        out_shape=jax.ShapeDtypeStruct((m, n), F32),
        compiler_params=_params("parallel", "arbitrary"),
        name="matmul_residual",
    )(a, w, h)


def _matmul_residual_norm_kernel(a_ref, w_ref, h_ref, nw_ref, o_ref):
    y = h_ref[...] + jnp.dot(a_ref[...].astype(BF16), w_ref[...], preferred_element_type=F32)
    ms = jnp.mean(y * y, axis=-1, keepdims=True)
    o_ref[...] = y * lax.rsqrt(ms + RMS_EPS) * nw_ref[...]


def _matmul_residual_norm(a, w, layer, h, nw, tm):
    m, k = a.shape
    n = w.shape[2]
    return pl.pallas_call(
        _matmul_residual_norm_kernel,
        grid=(m // tm,),
        in_specs=[
            pl.BlockSpec((tm, k), lambda i: (i, 0)),
            pl.BlockSpec((None, k, n), lambda i: (layer, 0, 0)),
            pl.BlockSpec((tm, n), lambda i: (i, 0)),
            pl.BlockSpec((1, n), lambda i: (0, 0)),
        ],
        out_specs=pl.BlockSpec((tm, n), lambda i: (i, 0)),
        out_shape=jax.ShapeDtypeStruct((m, n), F32),
        compiler_params=_params("parallel"),
        name="matmul_residual_norm",
    )(a, w, h, nw)


def _mix_history(ph_ref, ch_ref, uext_ref, zext_ref, tm):
    j = pl.program_id(1)

    @pl.when(j == 0)
    def _():
        uext_ref[POOL_PAD - POOL_HIST:POOL_PAD, :] = ph_ref[0]
        zext_ref[CONV_PAD - CONV_HIST:CONV_PAD, :] = ch_ref[0]

    @pl.when(j > 0)
    def _():
        uext_ref[0:POOL_PAD, :] = uext_ref[tm:tm + POOL_PAD, :]
        zext_ref[0:CONV_PAD, :] = zext_ref[tm:tm + CONV_PAD, :]


def _mix_tails(ptail_ref, ctail_ref, uext_ref, zext_ref, t_last):
    @pl.when(pl.program_id(1) == pl.num_programs(1) - 1)
    def _():
        ptail_ref[0] = uext_ref[POOL_PAD + t_last - POOL_HIST:POOL_PAD + t_last, :]
        ctail_ref[0] = zext_ref[CONV_PAD + t_last - CONV_HIST:CONV_PAD + t_last, :]


def _mix_body(proj_ref, plin_ref, pscale_ref, cw_ref, mix_ref, uext_ref, zext_ref, pos0):
    j = pl.program_id(1)
    tm = proj_ref.shape[1]
    gw = D_POOL_GROUP

    uext_ref[POOL_PAD:POOL_PAD + tm, :] = proj_ref[0, :, 0:D_POOL]
    c_off = 2 * D_POOL + D_CONV
    zext_ref[CONV_PAD:CONV_PAD + tm, :] = (
        proj_ref[0, :, c_off:c_off + D_CONV] * proj_ref[0, :, c_off + D_CONV:c_off + 2 * D_CONV])

    pos = (pos0 + j * tm + lax.broadcasted_iota(jnp.int32, (tm, 1), 0)).astype(F32)

    for g, k in enumerate(POOL_WINDOWS):
        c0 = g * gw
        u_g = uext_ref[POOL_PAD:POOL_PAD + tm, c0:c0 + gw]
        s = u_g
        for i in range(1, k):
            s = s + uext_ref[POOL_PAD - i:POOL_PAD - i + tm, c0:c0 + gw]
        cnt = jnp.minimum(float(k), pos + 1.0)
        pooled = s / cnt - u_g
        a = jnp.dot(pooled.astype(BF16), plin_ref[g], preferred_element_type=F32)
        a = a * pscale_ref[:, c0:c0 + gw]
        gate = proj_ref[0, :, D_POOL + c0:D_POOL + c0 + gw]
        mix_ref[0, :, c0:c0 + gw] = (a * _silu(gate)).astype(mix_ref.dtype)

    for c in range(D_CONV // gw):
        c0 = c * gw
        conv = zext_ref[CONV_PAD - 2:CONV_PAD - 2 + tm, c0:c0 + gw] * cw_ref[0:1, c0:c0 + gw]
        conv = conv + zext_ref[CONV_PAD - 1:CONV_PAD - 1 + tm, c0:c0 + gw] * cw_ref[1:2, c0:c0 + gw]
        conv = conv + zext_ref[CONV_PAD:CONV_PAD + tm, c0:c0 + gw] * cw_ref[2:3, c0:c0 + gw]
        b_gate = proj_ref[0, :, 2 * D_POOL + c0:2 * D_POOL + c0 + gw]
        gate = proj_ref[0, :, 2 * D_POOL + 3 * D_CONV + c0:2 * D_POOL + 3 * D_CONV + c0 + gw]
        mix_ref[0, :, D_POOL + c0:D_POOL + c0 + gw] = (b_gate * conv * _silu(gate)).astype(mix_ref.dtype)


def _ab_mix_kernel(proj_ref, ph_ref, ch_ref, plin_ref, pscale_ref, cw_ref,
                   mix_ref, ptail_ref, ctail_ref, uext_ref, zext_ref, *, pos0, t_last):
    _mix_history(ph_ref, ch_ref, uext_ref, zext_ref, proj_ref.shape[1])
    _mix_body(proj_ref, plin_ref, pscale_ref, cw_ref, mix_ref, uext_ref, zext_ref, pos0)
    _mix_tails(ptail_ref, ctail_ref, uext_ref, zext_ref, t_last)


def _ab_mix(proj, pool_hist, conv_hist, plin, pscale, cw, tm, pos0, t_valid, mix_dtype):
    b, t, n = proj.shape
    nj = t // tm
    t_last = t_valid - (nj - 1) * tm
    kern = functools.partial(_ab_mix_kernel, pos0=pos0, t_last=t_last)
    return pl.pallas_call(
        kern,
        grid=(b, nj),
        in_specs=[
            pl.BlockSpec((1, tm, n), lambda i, j: (i, j, 0)),
            pl.BlockSpec((1, POOL_HIST, D_POOL), lambda i, j: (i, 0, 0)),
            pl.BlockSpec((1, CONV_HIST, D_CONV), lambda i, j: (i, 0, 0)),
            pl.BlockSpec(plin.shape, lambda i, j: (0, 0, 0)),
            pl.BlockSpec((1, D_POOL), lambda i, j: (0, 0)),
            pl.BlockSpec((CONV_WIDTH, D_CONV), lambda i, j: (0, 0)),
        ],
        out_specs=[
            pl.BlockSpec((1, tm, D_POOL + D_CONV), lambda i, j: (i, j, 0)),
            pl.BlockSpec((1, POOL_HIST, D_POOL), lambda i, j: (i, 0, 0)),
            pl.BlockSpec((1, CONV_HIST, D_CONV), lambda i, j: (i, 0, 0)),
        ],
        out_shape=[
            jax.ShapeDtypeStruct((b, t, D_POOL + D_CONV), mix_dtype),
            jax.ShapeDtypeStruct((b, POOL_HIST, D_POOL), F32),
            jax.ShapeDtypeStruct((b, CONV_HIST, D_CONV), F32),
        ],
        scratch_shapes=[pltpu.VMEM((tm + POOL_PAD, D_POOL), F32), pltpu.VMEM((tm + CONV_PAD, D_CONV), F32)],
        compiler_params=_params("parallel", "arbitrary"),
        name="ab_mix",
    )(proj, pool_hist, conv_hist, plin, pscale, cw)


def _ab_layer_kernel(x_ref, nw_ref, win_ref, ph_ref, ch_ref, plin_ref, pscale_ref, cw_ref, wout_ref,
                     o_ref, ptail_ref, ctail_ref, proj_ref, mix_ref, uext_ref, zext_ref, *, pos0, t_last):
    _mix_history(ph_ref, ch_ref, uext_ref, zext_ref, x_ref.shape[1])
    x = x_ref[0]
    ms = jnp.mean(x * x, axis=-1, keepdims=True)
    xn = (x * lax.rsqrt(ms + RMS_EPS) * nw_ref[...]).astype(BF16)
    n_in = win_ref.shape[1]
    for c0 in range(0, n_in, AB_PROJ_CHUNK):
        proj_ref[0, :, c0:c0 + AB_PROJ_CHUNK] = jnp.dot(xn, win_ref[:, c0:c0 + AB_PROJ_CHUNK],
                                                        preferred_element_type=F32)
    _mix_body(proj_ref, plin_ref, pscale_ref, cw_ref, mix_ref, uext_ref, zext_ref, pos0)
    y = x
    for c0 in range(0, wout_ref.shape[0], AB_OUT_CHUNK):
        y = y + jnp.dot(mix_ref[0, :, c0:c0 + AB_OUT_CHUNK].astype(BF16), wout_ref[c0:c0 + AB_OUT_CHUNK, :],
                        preferred_element_type=F32)
    o_ref[0] = y
    _mix_tails(ptail_ref, ctail_ref, uext_ref, zext_ref, t_last)


def _ab_layer(x, nw, w_in, w_out, layer, pool_hist, conv_hist, plin, pscale, cw, tm, pos0, t_valid):
    b, t, d = x.shape
    n_in = w_in.shape[2]
    nj = t // tm
    t_last = t_valid - (nj - 1) * tm
    resident = pl.Buffered(1)
    return pl.pallas_call(
        functools.partial(_ab_layer_kernel, pos0=pos0, t_last=t_last),
        grid=(b, nj),
        in_specs=[
            pl.BlockSpec((1, tm, d), lambda i, j: (i, j, 0)),
            pl.BlockSpec((1, d), lambda i, j: (0, 0)),
            pl.BlockSpec((None, d, n_in), lambda i, j: (layer, 0, 0), pipeline_mode=resident),
            pl.BlockSpec((1, POOL_HIST, D_POOL), lambda i, j: (i, 0, 0)),
            pl.BlockSpec((1, CONV_HIST, D_CONV), lambda i, j: (i, 0, 0)),
            pl.BlockSpec(plin.shape, lambda i, j: (0, 0, 0)),
            pl.BlockSpec((1, D_POOL), lambda i, j: (0, 0)),
            pl.BlockSpec((CONV_WIDTH, D_CONV), lambda i, j: (0, 0)),
            pl.BlockSpec((None, D_POOL + D_CONV, d), lambda i, j: (layer, 0, 0), pipeline_mode=resident),
        ],
        out_specs=[
            pl.BlockSpec((1, tm, d), lambda i, j: (i, j, 0)),
            pl.BlockSpec((1, POOL_HIST, D_POOL), lambda i, j: (i, 0, 0)),
            pl.BlockSpec((1, CONV_HIST, D_CONV), lambda i, j: (i, 0, 0)),
        ],
        out_shape=[
            jax.ShapeDtypeStruct((b, t, d), F32),
            jax.ShapeDtypeStruct((b, POOL_HIST, D_POOL), F32),
            jax.ShapeDtypeStruct((b, CONV_HIST, D_CONV), F32),
        ],
        scratch_shapes=[
            pltpu.VMEM((1, tm, n_in), F32),
            pltpu.VMEM((1, tm, D_POOL + D_CONV), BF16 if tm % (2 * SUBLANES) == 0 else F32),
            pltpu.VMEM((tm + POOL_PAD, D_POOL), F32),
            pltpu.VMEM((tm + CONV_PAD, D_CONV), F32),
        ],
        compiler_params=pltpu.CompilerParams(dimension_semantics=("parallel", "arbitrary"),
                                             vmem_limit_bytes=AB_LAYER_VMEM_LIMIT),
        name="ab_layer",
    )(x, nw, w_in, pool_hist, conv_hist, plin, pscale, cw, w_out)


def _rows(start, size, stride):
    if stride == 1:
        return pl.ds(start, size)
    return pl.ds(start, size, stride=stride)


def _attn_prompt_kernel(q_ref, k_ref, v_ref, g_ref, cache_ref, o_ref, roll_ref,
                        kd_ref, vd_ref, bias_ref, s_ref, ring_ref, sem_ref, *state_refs, n_new):
    t = q_ref.shape[0]
    n_cb, n_buf = cache_ref.shape[1], cache_ref.shape[2]
    step = pl.program_id(0) * pl.num_programs(1) + pl.program_id(1)
    slab, slab_half = step // 2, step % 2
    lyr, cb = slab // n_cb, slab % n_cb
    dst0 = slab_half * (n_buf // 2 - n_new)
    n_ring = ring_ref.shape[0]

    def roll_read(c):
        return pltpu.make_async_copy(cache_ref.at[lyr, cb, pl.ds(dst0 + c * ROLL_CHUNK + n_new, ROLL_CHUNK)],
                                     ring_ref.at[c % n_ring], sem_ref.at[0, c % n_ring])

    def roll_write(c):
        return pltpu.make_async_copy(ring_ref.at[c % n_ring],
                                     roll_ref.at[lyr, cb, pl.ds(dst0 + c * ROLL_CHUNK, ROLL_CHUNK)],
                                     sem_ref.at[1, c % n_ring])

    def roll_tail():
        return pltpu.make_async_copy(ring_ref.at[0, pl.ds(0, n_new)],
                                     roll_ref.at[lyr, cb, pl.ds(n_buf - n_new, n_new)], sem_ref.at[1, 0])

    for c in range(n_ring):
        roll_read(c).start()
    n_groups = len(DILATED_GROUPS)
    kf_ref, vf_ref = state_refs[0:2]
    acc_refs = (None,) + tuple(state_refs[2:2 + n_groups - 1])
    stat_refs = (None,) + tuple(state_refs[2 + n_groups - 1:])
    nt_dims = (((1,), (1,)), ((), ()))
    half = HEAD_DIM // 2
    lane = lax.broadcasted_iota(jnp.int32, (Q_BLOCK, HEAD_DIM), 1)

    @pl.when((pl.program_id(0) == 0) & (pl.program_id(1) == 0))
    def _():
        delta = (lax.broadcasted_iota(jnp.int32, (Q_BLOCK, 2 * Q_BLOCK), 0)
                 - lax.broadcasted_iota(jnp.int32, (Q_BLOCK, 2 * Q_BLOCK), 1))
        for sel in range(2):
            dist = delta + sel * Q_BLOCK
            bias_ref[sel] = jnp.where((dist >= 0) & (dist <= Q_BLOCK), 0.0, NEG_INF)
        vd_ref[:, :, HEAD_DIM:] = jnp.ones((n_groups, t, HEAD_DIM), BF16)

    def deinterleave(gi, d, d_src, ksrc_ref, vsrc_ref, keep_f32):
        stream_len = t // d
        ratio = d // d_src

        def body(c, carry):
            dst = pl.multiple_of(c * DEINT_CHUNK, DEINT_CHUNK)
            r = dst // stream_len
            m0 = dst % stream_len
            src = _rows((r % d_src) * (t // d_src) + r // d_src + ratio * m0, DEINT_CHUNK, ratio)
            k = ksrc_ref[src, :]
            v = vsrc_ref[src, :]
            kd_ref[gi, pl.ds(dst, DEINT_CHUNK), :] = k.astype(BF16)
            vd_ref[gi, pl.ds(dst, DEINT_CHUNK), 0:HEAD_DIM] = v.astype(BF16)
            if keep_f32:
                kf_ref[pl.ds(dst, DEINT_CHUNK), :] = k
                vf_ref[pl.ds(dst, DEINT_CHUNK), :] = v
            return carry

        lax.fori_loop(0, t // DEINT_CHUNK, body, 0, unroll=2)

    def merged_output(rows, m0, acc0, l0):
        ms, ls, accs = [m0], [l0], [acc0]
        for gi in range(1, n_groups):
            st = stat_refs[gi][rows, :]
            sw = pltpu.roll(st, half, axis=1)
            ms.append(jnp.where(lane < half, st, sw))
            ls.append(jnp.where(lane < half, sw, st))
            accs.append(acc_refs[gi][rows, :])
        m_all = functools.reduce(jnp.maximum, ms)
        num = None
        den = None
        for m_g, l_g, acc_g in zip(ms, ls, accs):
            w = jnp.exp2(m_g - m_all)
            num = w * acc_g if num is None else num + w * acc_g
            den = w * l_g if den is None else den + w * l_g
        g = g_ref[rows, :]
        return ((num * g) / (den * (1.0 + jnp.exp(-g)))).astype(o_ref.dtype)

    def run_group(gi, d, after_scores, after_values):
        stream_len = t // d
        nb = stream_len // Q_BLOCK

        def block_rows(idx):
            r = idx // nb
            mb = idx % nb
            kb = jnp.maximum(mb - 1, 0)
            q_start = r + d * Q_BLOCK * mb
            q_rows = _rows(pl.multiple_of(q_start, Q_BLOCK) if d == 1 else q_start, Q_BLOCK, d)
            k_rows = pl.ds(pl.multiple_of(r * stream_len + kb * Q_BLOCK, Q_BLOCK), 2 * Q_BLOCK)
            return q_rows, k_rows, mb - kb

        def scores(it, carry):
            s = []
            for u in range(SCORE_BLOCKS_PER_TRIP):
                q_rows, k_rows, sel = block_rows(it * SCORE_BLOCKS_PER_TRIP + u)
                q = (q_ref[q_rows, :] * (ATTN_SCALE * LOG2_E)).astype(BF16)
                s.append(lax.dot_general(q, kd_ref[gi, k_rows, :], nt_dims, preferred_element_type=F32)
                         + bias_ref[sel])
            for u in range(SCORE_BLOCKS_PER_TRIP):
                s_ref[it * SCORE_BLOCKS_PER_TRIP + u] = s[u]
            return carry

        n_val = MERGE_BLOCKS_PER_TRIP if gi == 0 else VALUE_BLOCKS_PER_TRIP

        def values(it, carry):
            rows, s, v = [], [], []
            for u in range(n_val):
                q_rows, k_rows, _ = block_rows(it * n_val + u)
                rows.append(q_rows)
                s.append(s_ref[it * n_val + u])
                v.append(vd_ref[gi, k_rows, :])
            outs = []
            for u in range(n_val):
                m_blk = jnp.max(s[u], axis=-1, keepdims=True)
                p = jnp.exp2(s[u] - m_blk).astype(BF16)
                pv = jnp.dot(p, v[u], preferred_element_type=F32)
                acc, l_blk = pv[:, 0:HEAD_DIM], pv[:, HEAD_DIM:]
                if gi == 0:
                    outs.append((merged_output(rows[u], jnp.broadcast_to(m_blk, acc.shape), acc, l_blk),))
                else:
                    outs.append((acc, jnp.where(lane < half, m_blk, l_blk)))
            for u in range(n_val):
                if gi == 0:
                    o_ref[rows[u], :] = outs[u][0]
                else:
                    acc_refs[gi][rows[u], :] = outs[u][0]
                    stat_refs[gi][rows[u], :] = outs[u][1]
            return carry

        lax.fori_loop(0, d * nb // SCORE_BLOCKS_PER_TRIP, scores, 0)
        after_scores()
        lax.fori_loop(0, d * nb // n_val, values, 0)
        after_values()

    ksrc_ref, vsrc_ref, d_src = k_ref, v_ref, 1
    for gi, (_, d) in enumerate(DILATED_GROUPS):
        keep_f32 = 0 < gi < n_groups - 1
        deinterleave(gi, d, d_src, ksrc_ref, vsrc_ref, keep_f32)
        if keep_f32:
            ksrc_ref, vsrc_ref, d_src = kf_ref, vf_ref, d
    def forward_first():
        for c in range(n_ring):
            roll_read(c).wait()
            roll_write(c).start()

    def refill():
        for c in range(n_ring):
            roll_write(c).wait()
            roll_read(n_ring + c).start()

    def forward_second():
        for c in range(n_ring, 2 * n_ring):
            roll_read(c).wait()
            roll_write(c).start()

    def drain():
        for c in range(n_ring, 2 * n_ring):
            roll_write(c).wait()

        @pl.when(slab_half == 1)
        def _():
            roll_tail().start()
            roll_tail().wait()

    def nothing():
        pass

    hooks = [(forward_first, refill), (nothing, forward_second), (nothing, drain)]
    for (after_scores, after_values), gi in zip(hooks, reversed(range(n_groups))):
        run_group(gi, DILATED_GROUPS[gi][1], after_scores, after_values)


def _attn_prompt(proj, cache, n_new, out_dtype):
    b, t, _ = proj.shape
    n_groups = len(DILATED_GROUPS)
    n_layers, n_cb, n_buf = cache.shape[:3]
    assert b * N_HEADS == 2 * n_layers * n_cb and n_buf // 2 == 2 * ROLL_RING * ROLL_CHUNK
    dils = [dil for _, dil in DILATED_GROUPS]
    assert n_groups == 3 and dils[0] == 1 and all(hi % lo == 0 for lo, hi in zip(dils, dils[1:]))
    for window, dil in DILATED_GROUPS:
        assert window // dil == Q_BLOCK and (t // dil) % DEINT_CHUNK == 0
        assert (t // Q_BLOCK) % SCORE_BLOCKS_PER_TRIP == 0 and (t // Q_BLOCK) % VALUE_BLOCKS_PER_TRIP == 0
    blk = (None, t, HEAD_DIM)
    return pl.pallas_call(
        functools.partial(_attn_prompt_kernel, n_new=n_new),
        grid=(b, N_HEADS),
        in_specs=[
            pl.BlockSpec(blk, lambda i, h: (i, 0, h)),
            pl.BlockSpec(blk, lambda i, h: (i, 0, N_HEADS + h)),
            pl.BlockSpec(blk, lambda i, h: (i, 0, 2 * N_HEADS + h)),
            pl.BlockSpec(blk, lambda i, h: (i, 0, 3 * N_HEADS + h)),
            pl.BlockSpec(memory_space=pl.ANY),
        ],
        out_specs=[pl.BlockSpec(blk, lambda i, h: (i, 0, h)), pl.BlockSpec(memory_space=pl.ANY)],
        out_shape=[jax.ShapeDtypeStruct((b, t, D_ATTN), out_dtype),
                   jax.ShapeDtypeStruct(cache.shape, cache.dtype)],
        scratch_shapes=(
            [pltpu.VMEM((n_groups, t, HEAD_DIM), BF16), pltpu.VMEM((n_groups, t, 2 * HEAD_DIM), BF16)]
            + [pltpu.VMEM((2, Q_BLOCK, 2 * Q_BLOCK), F32)]
            + [pltpu.VMEM((t // Q_BLOCK, Q_BLOCK, 2 * Q_BLOCK), F32)]
            + [pltpu.VMEM((ROLL_RING, ROLL_CHUNK) + cache.shape[3:], cache.dtype)]
            + [pltpu.SemaphoreType.DMA((2, ROLL_RING))]
            + [pltpu.VMEM((t, HEAD_DIM), F32)] * (2 + 2 * (n_groups - 1))),
        compiler_params=pltpu.CompilerParams(dimension_semantics=("arbitrary", "arbitrary"),
                                             vmem_limit_bytes=ATTN_VMEM_LIMIT),
        name="attn_prompt",
    )(proj, proj, proj, proj, cache)


def _group_count(dist):
    cnt = jnp.zeros(dist.shape, F32)
    for window, dil in DILATED_GROUPS:
        hit = (dist >= 0) & (dist <= window) & ((dist & (dil - 1)) == 0)
        cnt = cnt + hit.astype(F32)
    return cnt


def _attn_sample_kernel(q_ref, g_ref, kn_ref, vn_ref, kc_ref, vc_ref, o_ref, m_ref, l_ref, acc_ref):
    j = pl.program_id(1)
    nj = pl.num_programs(1)
    tk = kc_ref.shape[0]
    n_buf = tk * nj
    n_rows = q_ref.shape[0]
    nt_dims = (((1,), (1,)), ((), ()))
    head_bits = N_HEADS.bit_length() - 1

    @pl.when(j == 0)
    def _():
        m_ref[...] = jnp.full(m_ref.shape, NEG_INF, F32)
        l_ref[...] = jnp.zeros(l_ref.shape, F32)
        acc_ref[...] = jnp.zeros(acc_ref.shape, F32)

    q = (q_ref[...] * ATTN_SCALE).astype(BF16)

    def accumulate(k2, v2, key_pos0):
        n_cols = k2.shape[0]
        row = lax.broadcasted_iota(jnp.int32, (n_rows, n_cols), 0)
        col = lax.broadcasted_iota(jnp.int32, (n_rows, n_cols), 1)
        same_head = (row & (N_HEADS - 1)) == (col & (N_HEADS - 1))
        dist = (n_buf + (row >> head_bits)) - (key_pos0 + (col >> head_bits))
        cnt = jnp.where(same_head, _group_count(dist), 0.0)
        s = lax.dot_general(q, k2.astype(BF16), nt_dims, preferred_element_type=F32)
        s = jnp.where(cnt > 0, s, NEG_INF)
        m_prev = m_ref[...]
        m_new = jnp.maximum(m_prev, jnp.max(s, axis=-1, keepdims=True))
        alpha = jnp.exp(m_prev - m_new)
        p = cnt * jnp.exp(s - m_new)
        l_ref[...] = alpha * l_ref[...] + jnp.sum(p, axis=-1, keepdims=True)
        acc_ref[...] = alpha * acc_ref[...] + jnp.dot(p.astype(BF16), v2.astype(BF16),
                                                      preferred_element_type=F32)
        m_ref[...] = m_new

    accumulate(kc_ref[...].reshape(tk * N_HEADS, HEAD_DIM), vc_ref[...].reshape(tk * N_HEADS, HEAD_DIM),
               j * tk)

    @pl.when(j == nj - 1)
    def _():
        accumulate(kn_ref[...], vn_ref[...], n_buf)
        o_ref[...] = (acc_ref[...] / l_ref[...]) * _silu(g_ref[...])


def _attn_sample(q2, g2, kn2, vn2, cache_k, cache_v, layer, tk):
    b, n_rows, _ = q2.shape
    n_buf = cache_k.shape[2]
    rblk = (None, n_rows, HEAD_DIM)
    cblk = (None, None, tk, N_HEADS, HEAD_DIM)
    return pl.pallas_call(
        _attn_sample_kernel,
        grid=(b, n_buf // tk),
        in_specs=[pl.BlockSpec(rblk, lambda i, j: (i, 0, 0))] * 4
        + [pl.BlockSpec(cblk, lambda i, j: (layer, i, j, 0, 0))] * 2,
        out_specs=pl.BlockSpec(rblk, lambda i, j: (i, 0, 0)),
        out_shape=jax.ShapeDtypeStruct((b, n_rows, HEAD_DIM), F32),
        scratch_shapes=[pltpu.VMEM((n_rows, 1), F32), pltpu.VMEM((n_rows, 1), F32),
                        pltpu.VMEM((n_rows, HEAD_DIM), F32)],
        compiler_params=_params("parallel", "arbitrary"),
        name="attn_sample",
    )(q2, g2, kn2, vn2, cache_k, cache_v)


def _write_new_rows_kernel(rolled_ref, new_ref, out_ref, sem):
    del rolled_ref
    n_layers, n_batch, n_new = new_ref.shape[:3]
    n_buf = out_ref.shape[2]
    copies = [pltpu.make_async_copy(new_ref.at[l, b], out_ref.at[l, b, pl.ds(n_buf - n_new, n_new)],
                                    sem.at[l * n_batch + b])
              for l in range(n_layers) for b in range(n_batch)]
    for copy in copies:
        copy.start()
    for copy in copies:
        copy.wait()


def _write_new_rows(rolled, new):
    n_layers, n_batch = new.shape[:2]
    return pl.pallas_call(
        _write_new_rows_kernel,
        in_specs=[pl.BlockSpec(memory_space=pl.ANY), pl.BlockSpec(memory_space=pltpu.VMEM)],
        out_specs=pl.BlockSpec(memory_space=pl.ANY),
        out_shape=jax.ShapeDtypeStruct(rolled.shape, rolled.dtype),
        scratch_shapes=[pltpu.SemaphoreType.DMA((n_layers * n_batch,))],
        input_output_aliases={0: 0},
        name="write_new_rows",
    )(rolled, new)


def _head_major_kernel(*refs, n_layers):
    ko_ref, vo_ref = refs[2 * n_layers:]
    tq = refs[0].shape[0]
    for layer in range(n_layers):
        @pl.when(pl.program_id(0) == layer)
        def _(layer=layer):
            for src, dst in ((refs[2 * layer], ko_ref), (refs[2 * layer + 1], vo_ref)):
                for h in range(N_HEADS):
                    dst[pl.ds(h, tq, stride=N_HEADS), :] = src[:, h * HEAD_DIM:(h + 1) * HEAD_DIM]


def _head_major_kv(projs, n_keep, tq):
    n_layers = len(projs)
    b, t, _ = projs[0].shape
    first = (t - n_keep) // tq
    nj = n_keep // tq
    in_specs = []
    for layer in range(n_layers):
        for col in (1, 2):
            in_specs.append(pl.BlockSpec(
                (None, tq, D_ATTN),
                lambda l, i, j, layer=layer, col=col: (jnp.where(l == layer, i, 0),
                                                       jnp.where(l == layer, first + j, 0), col)))
    out_spec = pl.BlockSpec((None, None, tq * N_HEADS, HEAD_DIM), lambda l, i, j: (l, i, j, 0))
    out = jax.ShapeDtypeStruct((n_layers, b, n_keep * N_HEADS, HEAD_DIM), F32)
    k_out, v_out = pl.pallas_call(
        functools.partial(_head_major_kernel, n_layers=n_layers),
        grid=(n_layers, b, nj),
        in_specs=in_specs,
        out_specs=[out_spec, out_spec],
        out_shape=[out, out],
        compiler_params=_params("arbitrary", "arbitrary", "arbitrary"),
        name="head_major_kv",
    )(*[p for p in projs for _ in (1, 2)])
    shape = (n_layers, b, n_keep, N_HEADS, HEAD_DIM)
    return k_out.reshape(shape), v_out.reshape(shape)


def _cast_kernel(x_ref, o_ref):
    o_ref[...] = x_ref[...].astype(o_ref.dtype)


def _cast_bf16(w, rows):
    n_layers, r, c = w.shape
    return pl.pallas_call(
        _cast_kernel,
        grid=(n_layers, r // rows),
        in_specs=[pl.BlockSpec((1, rows, c), lambda l, i: (l, i, 0))],
        out_specs=pl.BlockSpec((1, rows, c), lambda l, i: (l, i, 0)),
        out_shape=jax.ShapeDtypeStruct(w.shape, BF16),
        compiler_params=_params("parallel", "parallel"),
        name="cast_bf16",
    )(w)


def kernel(x_prompt, x_sample, state_pool, state_conv, cache_k, cache_v, norm_w, final_norm_w,
           w_in_ab, pool_lin, pool_scale, conv_w, w_out_ab, w_in_c, w_out_c):
    bp, t, d = x_prompt.shape
    bs, ts, _ = x_sample.shape
    ts_pad = -(-ts // SUBLANES) * SUBLANES
    n_buf = cache_k.shape[2]
    n_keep_p = min(MAX_WINDOW, t)

    hp = x_prompt.reshape(bp * t, d)
    hs = jnp.pad(x_sample, ((0, 0), (0, ts_pad - ts), (0, 0))).reshape(bs * ts_pad, d)
    zero_pool = jnp.zeros((bp, POOL_HIST, D_POOL), F32)
    zero_conv = jnp.zeros((bp, CONV_HIST, D_CONV), F32)

    tm_p, tn = 1024, 1024
    tm_s = bs * ts_pad
    cast_rows = 256

    w_in_ab = _cast_bf16(w_in_ab, cast_rows)
    w_out_ab = _cast_bf16(w_out_ab, cast_rows)
    w_in_c = _cast_bf16(w_in_c, cast_rows)
    w_out_c = _cast_bf16(w_out_c, cast_rows)
    pool_lin = _cast_bf16(pool_lin.reshape(pool_lin.shape[0], -1, D_POOL_GROUP), cast_rows).reshape(pool_lin.shape)

    def head_rows(a):
        return a.reshape(a.shape[0], a.shape[1] * N_HEADS, HEAD_DIM)

    assert DEPTH == 4
    assert min(MAX_WINDOW, n_buf + ts) == n_buf
    rolled = []
    fw = final_norm_w.reshape(1, d)
    pool_p, pool_s, conv_p, conv_s = [], [], [], []
    projs_c, k_new, v_new = [], [], []
    for l in range(DEPTH):
        i = l // 2
        nw = norm_w[l].reshape(1, d)
        if l % 2 == 0:
            pscale = pool_scale[i].reshape(1, D_POOL)
            hp, pp, cp = _ab_layer(hp.reshape(bp, t, d), nw, w_in_ab, w_out_ab, i, zero_pool, zero_conv,
                                   pool_lin[i], pscale, conv_w[i], AB_ROW_TILE, 0, t)
            hp = hp.reshape(bp * t, d)
            proj_s = _norm_matmul(hs, nw, w_in_ab, i, tm_s, tn).reshape(bs, ts_pad, -1)
            mix_s, ps, cs = _ab_mix(proj_s, state_pool[i], state_conv[i], pool_lin[i], pscale, conv_w[i],
                                    ts_pad, PAST_LEN, ts, F32)
            hs = _matmul_residual(mix_s.reshape(bs * ts_pad, -1), w_out_ab, i, hs, tm_s, tn)
            pool_p.append(pp)
            pool_s.append(ps)
            conv_p.append(cp)
            conv_s.append(cs)
        else:
            proj_p = _norm_matmul(hp, nw, w_in_c, i, tm_p, tn).reshape(bp, t, -1)
            proj_s = _norm_matmul(hs, nw, w_in_c, i, tm_s, tn).reshape(bs, ts_pad, -1)[:, :ts]
            og_p, rolled_c = _attn_prompt(proj_p, (cache_k, cache_v)[i], ts, BF16)
            rolled.append(rolled_c)
            q_s, kn_s, vn_s, g_s = (head_rows(proj_s[..., c * D_ATTN:(c + 1) * D_ATTN]) for c in range(4))
            og_s = _attn_sample(q_s, g_s, kn_s, vn_s, cache_k, cache_v, i, SAMPLE_KEY_TILE)
            og_s = jnp.pad(og_s.reshape(bs, ts, D_ATTN), ((0, 0), (0, ts_pad - ts), (0, 0)))
            if l == DEPTH - 1:
                hp = _matmul_residual_norm(og_p.reshape(bp * t, -1), w_out_c, i, hp, fw, 512)
                hs = _matmul_residual_norm(og_s.reshape(bs * ts_pad, -1), w_out_c, i, hs, fw, tm_s)
            else:
                hp = _matmul_residual(og_p.reshape(bp * t, -1), w_out_c, i, hp, tm_p, tn)
                hs = _matmul_residual(og_s.reshape(bs * ts_pad, -1), w_out_c, i, hs, tm_s, tn)
            projs_c.append(proj_p)
            k_new.append(kn_s.reshape(bs, ts, N_HEADS, HEAD_DIM))
            v_new.append(vn_s.reshape(bs, ts, N_HEADS, HEAD_DIM))

    k_s = _write_new_rows(rolled[0], jnp.stack(k_new))
    v_s = _write_new_rows(rolled[1], jnp.stack(v_new))
    k_p, v_p = _head_major_kv(projs_c, n_keep_p, 256)

    y_prompt = hp.reshape(bp, t, d)
    y_sample = hs.reshape(bs, ts_pad, d)[:, :ts]
    return (y_prompt, y_sample, jnp.stack(pool_p), jnp.stack(pool_s), jnp.stack(conv_p), jnp.stack(conv_s),
            k_p, k_s, v_p, v_s)
```

```python
import functools

import jax
import jax.numpy as jnp
from jax import lax
from jax.experimental import pallas as pl
from jax.experimental.pallas import tpu as pltpu

F32 = jnp.float32
BF16 = jnp.bfloat16

D_MODEL = 2048
DEPTH = 4
PAST_LEN = 16384
D_POOL = D_MODEL // 2
POOL_WINDOWS = (2, 4, 8, 16)
D_POOL_GROUP = D_POOL // len(POOL_WINDOWS)
POOL_HIST = max(POOL_WINDOWS) - 1
D_CONV = D_MODEL // 2
CONV_WIDTH = 3
CONV_HIST = CONV_WIDTH - 1
HEAD_DIM = 128
N_HEADS = D_MODEL // HEAD_DIM
D_ATTN = N_HEADS * HEAD_DIM
DILATED_GROUPS = ((128, 1), (512, 4), (2048, 16))
MAX_WINDOW = max(w for w, _ in DILATED_GROUPS)
Q_BLOCK = 128
ATTN_SCALE = HEAD_DIM ** -0.5
LOG2_E = 1.4426950408889634
RMS_EPS = 1e-6
NEG_INF = -1e30

SUBLANES = 8
POOL_PAD = 16
CONV_PAD = 8
DEINT_CHUNK = 2 * Q_BLOCK
SAMPLE_DENSE_ROWS = 512
SAMPLE_PERIODS = 32
SAMPLE_RESIDUES = 4
ROLL_CHUNK = 128
ROLL_RING = 4
ROLL_DMA_PRIORITY = 1
SCORE_BLOCKS_PER_TRIP = 16
VALUE_BLOCKS_PER_TRIP = 8
MERGE_BLOCKS_PER_TRIP = 8
VMEM_LIMIT = 48 * 1024 * 1024
AB_LAYER_VMEM_LIMIT = 56 * 1024 * 1024
ATTN_VMEM_LIMIT = 54 * 1024 * 1024
AB_ROW_TILE = 256
AB_OUT_CHUNK = 512
AB_PROJ_CHUNK = 1024


def _silu(x):
    return x * jax.nn.sigmoid(x)


def _params(*semantics):
    return pltpu.CompilerParams(dimension_semantics=semantics, vmem_limit_bytes=VMEM_LIMIT)


def _norm_matmul_kernel(x_ref, nw_ref, w_ref, o_ref, xn_ref):
    @pl.when(pl.program_id(1) == 0)
    def _():
        x = x_ref[...]
        ms = jnp.mean(x * x, axis=-1, keepdims=True)
        xn_ref[...] = (x * lax.rsqrt(ms + RMS_EPS) * nw_ref[...]).astype(BF16)

    o_ref[...] = jnp.dot(xn_ref[...], w_ref[...], preferred_element_type=F32)


def _norm_matmul(x, nw, w, layer, tm, tn):
    m, d = x.shape
    n = w.shape[2]
    return pl.pallas_call(
        _norm_matmul_kernel,
        grid=(m // tm, n // tn),
        in_specs=[
            pl.BlockSpec((tm, d), lambda i, j: (i, 0)),
            pl.BlockSpec((1, d), lambda i, j: (0, 0)),
            pl.BlockSpec((None, d, tn), lambda i, j: (layer, 0, j)),
        ],
        out_specs=pl.BlockSpec((tm, tn), lambda i, j: (i, j)),
        out_shape=jax.ShapeDtypeStruct((m, n), F32),
        scratch_shapes=[pltpu.VMEM((tm, d), BF16)],
        compiler_params=_params("parallel", "arbitrary"),
        name="norm_matmul",
    )(x, nw, w)


def _matmul_residual_kernel(a_ref, w_ref, h_ref, o_ref):
    o_ref[...] = h_ref[...] + jnp.dot(a_ref[...].astype(BF16), w_ref[...], preferred_element_type=F32)


def _matmul_residual(a, w, layer, h, tm, tn):
    m, k = a.shape
    n = w.shape[2]
    return pl.pallas_call(
        _matmul_residual_kernel,
        grid=(m // tm, n // tn),
        in_specs=[
            pl.BlockSpec((tm, k), lambda i, j: (i, 0)),
            pl.BlockSpec((None, k, tn), lambda i, j: (layer, 0, j)),
            pl.BlockSpec((tm, tn), lambda i, j: (i, j)),
        ],
        out_specs=pl.BlockSpec((tm, tn), lambda i, j: (i, j)),
        out_shape=jax.ShapeDtypeStruct((m, n), F32),
        compiler_params=_params("parallel", "arbitrary"),
        name="matmul_residual",
    )(a, w, h)


def _matmul_residual_norm_kernel(a_ref, w_ref, h_ref, nw_ref, o_ref):
    y = h_ref[...] + jnp.dot(a_ref[...].astype(BF16), w_ref[...], preferred_element_type=F32)
    ms = jnp.mean(y * y, axis=-1, keepdims=True)
    o_ref[...] = y * lax.rsqrt(ms + RMS_EPS) * nw_ref[...]


def _matmul_residual_norm(a, w, layer, h, nw, tm):
    m, k = a.shape
    n = w.shape[2]
    return pl.pallas_call(
        _matmul_residual_norm_kernel,
        grid=(m // tm,),
        in_specs=[
            pl.BlockSpec((tm, k), lambda i: (i, 0)),
            pl.BlockSpec((None, k, n), lambda i: (layer, 0, 0)),
            pl.BlockSpec((tm, n), lambda i: (i, 0)),
            pl.BlockSpec((1, n), lambda i: (0, 0)),
        ],
        out_specs=pl.BlockSpec((tm, n), lambda i: (i, 0)),
        out_shape=jax.ShapeDtypeStruct((m, n), F32),
        compiler_params=_params("parallel"),
        name="matmul_residual_norm",
    )(a, w, h, nw)


def _mix_history(ph_ref, ch_ref, uext_ref, zext_ref, tm):
    j = pl.program_id(1)

    @pl.when(j == 0)
    def _():
        uext_ref[POOL_PAD - POOL_HIST:POOL_PAD, :] = ph_ref[0]
        zext_ref[CONV_PAD - CONV_HIST:CONV_PAD, :] = ch_ref[0]

    @pl.when(j > 0)
    def _():
        uext_ref[0:POOL_PAD, :] = uext_ref[tm:tm + POOL_PAD, :]
        zext_ref[0:CONV_PAD, :] = zext_ref[tm:tm + CONV_PAD, :]


def _mix_tails(ptail_ref, ctail_ref, uext_ref, zext_ref, t_last):
    @pl.when(pl.program_id(1) == pl.num_programs(1) - 1)
    def _():
        ptail_ref[0] = uext_ref[POOL_PAD + t_last - POOL_HIST:POOL_PAD + t_last, :]
        ctail_ref[0] = zext_ref[CONV_PAD + t_last - CONV_HIST:CONV_PAD + t_last, :]


def _mix_body(proj_ref, plin_ref, pscale_ref, cw_ref, mix_ref, uext_ref, zext_ref, pos0):
    j = pl.program_id(1)
    tm = proj_ref.shape[1]
    gw = D_POOL_GROUP

    uext_ref[POOL_PAD:POOL_PAD + tm, :] = proj_ref[0, :, 0:D_POOL]
    c_off = 2 * D_POOL + D_CONV
    zext_ref[CONV_PAD:CONV_PAD + tm, :] = (
        proj_ref[0, :, c_off:c_off + D_CONV] * proj_ref[0, :, c_off + D_CONV:c_off + 2 * D_CONV])

    pos = (pos0 + j * tm + lax.broadcasted_iota(jnp.int32, (tm, 1), 0)).astype(F32)

    for g, k in enumerate(POOL_WINDOWS):
        c0 = g * gw
        u_g = uext_ref[POOL_PAD:POOL_PAD + tm, c0:c0 + gw]
        s = u_g
        for i in range(1, k):
            s = s + uext_ref[POOL_PAD - i:POOL_PAD - i + tm, c0:c0 + gw]
        cnt = jnp.minimum(float(k), pos + 1.0)
        pooled = s / cnt - u_g
        a = jnp.dot(pooled.astype(BF16), plin_ref[g], preferred_element_type=F32)
        a = a * pscale_ref[:, c0:c0 + gw]
        gate = proj_ref[0, :, D_POOL + c0:D_POOL + c0 + gw]
        mix_ref[0, :, c0:c0 + gw] = (a * _silu(gate)).astype(mix_ref.dtype)

    for c in range(D_CONV // gw):
        c0 = c * gw
        conv = zext_ref[CONV_PAD - 2:CONV_PAD - 2 + tm, c0:c0 + gw] * cw_ref[0:1, c0:c0 + gw]
        conv = conv + zext_ref[CONV_PAD - 1:CONV_PAD - 1 + tm, c0:c0 + gw] * cw_ref[1:2, c0:c0 + gw]
        conv = conv + zext_ref[CONV_PAD:CONV_PAD + tm, c0:c0 + gw] * cw_ref[2:3, c0:c0 + gw]
        b_gate = proj_ref[0, :, 2 * D_POOL + c0:2 * D_POOL + c0 + gw]
        gate = proj_ref[0, :, 2 * D_POOL + 3 * D_CONV + c0:2 * D_POOL + 3 * D_CONV + c0 + gw]
        mix_ref[0, :, D_POOL + c0:D_POOL + c0 + gw] = (b_gate * conv * _silu(gate)).astype(mix_ref.dtype)


def _ab_mix_kernel(proj_ref, ph_ref, ch_ref, plin_ref, pscale_ref, cw_ref,
                   mix_ref, ptail_ref, ctail_ref, uext_ref, zext_ref, *, pos0, t_last):
    _mix_history(ph_ref, ch_ref, uext_ref, zext_ref, proj_ref.shape[1])
    _mix_body(proj_ref, plin_ref, pscale_ref, cw_ref, mix_ref, uext_ref, zext_ref, pos0)
    _mix_tails(ptail_ref, ctail_ref, uext_ref, zext_ref, t_last)


def _ab_mix(proj, pool_hist, conv_hist, plin, pscale, cw, tm, pos0, t_valid, mix_dtype):
    b, t, n = proj.shape
    nj = t // tm
    t_last = t_valid - (nj - 1) * tm
    kern = functools.partial(_ab_mix_kernel, pos0=pos0, t_last=t_last)
    return pl.pallas_call(
        kern,
        grid=(b, nj),
        in_specs=[
            pl.BlockSpec((1, tm, n), lambda i, j: (i, j, 0)),
            pl.BlockSpec((1, POOL_HIST, D_POOL), lambda i, j: (i, 0, 0)),
            pl.BlockSpec((1, CONV_HIST, D_CONV), lambda i, j: (i, 0, 0)),
            pl.BlockSpec(plin.shape, lambda i, j: (0, 0, 0)),
            pl.BlockSpec((1, D_POOL), lambda i, j: (0, 0)),
            pl.BlockSpec((CONV_WIDTH, D_CONV), lambda i, j: (0, 0)),
        ],
        out_specs=[
            pl.BlockSpec((1, tm, D_POOL + D_CONV), lambda i, j: (i, j, 0)),
            pl.BlockSpec((1, POOL_HIST, D_POOL), lambda i, j: (i, 0, 0)),
            pl.BlockSpec((1, CONV_HIST, D_CONV), lambda i, j: (i, 0, 0)),
        ],
        out_shape=[
            jax.ShapeDtypeStruct((b, t, D_POOL + D_CONV), mix_dtype),
            jax.ShapeDtypeStruct((b, POOL_HIST, D_POOL), F32),
            jax.ShapeDtypeStruct((b, CONV_HIST, D_CONV), F32),
        ],
        scratch_shapes=[pltpu.VMEM((tm + POOL_PAD, D_POOL), F32), pltpu.VMEM((tm + CONV_PAD, D_CONV), F32)],
        compiler_params=_params("parallel", "arbitrary"),
        name="ab_mix",
    )(proj, pool_hist, conv_hist, plin, pscale, cw)


def _ab_layer_kernel(x_ref, nw_ref, win_ref, ph_ref, ch_ref, plin_ref, pscale_ref, cw_ref, wout_ref,
                     o_ref, ptail_ref, ctail_ref, proj_ref, mix_ref, uext_ref, zext_ref, *, pos0, t_last):
    _mix_history(ph_ref, ch_ref, uext_ref, zext_ref, x_ref.shape[1])
    x = x_ref[0]
    ms = jnp.mean(x * x, axis=-1, keepdims=True)
    xn = (x * lax.rsqrt(ms + RMS_EPS) * nw_ref[...]).astype(BF16)
    n_in = win_ref.shape[1]
    for c0 in range(0, n_in, AB_PROJ_CHUNK):
        proj_ref[0, :, c0:c0 + AB_PROJ_CHUNK] = jnp.dot(xn, win_ref[:, c0:c0 + AB_PROJ_CHUNK],
                                                        preferred_element_type=F32)
    _mix_body(proj_ref, plin_ref, pscale_ref, cw_ref, mix_ref, uext_ref, zext_ref, pos0)
    y = x
    for c0 in range(0, wout_ref.shape[0], AB_OUT_CHUNK):
        y = y + jnp.dot(mix_ref[0, :, c0:c0 + AB_OUT_CHUNK].astype(BF16), wout_ref[c0:c0 + AB_OUT_CHUNK, :],
                        preferred_element_type=F32)
    o_ref[0] = y
    _mix_tails(ptail_ref, ctail_ref, uext_ref, zext_ref, t_last)


def _ab_layer(x, nw, w_in, w_out, layer, pool_hist, conv_hist, plin, pscale, cw, tm, pos0, t_valid):
    b, t, d = x.shape
    n_in = w_in.shape[2]
    nj = t // tm
    t_last = t_valid - (nj - 1) * tm
    resident = pl.Buffered(1)
    return pl.pallas_call(
        functools.partial(_ab_layer_kernel, pos0=pos0, t_last=t_last),
        grid=(b, nj),
        in_specs=[
            pl.BlockSpec((1, tm, d), lambda i, j: (i, j, 0)),
            pl.BlockSpec((1, d), lambda i, j: (0, 0)),
            pl.BlockSpec((None, d, n_in), lambda i, j: (layer, 0, 0), pipeline_mode=resident),
            pl.BlockSpec((1, POOL_HIST, D_POOL), lambda i, j: (i, 0, 0)),
            pl.BlockSpec((1, CONV_HIST, D_CONV), lambda i, j: (i, 0, 0)),
            pl.BlockSpec(plin.shape, lambda i, j: (0, 0, 0)),
            pl.BlockSpec((1, D_POOL), lambda i, j: (0, 0)),
            pl.BlockSpec((CONV_WIDTH, D_CONV), lambda i, j: (0, 0)),
            pl.BlockSpec((None, D_POOL + D_CONV, d), lambda i, j: (layer, 0, 0), pipeline_mode=resident),
        ],
        out_specs=[
            pl.BlockSpec((1, tm, d), lambda i, j: (i, j, 0)),
            pl.BlockSpec((1, POOL_HIST, D_POOL), lambda i, j: (i, 0, 0)),
            pl.BlockSpec((1, CONV_HIST, D_CONV), lambda i, j: (i, 0, 0)),
        ],
        out_shape=[
            jax.ShapeDtypeStruct((b, t, d), F32),
            jax.ShapeDtypeStruct((b, POOL_HIST, D_POOL), F32),
            jax.ShapeDtypeStruct((b, CONV_HIST, D_CONV), F32),
        ],
        scratch_shapes=[
            pltpu.VMEM((1, tm, n_in), F32),
            pltpu.VMEM((1, tm, D_POOL + D_CONV), BF16 if tm % (2 * SUBLANES) == 0 else F32),
            pltpu.VMEM((tm + POOL_PAD, D_POOL), F32),
            pltpu.VMEM((tm + CONV_PAD, D_CONV), F32),
        ],
        compiler_params=pltpu.CompilerParams(dimension_semantics=("parallel", "arbitrary"),
                                             vmem_limit_bytes=AB_LAYER_VMEM_LIMIT),
        name="ab_layer",
    )(x, nw, w_in, pool_hist, conv_hist, plin, pscale, cw, w_out)


def _rows(start, size, stride):
    if stride == 1:
        return pl.ds(start, size)
    return pl.ds(start, size, stride=stride)


def _attn_prompt_kernel(q_ref, k_ref, v_ref, g_ref, cache_ref, o_ref, roll_ref,
                        kd_ref, vd_ref, bias_ref, s_ref, ring_ref, sem_ref, *state_refs, n_new):
    t = q_ref.shape[0]
    n_cb, n_buf = cache_ref.shape[1], cache_ref.shape[2]
    step = pl.program_id(0) * pl.num_programs(1) + pl.program_id(1)
    slab, slab_half = step // 2, step % 2
    lyr, cb = slab // n_cb, slab % n_cb
    dst0 = slab_half * (n_buf // 2 - n_new)
    n_ring = ring_ref.shape[0]

    def roll_read(c):
        return pltpu.make_async_copy(cache_ref.at[lyr, cb, pl.ds(dst0 + c * ROLL_CHUNK + n_new, ROLL_CHUNK)],
                                     ring_ref.at[c % n_ring], sem_ref.at[0, c % n_ring])

    def roll_write(c):
        return pltpu.make_async_copy(ring_ref.at[c % n_ring],
                                     roll_ref.at[lyr, cb, pl.ds(dst0 + c * ROLL_CHUNK, ROLL_CHUNK)],
                                     sem_ref.at[1, c % n_ring])

    def roll_tail():
        return pltpu.make_async_copy(ring_ref.at[0, pl.ds(0, n_new)],
                                     roll_ref.at[lyr, cb, pl.ds(n_buf - n_new, n_new)], sem_ref.at[1, 0])

    for c in range(n_ring):
        roll_read(c).start(priority=ROLL_DMA_PRIORITY)
    n_groups = len(DILATED_GROUPS)
    kf_ref, vf_ref = state_refs[0:2]
    acc_refs = (None,) + tuple(state_refs[2:2 + n_groups - 1])
    stat_refs = (None,) + tuple(state_refs[2 + n_groups - 1:])
    nt_dims = (((1,), (1,)), ((), ()))
    half = HEAD_DIM // 2
    lane = lax.broadcasted_iota(jnp.int32, (Q_BLOCK, HEAD_DIM), 1)

    @pl.when((pl.program_id(0) == 0) & (pl.program_id(1) == 0))
    def _():
        delta = (lax.broadcasted_iota(jnp.int32, (Q_BLOCK, 2 * Q_BLOCK), 0)
                 - lax.broadcasted_iota(jnp.int32, (Q_BLOCK, 2 * Q_BLOCK), 1))
        for sel in range(2):
            dist = delta + sel * Q_BLOCK
            bias_ref[sel] = jnp.where((dist >= 0) & (dist <= Q_BLOCK), 0.0, NEG_INF)
        vd_ref[:, :, HEAD_DIM:] = jnp.ones((n_groups, t, HEAD_DIM), BF16)

    def deinterleave(gi, d, d_src, ksrc_ref, vsrc_ref, keep_f32):
        stream_len = t // d
        ratio = d // d_src

        def body(c, carry):
            dst = pl.multiple_of(c * DEINT_CHUNK, DEINT_CHUNK)
            r = dst // stream_len
            m0 = dst % stream_len
            src = _rows((r % d_src) * (t // d_src) + r // d_src + ratio * m0, DEINT_CHUNK, ratio)
            k = ksrc_ref[src, :]
            v = vsrc_ref[src, :]
            kd_ref[gi, pl.ds(dst, DEINT_CHUNK), :] = k.astype(BF16)
            vd_ref[gi, pl.ds(dst, DEINT_CHUNK), 0:HEAD_DIM] = v.astype(BF16)
            if keep_f32:
                kf_ref[pl.ds(dst, DEINT_CHUNK), :] = k
                vf_ref[pl.ds(dst, DEINT_CHUNK), :] = v
            return carry

        lax.fori_loop(0, t // DEINT_CHUNK, body, 0, unroll=2)

    def merged_output(rows, m0, acc0, l0):
        ms, ls, accs = [m0], [l0], [acc0]
        for gi in range(1, n_groups):
            st = stat_refs[gi][rows, :]
            sw = pltpu.roll(st, half, axis=1)
            ms.append(jnp.where(lane < half, st, sw))
            ls.append(jnp.where(lane < half, sw, st))
            accs.append(acc_refs[gi][rows, :])
        m_all = functools.reduce(jnp.maximum, ms)
        num = None
        den = None
        for m_g, l_g, acc_g in zip(ms, ls, accs):
            w = jnp.exp2(m_g - m_all)
            num = w * acc_g if num is None else num + w * acc_g
            den = w * l_g if den is None else den + w * l_g
        g = g_ref[rows, :]
        return ((num * g) / (den * (1.0 + jnp.exp(-g)))).astype(o_ref.dtype)

    def run_group(gi, d, after_scores, after_values):
        stream_len = t // d
        nb = stream_len // Q_BLOCK

        def block_rows(idx):
            r = idx // nb
            mb = idx % nb
            kb = jnp.maximum(mb - 1, 0)
            q_start = r + d * Q_BLOCK * mb
            q_rows = _rows(pl.multiple_of(q_start, Q_BLOCK) if d == 1 else q_start, Q_BLOCK, d)
            k_rows = pl.ds(pl.multiple_of(r * stream_len + kb * Q_BLOCK, Q_BLOCK), 2 * Q_BLOCK)
            return q_rows, k_rows, mb - kb

        def scores(it, carry):
            s = []
            for u in range(SCORE_BLOCKS_PER_TRIP):
                q_rows, k_rows, sel = block_rows(it * SCORE_BLOCKS_PER_TRIP + u)
                q = (q_ref[q_rows, :] * (ATTN_SCALE * LOG2_E)).astype(BF16)
                s.append(lax.dot_general(q, kd_ref[gi, k_rows, :], nt_dims, preferred_element_type=F32)
                         + bias_ref[sel])
            for u in range(SCORE_BLOCKS_PER_TRIP):
                s_ref[it * SCORE_BLOCKS_PER_TRIP + u] = s[u]
            return carry

        n_val = MERGE_BLOCKS_PER_TRIP if gi == 0 else VALUE_BLOCKS_PER_TRIP

        def values(it, carry):
            rows, s, v = [], [], []
            for u in range(n_val):
                q_rows, k_rows, _ = block_rows(it * n_val + u)
                rows.append(q_rows)
                s.append(s_ref[it * n_val + u])
                v.append(vd_ref[gi, k_rows, :])
            outs = []
            for u in range(n_val):
                m_blk = jnp.max(s[u], axis=-1, keepdims=True)
                p = jnp.exp2(s[u] - m_blk).astype(BF16)
                pv = jnp.dot(p, v[u], preferred_element_type=F32)
                acc, l_blk = pv[:, 0:HEAD_DIM], pv[:, HEAD_DIM:]
                if gi == 0:
                    outs.append((merged_output(rows[u], jnp.broadcast_to(m_blk, acc.shape), acc, l_blk),))
                else:
                    outs.append((acc, jnp.where(lane < half, m_blk, l_blk)))
            for u in range(n_val):
                if gi == 0:
                    o_ref[rows[u], :] = outs[u][0]
                else:
                    acc_refs[gi][rows[u], :] = outs[u][0]
                    stat_refs[gi][rows[u], :] = outs[u][1]
            return carry

        lax.fori_loop(0, d * nb // SCORE_BLOCKS_PER_TRIP, scores, 0)
        after_scores()
        lax.fori_loop(0, d * nb // n_val, values, 0)
        after_values()

    ksrc_ref, vsrc_ref, d_src = k_ref, v_ref, 1
    for gi, (_, d) in enumerate(DILATED_GROUPS):
        keep_f32 = 0 < gi < n_groups - 1
        deinterleave(gi, d, d_src, ksrc_ref, vsrc_ref, keep_f32)
        if keep_f32:
            ksrc_ref, vsrc_ref, d_src = kf_ref, vf_ref, d
    def forward_first():
        for c in range(n_ring):
            roll_read(c).wait()
            roll_write(c).start(priority=ROLL_DMA_PRIORITY)

    def refill():
        for c in range(n_ring):
            roll_write(c).wait()
            roll_read(n_ring + c).start(priority=ROLL_DMA_PRIORITY)

    def forward_second():
        for c in range(n_ring, 2 * n_ring):
            roll_read(c).wait()
            roll_write(c).start(priority=ROLL_DMA_PRIORITY)

    def drain():
        for c in range(n_ring, 2 * n_ring):
            roll_write(c).wait()

        @pl.when(slab_half == 1)
        def _():
            roll_tail().start()
            roll_tail().wait()

    def nothing():
        pass

    hooks = [(forward_first, refill), (nothing, forward_second), (nothing, drain)]
    for (after_scores, after_values), gi in zip(hooks, reversed(range(n_groups))):
        run_group(gi, DILATED_GROUPS[gi][1], after_scores, after_values)


def _attn_prompt(proj, cache, n_new, out_dtype):
    b, t, _ = proj.shape
    n_groups = len(DILATED_GROUPS)
    n_layers, n_cb, n_buf = cache.shape[:3]
    assert b * N_HEADS == 2 * n_layers * n_cb and n_buf // 2 == 2 * ROLL_RING * ROLL_CHUNK
    dils = [dil for _, dil in DILATED_GROUPS]
    assert n_groups == 3 and dils[0] == 1 and all(hi % lo == 0 for lo, hi in zip(dils, dils[1:]))
    for window, dil in DILATED_GROUPS:
        assert window // dil == Q_BLOCK and (t // dil) % DEINT_CHUNK == 0
        assert (t // Q_BLOCK) % SCORE_BLOCKS_PER_TRIP == 0 and (t // Q_BLOCK) % VALUE_BLOCKS_PER_TRIP == 0
    blk = (None, t, HEAD_DIM)
    return pl.pallas_call(
        functools.partial(_attn_prompt_kernel, n_new=n_new),
        grid=(b, N_HEADS),
        in_specs=[
            pl.BlockSpec(blk, lambda i, h: (i, 0, h)),
            pl.BlockSpec(blk, lambda i, h: (i, 0, N_HEADS + h)),
            pl.BlockSpec(blk, lambda i, h: (i, 0, 2 * N_HEADS + h)),
            pl.BlockSpec(blk, lambda i, h: (i, 0, 3 * N_HEADS + h)),
            pl.BlockSpec(memory_space=pl.ANY),
        ],
        out_specs=[pl.BlockSpec(blk, lambda i, h: (i, 0, h)), pl.BlockSpec(memory_space=pl.ANY)],
        out_shape=[jax.ShapeDtypeStruct((b, t, D_ATTN), out_dtype),
                   jax.ShapeDtypeStruct(cache.shape, cache.dtype)],
        scratch_shapes=(
            [pltpu.VMEM((n_groups, t, HEAD_DIM), BF16), pltpu.VMEM((n_groups, t, 2 * HEAD_DIM), BF16)]
            + [pltpu.VMEM((2, Q_BLOCK, 2 * Q_BLOCK), F32)]
            + [pltpu.VMEM((t // Q_BLOCK, Q_BLOCK, 2 * Q_BLOCK), F32)]
            + [pltpu.VMEM((ROLL_RING, ROLL_CHUNK) + cache.shape[3:], cache.dtype)]
            + [pltpu.SemaphoreType.DMA((2, ROLL_RING))]
            + [pltpu.VMEM((t, HEAD_DIM), F32)] * (2 + 2 * (n_groups - 1))),
        compiler_params=pltpu.CompilerParams(dimension_semantics=("arbitrary", "arbitrary"),
                                             vmem_limit_bytes=ATTN_VMEM_LIMIT),
        name="attn_prompt",
    )(proj, proj, proj, proj, cache)


def _group_count(dist):
    cnt = jnp.zeros(dist.shape, F32)
    for window, dil in DILATED_GROUPS:
        hit = (dist >= 0) & (dist <= window) & ((dist & (dil - 1)) == 0)
        cnt = cnt + hit.astype(F32)
    return cnt


def _attn_sample_kernel(q_ref, g_ref, kn_ref, vn_ref, ks_ref, vs_ref, kd_ref, vd_ref, o_ref,
                        m_ref, l_ref, acc_ref, *, n_buf):
    j = pl.program_id(1)
    nj = pl.num_programs(1)
    n_rows = q_ref.shape[0]
    nt_dims = (((1,), (1,)), ((), ()))
    head_bits = N_HEADS.bit_length() - 1
    dil = DILATED_GROUPS[-1][1]

    @pl.when(j == 0)
    def _():
        m_ref[...] = jnp.full(m_ref.shape, NEG_INF, F32)
        l_ref[...] = jnp.zeros(l_ref.shape, F32)
        acc_ref[...] = jnp.zeros(acc_ref.shape, F32)

    q = (q_ref[...] * ATTN_SCALE).astype(BF16)

    def accumulate(k2, v2, key_pos):
        n_cols = k2.shape[0]
        row = lax.broadcasted_iota(jnp.int32, (n_rows, n_cols), 0)
        col = lax.broadcasted_iota(jnp.int32, (n_rows, n_cols), 1)
        same_head = (row & (N_HEADS - 1)) == (col & (N_HEADS - 1))
        dist = (n_buf + (row >> head_bits)) - key_pos(col)
        cnt = jnp.where(same_head, _group_count(dist), 0.0)
        s = lax.dot_general(q, k2.astype(BF16), nt_dims, preferred_element_type=F32)
        s = jnp.where(cnt > 0, s, NEG_INF)
        m_prev = m_ref[...]
        m_new = jnp.maximum(m_prev, jnp.max(s, axis=-1, keepdims=True))
        alpha = jnp.exp(m_prev - m_new)
        p = cnt * jnp.exp(s - m_new)
        l_ref[...] = alpha * l_ref[...] + jnp.sum(p, axis=-1, keepdims=True)
        acc_ref[...] = alpha * acc_ref[...] + jnp.dot(p.astype(BF16), v2.astype(BF16),
                                                      preferred_element_type=F32)
        m_ref[...] = m_new

    @pl.when(j < nj - 1)
    def _():
        periods, residues = ks_ref.shape[0], ks_ref.shape[1]
        n_cols = periods * residues * N_HEADS
        res_bits = residues.bit_length() - 1

        def key_pos(col):
            period = j * periods + (col >> (head_bits + res_bits))
            return period * dil + ((col >> head_bits) & (residues - 1))

        accumulate(ks_ref[...].reshape(n_cols, HEAD_DIM), vs_ref[...].reshape(n_cols, HEAD_DIM), key_pos)

    @pl.when(j == nj - 1)
    def _():
        n_dense = kd_ref.shape[0]
        accumulate(kd_ref[...].reshape(n_dense * N_HEADS, HEAD_DIM),
                   vd_ref[...].reshape(n_dense * N_HEADS, HEAD_DIM),
                   lambda col: (n_buf - n_dense) + (col >> head_bits))
        accumulate(kn_ref[...], vn_ref[...], lambda col: n_buf + (col >> head_bits))
        o_ref[...] = (acc_ref[...] / l_ref[...]) * _silu(g_ref[...])


def _attn_sample(q2, g2, kn2, vn2, cache_k, cache_v, layer):
    b, n_rows, _ = q2.shape
    n_layers, _, n_buf = cache_k.shape[:3]
    n_new = n_rows // N_HEADS
    dil = DILATED_GROUPS[-1][1]
    n_dense = SAMPLE_DENSE_ROWS
    assert all(w <= n_dense for w, _ in DILATED_GROUPS[:-1]) and n_buf % dil == 0 and n_dense % dil == 0
    assert n_new <= SAMPLE_RESIDUES and dil % SAMPLE_RESIDUES == 0 and n_buf % n_dense == 0
    n_sparse_steps = (n_buf - n_dense) // dil // SAMPLE_PERIODS
    assert n_sparse_steps * SAMPLE_PERIODS * dil == n_buf - n_dense and n_sparse_steps >= 1
    by_period = (n_layers, cache_k.shape[1], n_buf // dil, dil, N_HEADS, HEAD_DIM)
    rblk = (None, n_rows, HEAD_DIM)
    sparse = pl.BlockSpec((None, None, SAMPLE_PERIODS, SAMPLE_RESIDUES, N_HEADS, HEAD_DIM),
                          lambda i, j: (layer, i, jnp.minimum(j, n_sparse_steps - 1), 0, 0, 0))
    dense = pl.BlockSpec((None, None, n_dense, N_HEADS, HEAD_DIM),
                         lambda i, j: (layer, i, n_buf // n_dense - 1, 0, 0))
    return pl.pallas_call(
        functools.partial(_attn_sample_kernel, n_buf=n_buf),
        grid=(b, n_sparse_steps + 1),
        in_specs=[pl.BlockSpec(rblk, lambda i, j: (i, 0, 0))] * 4 + [sparse, sparse, dense, dense],
        out_specs=pl.BlockSpec(rblk, lambda i, j: (i, 0, 0)),
        out_shape=jax.ShapeDtypeStruct((b, n_rows, HEAD_DIM), F32),
        scratch_shapes=[pltpu.VMEM((n_rows, 1), F32), pltpu.VMEM((n_rows, 1), F32),
                        pltpu.VMEM((n_rows, HEAD_DIM), F32)],
        compiler_params=_params("parallel", "arbitrary"),
        name="attn_sample",
    )(q2, g2, kn2, vn2, cache_k.reshape(by_period), cache_v.reshape(by_period), cache_k, cache_v)


def _write_new_rows_kernel(rolled_ref, new_ref, out_ref, sem):
    del rolled_ref
    n_layers, n_batch, n_new = new_ref.shape[:3]
    n_buf = out_ref.shape[2]
    copies = [pltpu.make_async_copy(new_ref.at[l, b], out_ref.at[l, b, pl.ds(n_buf - n_new, n_new)],
                                    sem.at[l * n_batch + b])
              for l in range(n_layers) for b in range(n_batch)]
    for copy in copies:
        copy.start()
    for copy in copies:
        copy.wait()


def _write_new_rows(rolled, new):
    n_layers, n_batch = new.shape[:2]
    return pl.pallas_call(
        _write_new_rows_kernel,
        in_specs=[pl.BlockSpec(memory_space=pl.ANY), pl.BlockSpec(memory_space=pltpu.VMEM)],
        out_specs=pl.BlockSpec(memory_space=pl.ANY),
        out_shape=jax.ShapeDtypeStruct(rolled.shape, rolled.dtype),
        scratch_shapes=[pltpu.SemaphoreType.DMA((n_layers * n_batch,))],
        input_output_aliases={0: 0},
        name="write_new_rows",
    )(rolled, new)


def _head_major_kernel(*refs, n_layers):
    ko_ref, vo_ref = refs[2 * n_layers:]
    tq = refs[0].shape[0]
    for layer in range(n_layers):
        @pl.when(pl.program_id(0) == layer)
        def _(layer=layer):
            for src, dst in ((refs[2 * layer], ko_ref), (refs[2 * layer + 1], vo_ref)):
                for h in range(N_HEADS):
                    dst[pl.ds(h, tq, stride=N_HEADS), :] = src[:, h * HEAD_DIM:(h + 1) * HEAD_DIM]


def _head_major_kv(projs, n_keep, tq):
    n_layers = len(projs)
    b, t, _ = projs[0].shape
    first = (t - n_keep) // tq
    nj = n_keep // tq
    in_specs = []
    for layer in range(n_layers):
        for col in (1, 2):
            in_specs.append(pl.BlockSpec(
                (None, tq, D_ATTN),
                lambda l, i, j, layer=layer, col=col: (jnp.where(l == layer, i, 0),
                                                       jnp.where(l == layer, first + j, 0), col)))
    out_spec = pl.BlockSpec((None, None, tq * N_HEADS, HEAD_DIM), lambda l, i, j: (l, i, j, 0))
    out = jax.ShapeDtypeStruct((n_layers, b, n_keep * N_HEADS, HEAD_DIM), F32)
    k_out, v_out = pl.pallas_call(
        functools.partial(_head_major_kernel, n_layers=n_layers),
        grid=(n_layers, b, nj),
        in_specs=in_specs,
        out_specs=[out_spec, out_spec],
        out_shape=[out, out],
        compiler_params=_params("arbitrary", "arbitrary", "arbitrary"),
        name="head_major_kv",
    )(*[p for p in projs for _ in (1, 2)])
    shape = (n_layers, b, n_keep, N_HEADS, HEAD_DIM)
    return k_out.reshape(shape), v_out.reshape(shape)


def _cast_kernel(x_ref, o_ref):
    o_ref[...] = x_ref[...].astype(o_ref.dtype)


def _cast_bf16(w, rows):
    n_layers, r, c = w.shape
    return pl.pallas_call(
        _cast_kernel,
        grid=(n_layers, r // rows),
        in_specs=[pl.BlockSpec((1, rows, c), lambda l, i: (l, i, 0))],
        out_specs=pl.BlockSpec((1, rows, c), lambda l, i: (l, i, 0)),
        out_shape=jax.ShapeDtypeStruct(w.shape, BF16),
        compiler_params=_params("parallel", "parallel"),
        name="cast_bf16",
    )(w)


def kernel(x_prompt, x_sample, state_pool, state_conv, cache_k, cache_v, norm_w, final_norm_w,
           w_in_ab, pool_lin, pool_scale, conv_w, w_out_ab, w_in_c, w_out_c):
    bp, t, d = x_prompt.shape
    bs, ts, _ = x_sample.shape
    ts_pad = -(-ts // SUBLANES) * SUBLANES
    n_buf = cache_k.shape[2]
    n_keep_p = min(MAX_WINDOW, t)

    hp = x_prompt.reshape(bp * t, d)
    hs = jnp.pad(x_sample, ((0, 0), (0, ts_pad - ts), (0, 0))).reshape(bs * ts_pad, d)
    zero_pool = jnp.zeros((bp, POOL_HIST, D_POOL), F32)
    zero_conv = jnp.zeros((bp, CONV_HIST, D_CONV), F32)

    tm_p, tn = 1024, 1024
    tm_s = bs * ts_pad
    cast_rows = 256

    w_in_ab = _cast_bf16(w_in_ab, cast_rows)
    w_out_ab = _cast_bf16(w_out_ab, cast_rows)
    w_in_c = _cast_bf16(w_in_c, cast_rows)
    w_out_c = _cast_bf16(w_out_c, cast_rows)
    pool_lin = _cast_bf16(pool_lin.reshape(pool_lin.shape[0], -1, D_POOL_GROUP), cast_rows).reshape(pool_lin.shape)

    def head_rows(a):
        return a.reshape(a.shape[0], a.shape[1] * N_HEADS, HEAD_DIM)

    assert DEPTH == 4
    assert min(MAX_WINDOW, n_buf + ts) == n_buf
    rolled = []
    fw = final_norm_w.reshape(1, d)
    pool_p, pool_s, conv_p, conv_s = [], [], [], []
    projs_c, k_new, v_new = [], [], []
    for l in range(DEPTH):
        i = l // 2
        nw = norm_w[l].reshape(1, d)
        if l % 2 == 0:
            pscale = pool_scale[i].reshape(1, D_POOL)
            hp, pp, cp = _ab_layer(hp.reshape(bp, t, d), nw, w_in_ab, w_out_ab, i, zero_pool, zero_conv,
                                   pool_lin[i], pscale, conv_w[i], AB_ROW_TILE, 0, t)
            hp = hp.reshape(bp * t, d)
            proj_s = _norm_matmul(hs, nw, w_in_ab, i, tm_s, tn).reshape(bs, ts_pad, -1)
            mix_s, ps, cs = _ab_mix(proj_s, state_pool[i], state_conv[i], pool_lin[i], pscale, conv_w[i],
                                    ts_pad, PAST_LEN, ts, F32)
            hs = _matmul_residual(mix_s.reshape(bs * ts_pad, -1), w_out_ab, i, hs, tm_s, tn)
            pool_p.append(pp)
            pool_s.append(ps)
            conv_p.append(cp)
            conv_s.append(cs)
        else:
            proj_p = _norm_matmul(hp, nw, w_in_c, i, tm_p, tn).reshape(bp, t, -1)
            proj_s = _norm_matmul(hs, nw, w_in_c, i, tm_s, tn).reshape(bs, ts_pad, -1)[:, :ts]
            og_p, rolled_c = _attn_prompt(proj_p, (cache_k, cache_v)[i], ts, BF16)
            rolled.append(rolled_c)
            q_s, kn_s, vn_s, g_s = (head_rows(proj_s[..., c * D_ATTN:(c + 1) * D_ATTN]) for c in range(4))
            og_s = _attn_sample(q_s, g_s, kn_s, vn_s, cache_k, cache_v, i)
            og_s = jnp.pad(og_s.reshape(bs, ts, D_ATTN), ((0, 0), (0, ts_pad - ts), (0, 0)))
            if l == DEPTH - 1:
                hp = _matmul_residual_norm(og_p.reshape(bp * t, -1), w_out_c, i, hp, fw, 512)
                hs = _matmul_residual_norm(og_s.reshape(bs * ts_pad, -1), w_out_c, i, hs, fw, tm_s)
            else:
                hp = _matmul_residual(og_p.reshape(bp * t, -1), w_out_c, i, hp, tm_p, tn)
                hs = _matmul_residual(og_s.reshape(bs * ts_pad, -1), w_out_c, i, hs, tm_s, tn)
            projs_c.append(proj_p)
            k_new.append(kn_s.reshape(bs, ts, N_HEADS, HEAD_DIM))
            v_new.append(vn_s.reshape(bs, ts, N_HEADS, HEAD_DIM))

    k_s = _write_new_rows(rolled[0], jnp.stack(k_new))
    v_s = _write_new_rows(rolled[1], jnp.stack(v_new))
    k_p, v_p = _head_major_kv(projs_c, n_keep_p, 256)

    y_prompt = hp.reshape(bp, t, d)
    y_sample = hs.reshape(bs, ts_pad, d)[:, :ts]
    return (y_prompt, y_sample, jnp.stack(pool_p), jnp.stack(pool_s), jnp.stack(conv_p), jnp.stack(conv_s),
            k_p, k_s, v_p, v_s)
```

```python
import functools

import jax
import jax.numpy as jnp
from jax import lax
from jax.experimental import pallas as pl
from jax.experimental.pallas import tpu as pltpu

F32 = jnp.float32
BF16 = jnp.bfloat16

D_MODEL = 2048
DEPTH = 4
PAST_LEN = 16384
D_POOL = D_MODEL // 2
POOL_WINDOWS = (2, 4, 8, 16)
D_POOL_GROUP = D_POOL // len(POOL_WINDOWS)
POOL_HIST = max(POOL_WINDOWS) - 1
D_CONV = D_MODEL // 2
CONV_WIDTH = 3
CONV_HIST = CONV_WIDTH - 1
HEAD_DIM = 128
N_HEADS = D_MODEL // HEAD_DIM
D_ATTN = N_HEADS * HEAD_DIM
DILATED_GROUPS = ((128, 1), (512, 4), (2048, 16))
MAX_WINDOW = max(w for w, _ in DILATED_GROUPS)
Q_BLOCK = 128
ATTN_SCALE = HEAD_DIM ** -0.5
LOG2_E = 1.4426950408889634
RMS_EPS = 1e-6
NEG_INF = -1e30

SUBLANES = 8
POOL_PAD = 16
CONV_PAD = 8
DEINT_CHUNK = 2 * Q_BLOCK
SAMPLE_DENSE_ROWS = 512
SAMPLE_PERIODS = 32
SAMPLE_RESIDUES = 4
ROLL_CHUNK = 128
ROLL_RING = 4
ROLL_DMA_PRIORITY = 1
SCORE_BLOCKS_PER_TRIP = 16
VALUE_BLOCKS_PER_TRIP = 16
MERGE_BLOCKS_PER_TRIP = 8
VMEM_LIMIT = 48 * 1024 * 1024
AB_LAYER_VMEM_LIMIT = 56 * 1024 * 1024
ATTN_VMEM_LIMIT = 54 * 1024 * 1024
AB_ROW_TILE = 256
AB_OUT_CHUNK = 512
AB_PROJ_CHUNK = 1024


def _silu(x):
    return x * jax.nn.sigmoid(x)


def _params(*semantics):
    return pltpu.CompilerParams(dimension_semantics=semantics, vmem_limit_bytes=VMEM_LIMIT)


def _norm_matmul_kernel(x_ref, nw_ref, w_ref, o_ref, xn_ref):
    @pl.when(pl.program_id(1) == 0)
    def _():
        x = x_ref[...]
        ms = jnp.mean(x * x, axis=-1, keepdims=True)
        xn_ref[...] = (x * lax.rsqrt(ms + RMS_EPS) * nw_ref[...]).astype(BF16)

    o_ref[...] = jnp.dot(xn_ref[...], w_ref[...], preferred_element_type=F32)


def _norm_matmul(x, nw, w, layer, tm, tn):
    m, d = x.shape
    n = w.shape[2]
    return pl.pallas_call(
        _norm_matmul_kernel,
        grid=(m // tm, n // tn),
        in_specs=[
            pl.BlockSpec((tm, d), lambda i, j: (i, 0)),
            pl.BlockSpec((1, d), lambda i, j: (0, 0)),
            pl.BlockSpec((None, d, tn), lambda i, j: (layer, 0, j)),
        ],
        out_specs=pl.BlockSpec((tm, tn), lambda i, j: (i, j)),
        out_shape=jax.ShapeDtypeStruct((m, n), F32),
        scratch_shapes=[pltpu.VMEM((tm, d), BF16)],
        compiler_params=_params("parallel", "arbitrary"),
        name="norm_matmul",
    )(x, nw, w)


def _matmul_residual_kernel(a_ref, w_ref, h_ref, o_ref):
    o_ref[...] = h_ref[...] + jnp.dot(a_ref[...].astype(BF16), w_ref[...], preferred_element_type=F32)


def _matmul_residual(a, w, layer, h, tm, tn):
    m, k = a.shape
    n = w.shape[2]
    return pl.pallas_call(
        _matmul_residual_kernel,
        grid=(m // tm, n // tn),
        in_specs=[
            pl.BlockSpec((tm, k), lambda i, j: (i, 0)),
            pl.BlockSpec((None, k, tn), lambda i, j: (layer, 0, j)),
            pl.BlockSpec((tm, tn), lambda i, j: (i, j)),
        ],
        out_specs=pl.BlockSpec((tm, tn), lambda i, j: (i, j)),
        out_shape=jax.ShapeDtypeStruct((m, n), F32),
        compiler_params=_params("parallel", "arbitrary"),
        name="matmul_residual",
    )(a, w, h)


def _matmul_residual_norm_kernel(a_ref, w_ref, h_ref, nw_ref, o_ref):
    y = h_ref[...] + jnp.dot(a_ref[...].astype(BF16), w_ref[...], preferred_element_type=F32)
    ms = jnp.mean(y * y, axis=-1, keepdims=True)
    o_ref[...] = y * lax.rsqrt(ms + RMS_EPS) * nw_ref[...]


def _matmul_residual_norm(a, w, layer, h, nw, tm):
    m, k = a.shape
    n = w.shape[2]
    return pl.pallas_call(
        _matmul_residual_norm_kernel,
        grid=(m // tm,),
        in_specs=[
            pl.BlockSpec((tm, k), lambda i: (i, 0)),
            pl.BlockSpec((None, k, n), lambda i: (layer, 0, 0)),
            pl.BlockSpec((tm, n), lambda i: (i, 0)),
            pl.BlockSpec((1, n), lambda i: (0, 0)),
        ],
        out_specs=pl.BlockSpec((tm, n), lambda i: (i, 0)),
        out_shape=jax.ShapeDtypeStruct((m, n), F32),
        compiler_params=_params("parallel"),
        name="matmul_residual_norm",
    )(a, w, h, nw)


def _mix_history(ph_ref, ch_ref, uext_ref, zext_ref, tm):
    j = pl.program_id(1)

    @pl.when(j == 0)
    def _():
        uext_ref[POOL_PAD - POOL_HIST:POOL_PAD, :] = ph_ref[0]
        zext_ref[CONV_PAD - CONV_HIST:CONV_PAD, :] = ch_ref[0]

    @pl.when(j > 0)
    def _():
        uext_ref[0:POOL_PAD, :] = uext_ref[tm:tm + POOL_PAD, :]
        zext_ref[0:CONV_PAD, :] = zext_ref[tm:tm + CONV_PAD, :]


def _mix_tails(ptail_ref, ctail_ref, uext_ref, zext_ref, t_last):
    @pl.when(pl.program_id(1) == pl.num_programs(1) - 1)
    def _():
        ptail_ref[0] = uext_ref[POOL_PAD + t_last - POOL_HIST:POOL_PAD + t_last, :]
        ctail_ref[0] = zext_ref[CONV_PAD + t_last - CONV_HIST:CONV_PAD + t_last, :]


def _mix_body(proj_ref, plin_ref, pscale_ref, cw_ref, mix_ref, uext_ref, zext_ref, pos0):
    j = pl.program_id(1)
    tm = proj_ref.shape[1]
    gw = D_POOL_GROUP

    uext_ref[POOL_PAD:POOL_PAD + tm, :] = proj_ref[0, :, 0:D_POOL]
    c_off = 2 * D_POOL + D_CONV
    zext_ref[CONV_PAD:CONV_PAD + tm, :] = (
        proj_ref[0, :, c_off:c_off + D_CONV] * proj_ref[0, :, c_off + D_CONV:c_off + 2 * D_CONV])

    pos = (pos0 + j * tm + lax.broadcasted_iota(jnp.int32, (tm, 1), 0)).astype(F32)

    for g, k in enumerate(POOL_WINDOWS):
        c0 = g * gw
        u_g = uext_ref[POOL_PAD:POOL_PAD + tm, c0:c0 + gw]
        s = u_g
        for i in range(1, k):
            s = s + uext_ref[POOL_PAD - i:POOL_PAD - i + tm, c0:c0 + gw]
        cnt = jnp.minimum(float(k), pos + 1.0)
        pooled = s / cnt - u_g
        a = jnp.dot(pooled.astype(BF16), plin_ref[g], preferred_element_type=F32)
        a = a * pscale_ref[:, c0:c0 + gw]
        gate = proj_ref[0, :, D_POOL + c0:D_POOL + c0 + gw]
        mix_ref[0, :, c0:c0 + gw] = (a * _silu(gate)).astype(mix_ref.dtype)

    for c in range(D_CONV // gw):
        c0 = c * gw
        conv = zext_ref[CONV_PAD - 2:CONV_PAD - 2 + tm, c0:c0 + gw] * cw_ref[0:1, c0:c0 + gw]
        conv = conv + zext_ref[CONV_PAD - 1:CONV_PAD - 1 + tm, c0:c0 + gw] * cw_ref[1:2, c0:c0 + gw]
        conv = conv + zext_ref[CONV_PAD:CONV_PAD + tm, c0:c0 + gw] * cw_ref[2:3, c0:c0 + gw]
        b_gate = proj_ref[0, :, 2 * D_POOL + c0:2 * D_POOL + c0 + gw]
        gate = proj_ref[0, :, 2 * D_POOL + 3 * D_CONV + c0:2 * D_POOL + 3 * D_CONV + c0 + gw]
        mix_ref[0, :, D_POOL + c0:D_POOL + c0 + gw] = (b_gate * conv * _silu(gate)).astype(mix_ref.dtype)


def _ab_mix_kernel(proj_ref, ph_ref, ch_ref, plin_ref, pscale_ref, cw_ref,
                   mix_ref, ptail_ref, ctail_ref, uext_ref, zext_ref, *, pos0, t_last):
    _mix_history(ph_ref, ch_ref, uext_ref, zext_ref, proj_ref.shape[1])
    _mix_body(proj_ref, plin_ref, pscale_ref, cw_ref, mix_ref, uext_ref, zext_ref, pos0)
    _mix_tails(ptail_ref, ctail_ref, uext_ref, zext_ref, t_last)


def _ab_mix(proj, pool_hist, conv_hist, plin, pscale, cw, tm, pos0, t_valid, mix_dtype):
    b, t, n = proj.shape
    nj = t // tm
    t_last = t_valid - (nj - 1) * tm
    kern = functools.partial(_ab_mix_kernel, pos0=pos0, t_last=t_last)
    return pl.pallas_call(
        kern,
        grid=(b, nj),
        in_specs=[
            pl.BlockSpec((1, tm, n), lambda i, j: (i, j, 0)),
            pl.BlockSpec((1, POOL_HIST, D_POOL), lambda i, j: (i, 0, 0)),
            pl.BlockSpec((1, CONV_HIST, D_CONV), lambda i, j: (i, 0, 0)),
            pl.BlockSpec(plin.shape, lambda i, j: (0, 0, 0)),
            pl.BlockSpec((1, D_POOL), lambda i, j: (0, 0)),
            pl.BlockSpec((CONV_WIDTH, D_CONV), lambda i, j: (0, 0)),
        ],
        out_specs=[
            pl.BlockSpec((1, tm, D_POOL + D_CONV), lambda i, j: (i, j, 0)),
            pl.BlockSpec((1, POOL_HIST, D_POOL), lambda i, j: (i, 0, 0)),
            pl.BlockSpec((1, CONV_HIST, D_CONV), lambda i, j: (i, 0, 0)),
        ],
        out_shape=[
            jax.ShapeDtypeStruct((b, t, D_POOL + D_CONV), mix_dtype),
            jax.ShapeDtypeStruct((b, POOL_HIST, D_POOL), F32),
            jax.ShapeDtypeStruct((b, CONV_HIST, D_CONV), F32),
        ],
        scratch_shapes=[pltpu.VMEM((tm + POOL_PAD, D_POOL), F32), pltpu.VMEM((tm + CONV_PAD, D_CONV), F32)],
        compiler_params=_params("parallel", "arbitrary"),
        name="ab_mix",
    )(proj, pool_hist, conv_hist, plin, pscale, cw)


def _ab_layer_kernel(x_ref, nw_ref, win_ref, ph_ref, ch_ref, plin_ref, pscale_ref, cw_ref, wout_ref,
                     o_ref, ptail_ref, ctail_ref, proj_ref, mix_ref, uext_ref, zext_ref, *, pos0, t_last):
    _mix_history(ph_ref, ch_ref, uext_ref, zext_ref, x_ref.shape[1])
    x = x_ref[0]
    ms = jnp.mean(x * x, axis=-1, keepdims=True)
    xn = (x * lax.rsqrt(ms + RMS_EPS) * nw_ref[...]).astype(BF16)
    n_in = win_ref.shape[1]
    for c0 in range(0, n_in, AB_PROJ_CHUNK):
        proj_ref[0, :, c0:c0 + AB_PROJ_CHUNK] = jnp.dot(xn, win_ref[:, c0:c0 + AB_PROJ_CHUNK],
                                                        preferred_element_type=F32)
    _mix_body(proj_ref, plin_ref, pscale_ref, cw_ref, mix_ref, uext_ref, zext_ref, pos0)
    y = x
    for c0 in range(0, wout_ref.shape[0], AB_OUT_CHUNK):
        y = y + jnp.dot(mix_ref[0, :, c0:c0 + AB_OUT_CHUNK].astype(BF16), wout_ref[c0:c0 + AB_OUT_CHUNK, :],
                        preferred_element_type=F32)
    o_ref[0] = y
    _mix_tails(ptail_ref, ctail_ref, uext_ref, zext_ref, t_last)


def _ab_layer(x, nw, w_in, w_out, layer, pool_hist, conv_hist, plin, pscale, cw, tm, pos0, t_valid):
    b, t, d = x.shape
    n_in = w_in.shape[2]
    nj = t // tm
    t_last = t_valid - (nj - 1) * tm
    resident = pl.Buffered(1)
    return pl.pallas_call(
        functools.partial(_ab_layer_kernel, pos0=pos0, t_last=t_last),
        grid=(b, nj),
        in_specs=[
            pl.BlockSpec((1, tm, d), lambda i, j: (i, j, 0)),
            pl.BlockSpec((1, d), lambda i, j: (0, 0)),
            pl.BlockSpec((None, d, n_in), lambda i, j: (layer, 0, 0), pipeline_mode=resident),
            pl.BlockSpec((1, POOL_HIST, D_POOL), lambda i, j: (i, 0, 0)),
            pl.BlockSpec((1, CONV_HIST, D_CONV), lambda i, j: (i, 0, 0)),
            pl.BlockSpec(plin.shape, lambda i, j: (0, 0, 0)),
            pl.BlockSpec((1, D_POOL), lambda i, j: (0, 0)),
            pl.BlockSpec((CONV_WIDTH, D_CONV), lambda i, j: (0, 0)),
            pl.BlockSpec((None, D_POOL + D_CONV, d), lambda i, j: (layer, 0, 0), pipeline_mode=resident),
        ],
        out_specs=[
            pl.BlockSpec((1, tm, d), lambda i, j: (i, j, 0)),
            pl.BlockSpec((1, POOL_HIST, D_POOL), lambda i, j: (i, 0, 0)),
            pl.BlockSpec((1, CONV_HIST, D_CONV), lambda i, j: (i, 0, 0)),
        ],
        out_shape=[
            jax.ShapeDtypeStruct((b, t, d), F32),
            jax.ShapeDtypeStruct((b, POOL_HIST, D_POOL), F32),
            jax.ShapeDtypeStruct((b, CONV_HIST, D_CONV), F32),
        ],
        scratch_shapes=[
            pltpu.VMEM((1, tm, n_in), F32),
            pltpu.VMEM((1, tm, D_POOL + D_CONV), BF16 if tm % (2 * SUBLANES) == 0 else F32),
            pltpu.VMEM((tm + POOL_PAD, D_POOL), F32),
            pltpu.VMEM((tm + CONV_PAD, D_CONV), F32),
        ],
        compiler_params=pltpu.CompilerParams(dimension_semantics=("parallel", "arbitrary"),
                                             vmem_limit_bytes=AB_LAYER_VMEM_LIMIT),
        name="ab_layer",
    )(x, nw, w_in, pool_hist, conv_hist, plin, pscale, cw, w_out)


def _rows(start, size, stride):
    if stride == 1:
        return pl.ds(start, size)
    return pl.ds(start, size, stride=stride)


def _attn_prompt_kernel(q_ref, k_ref, v_ref, g_ref, cache_ref, o_ref, roll_ref,
                        kd_ref, vd_ref, bias_ref, s_ref, ring_ref, sem_ref, *state_refs, n_new):
    t = q_ref.shape[0]
    n_cb, n_buf = cache_ref.shape[1], cache_ref.shape[2]
    step = pl.program_id(0) * pl.num_programs(1) + pl.program_id(1)
    slab, slab_half = step // 2, step % 2
    lyr, cb = slab // n_cb, slab % n_cb
    dst0 = slab_half * (n_buf // 2 - n_new)
    n_ring = ring_ref.shape[0]

    def roll_read(c):
        return pltpu.make_async_copy(cache_ref.at[lyr, cb, pl.ds(dst0 + c * ROLL_CHUNK + n_new, ROLL_CHUNK)],
                                     ring_ref.at[c % n_ring], sem_ref.at[0, c % n_ring])

    def roll_write(c):
        return pltpu.make_async_copy(ring_ref.at[c % n_ring],
                                     roll_ref.at[lyr, cb, pl.ds(dst0 + c * ROLL_CHUNK, ROLL_CHUNK)],
                                     sem_ref.at[1, c % n_ring])

    def roll_tail():
        return pltpu.make_async_copy(ring_ref.at[0, pl.ds(0, n_new)],
                                     roll_ref.at[lyr, cb, pl.ds(n_buf - n_new, n_new)], sem_ref.at[1, 0])

    for c in range(n_ring):
        roll_read(c).start(priority=ROLL_DMA_PRIORITY)
    n_groups = len(DILATED_GROUPS)
    qf_ref, kf_ref, vf_ref = state_refs[0:3]
    acc_refs = (None,) + tuple(state_refs[3:3 + n_groups - 1])
    stat_refs = (None,) + tuple(state_refs[3 + n_groups - 1:])
    mid_dil = DILATED_GROUPS[n_groups - 2][1]
    nt_dims = (((1,), (1,)), ((), ()))
    half = HEAD_DIM // 2
    lane = lax.broadcasted_iota(jnp.int32, (Q_BLOCK, HEAD_DIM), 1)

    @pl.when((pl.program_id(0) == 0) & (pl.program_id(1) == 0))
    def _():
        delta = (lax.broadcasted_iota(jnp.int32, (Q_BLOCK, 2 * Q_BLOCK), 0)
                 - lax.broadcasted_iota(jnp.int32, (Q_BLOCK, 2 * Q_BLOCK), 1))
        for sel in range(2):
            dist = delta + sel * Q_BLOCK
            bias_ref[sel] = jnp.where((dist >= 0) & (dist <= Q_BLOCK), 0.0, NEG_INF)
        vd_ref[:, :, HEAD_DIM:] = jnp.ones((n_groups, t, HEAD_DIM), BF16)

    def deinterleave(gi, d, d_src, ksrc_ref, vsrc_ref, keep_f32):
        stream_len = t // d
        ratio = d // d_src

        def body(c, carry):
            dst = pl.multiple_of(c * DEINT_CHUNK, DEINT_CHUNK)
            r = dst // stream_len
            m0 = dst % stream_len
            src = _rows((r % d_src) * (t // d_src) + r // d_src + ratio * m0, DEINT_CHUNK, ratio)
            k = ksrc_ref[src, :]
            v = vsrc_ref[src, :]
            kd_ref[gi, pl.ds(dst, DEINT_CHUNK), :] = k.astype(BF16)
            vd_ref[gi, pl.ds(dst, DEINT_CHUNK), 0:HEAD_DIM] = v.astype(BF16)
            if keep_f32:
                qf_ref[pl.ds(dst, DEINT_CHUNK), :] = q_ref[src, :]
                kf_ref[pl.ds(dst, DEINT_CHUNK), :] = k
                vf_ref[pl.ds(dst, DEINT_CHUNK), :] = v
            return carry

        lax.fori_loop(0, t // DEINT_CHUNK, body, 0, unroll=2)

    def merged_output(rows, m0, acc0, l0):
        ms, ls, accs = [m0], [l0], [acc0]
        for gi in range(1, n_groups):
            st = stat_refs[gi][rows, :]
            sw = pltpu.roll(st, half, axis=1)
            ms.append(jnp.where(lane < half, st, sw))
            ls.append(jnp.where(lane < half, sw, st))
            accs.append(acc_refs[gi][rows, :])
        m_all = functools.reduce(jnp.maximum, ms)
        num = None
        den = None
        for m_g, l_g, acc_g in zip(ms, ls, accs):
            w = jnp.exp2(m_g - m_all)
            num = w * acc_g if num is None else num + w * acc_g
            den = w * l_g if den is None else den + w * l_g
        g = g_ref[rows, :]
        return ((num * g) / (den * (1.0 + jnp.exp(-g)))).astype(o_ref.dtype)

    def run_group(gi, d, after_scores, after_values):
        stream_len = t // d
        nb = stream_len // Q_BLOCK

        def block_rows(idx):
            r = idx // nb
            mb = idx % nb
            kb = jnp.maximum(mb - 1, 0)
            q_start = r + d * Q_BLOCK * mb
            q_rows = _rows(pl.multiple_of(q_start, Q_BLOCK) if d == 1 else q_start, Q_BLOCK, d)
            k_rows = pl.ds(pl.multiple_of(r * stream_len + kb * Q_BLOCK, Q_BLOCK), 2 * Q_BLOCK)
            return q_rows, k_rows, mb - kb

        def load_q(idx, q_rows):
            if gi < n_groups - 1:
                return q_ref[q_rows, :]
            r = idx // nb
            mb = idx % nb
            ratio = d // mid_dil
            start = (r % mid_dil) * (t // mid_dil) + r // mid_dil + ratio * Q_BLOCK * mb
            return qf_ref[_rows(start, Q_BLOCK, ratio), :]

        def scores(it, carry):
            s = []
            for u in range(SCORE_BLOCKS_PER_TRIP):
                idx = it * SCORE_BLOCKS_PER_TRIP + u
                q_rows, k_rows, sel = block_rows(idx)
                q = (load_q(idx, q_rows) * (ATTN_SCALE * LOG2_E)).astype(BF16)
                s.append(lax.dot_general(q, kd_ref[gi, k_rows, :], nt_dims, preferred_element_type=F32)
                         + bias_ref[sel])
            for u in range(SCORE_BLOCKS_PER_TRIP):
                s_ref[it * SCORE_BLOCKS_PER_TRIP + u] = s[u]
            return carry

        n_val = MERGE_BLOCKS_PER_TRIP if gi == 0 else VALUE_BLOCKS_PER_TRIP

        def values(it, carry):
            rows, s, v = [], [], []
            for u in range(n_val):
                q_rows, k_rows, _ = block_rows(it * n_val + u)
                rows.append(q_rows)
                s.append(s_ref[it * n_val + u])
                v.append(vd_ref[gi, k_rows, :])
            outs = []
            for u in range(n_val):
                m_blk = jnp.max(s[u], axis=-1, keepdims=True)
                p = jnp.exp2(s[u] - m_blk).astype(BF16)
                pv = jnp.dot(p, v[u], preferred_element_type=F32)
                acc, l_blk = pv[:, 0:HEAD_DIM], pv[:, HEAD_DIM:]
                if gi == 0:
                    outs.append((merged_output(rows[u], jnp.broadcast_to(m_blk, acc.shape), acc, l_blk),))
                else:
                    outs.append((acc, jnp.where(lane < half, m_blk, l_blk)))
            for u in range(n_val):
                if gi == 0:
                    o_ref[rows[u], :] = outs[u][0]
                else:
                    acc_refs[gi][rows[u], :] = outs[u][0]
                    stat_refs[gi][rows[u], :] = outs[u][1]
            return carry

        lax.fori_loop(0, d * nb // SCORE_BLOCKS_PER_TRIP, scores, 0)
        after_scores()
        lax.fori_loop(0, d * nb // n_val, values, 0)
        after_values()

    ksrc_ref, vsrc_ref, d_src = k_ref, v_ref, 1
    for gi, (_, d) in enumerate(DILATED_GROUPS):
        keep_f32 = 0 < gi < n_groups - 1
        deinterleave(gi, d, d_src, ksrc_ref, vsrc_ref, keep_f32)
        if keep_f32:
            ksrc_ref, vsrc_ref, d_src = kf_ref, vf_ref, d
    def forward_first():
        for c in range(n_ring):
            roll_read(c).wait()
            roll_write(c).start(priority=ROLL_DMA_PRIORITY)

    def refill():
        for c in range(n_ring):
            roll_write(c).wait()
            roll_read(n_ring + c).start(priority=ROLL_DMA_PRIORITY)

    def forward_second():
        for c in range(n_ring, 2 * n_ring):
            roll_read(c).wait()
            roll_write(c).start(priority=ROLL_DMA_PRIORITY)

    def drain():
        for c in range(n_ring, 2 * n_ring):
            roll_write(c).wait()

        @pl.when(slab_half == 1)
        def _():
            roll_tail().start()
            roll_tail().wait()

    def nothing():
        pass

    hooks = [(forward_first, refill), (nothing, forward_second), (nothing, drain)]
    for (after_scores, after_values), gi in zip(hooks, reversed(range(n_groups))):
        run_group(gi, DILATED_GROUPS[gi][1], after_scores, after_values)


def _attn_prompt(proj, cache, n_new, out_dtype):
    b, t, _ = proj.shape
    n_groups = len(DILATED_GROUPS)
    n_layers, n_cb, n_buf = cache.shape[:3]
    assert b * N_HEADS == 2 * n_layers * n_cb and n_buf // 2 == 2 * ROLL_RING * ROLL_CHUNK
    dils = [dil for _, dil in DILATED_GROUPS]
    assert n_groups == 3 and dils[0] == 1 and all(hi % lo == 0 for lo, hi in zip(dils, dils[1:]))
    for window, dil in DILATED_GROUPS:
        assert window // dil == Q_BLOCK and (t // dil) % DEINT_CHUNK == 0
        assert (t // Q_BLOCK) % SCORE_BLOCKS_PER_TRIP == 0 and (t // Q_BLOCK) % VALUE_BLOCKS_PER_TRIP == 0
    blk = (None, t, HEAD_DIM)
    return pl.pallas_call(
        functools.partial(_attn_prompt_kernel, n_new=n_new),
        grid=(b, N_HEADS),
        in_specs=[
            pl.BlockSpec(blk, lambda i, h: (i, 0, h)),
            pl.BlockSpec(blk, lambda i, h: (i, 0, N_HEADS + h)),
            pl.BlockSpec(blk, lambda i, h: (i, 0, 2 * N_HEADS + h)),
            pl.BlockSpec(blk, lambda i, h: (i, 0, 3 * N_HEADS + h)),
            pl.BlockSpec(memory_space=pl.ANY),
        ],
        out_specs=[pl.BlockSpec(blk, lambda i, h: (i, 0, h)), pl.BlockSpec(memory_space=pl.ANY)],
        out_shape=[jax.ShapeDtypeStruct((b, t, D_ATTN), out_dtype),
                   jax.ShapeDtypeStruct(cache.shape, cache.dtype)],
        scratch_shapes=(
            [pltpu.VMEM((n_groups, t, HEAD_DIM), BF16), pltpu.VMEM((n_groups, t, 2 * HEAD_DIM), BF16)]
            + [pltpu.VMEM((2, Q_BLOCK, 2 * Q_BLOCK), F32)]
            + [pltpu.VMEM((t // Q_BLOCK, Q_BLOCK, 2 * Q_BLOCK), F32)]
            + [pltpu.VMEM((ROLL_RING, ROLL_CHUNK) + cache.shape[3:], cache.dtype)]
            + [pltpu.SemaphoreType.DMA((2, ROLL_RING))]
            + [pltpu.VMEM((t, HEAD_DIM), F32)] * (3 + 2 * (n_groups - 1))),
        compiler_params=pltpu.CompilerParams(dimension_semantics=("arbitrary", "arbitrary"),
                                             vmem_limit_bytes=ATTN_VMEM_LIMIT),
        name="attn_prompt",
    )(proj, proj, proj, proj, cache)


def _group_count(dist):
    cnt = jnp.zeros(dist.shape, F32)
    for window, dil in DILATED_GROUPS:
        hit = (dist >= 0) & (dist <= window) & ((dist & (dil - 1)) == 0)
        cnt = cnt + hit.astype(F32)
    return cnt


def _attn_sample_kernel(q_ref, g_ref, kn_ref, vn_ref, ks_ref, vs_ref, kd_ref, vd_ref, o_ref,
                        m_ref, l_ref, acc_ref, *, n_buf):
    j = pl.program_id(1)
    nj = pl.num_programs(1)
    n_rows = q_ref.shape[0]
    nt_dims = (((1,), (1,)), ((), ()))
    head_bits = N_HEADS.bit_length() - 1
    dil = DILATED_GROUPS[-1][1]

    @pl.when(j == 0)
    def _():
        m_ref[...] = jnp.full(m_ref.shape, NEG_INF, F32)
        l_ref[...] = jnp.zeros(l_ref.shape, F32)
        acc_ref[...] = jnp.zeros(acc_ref.shape, F32)

    q = (q_ref[...] * ATTN_SCALE).astype(BF16)

    def accumulate(k2, v2, key_pos):
        n_cols = k2.shape[0]
        row = lax.broadcasted_iota(jnp.int32, (n_rows, n_cols), 0)
        col = lax.broadcasted_iota(jnp.int32, (n_rows, n_cols), 1)
        same_head = (row & (N_HEADS - 1)) == (col & (N_HEADS - 1))
        dist = (n_buf + (row >> head_bits)) - key_pos(col)
        cnt = jnp.where(same_head, _group_count(dist), 0.0)
        s = lax.dot_general(q, k2.astype(BF16), nt_dims, preferred_element_type=F32)
        s = jnp.where(cnt > 0, s, NEG_INF)
        m_prev = m_ref[...]
        m_new = jnp.maximum(m_prev, jnp.max(s, axis=-1, keepdims=True))
        alpha = jnp.exp(m_prev - m_new)
        p = cnt * jnp.exp(s - m_new)
        l_ref[...] = alpha * l_ref[...] + jnp.sum(p, axis=-1, keepdims=True)
        acc_ref[...] = alpha * acc_ref[...] + jnp.dot(p.astype(BF16), v2.astype(BF16),
                                                      preferred_element_type=F32)
        m_ref[...] = m_new

    @pl.when(j < nj - 1)
    def _():
        periods, residues = ks_ref.shape[0], ks_ref.shape[1]
        n_cols = periods * residues * N_HEADS
        res_bits = residues.bit_length() - 1

        def key_pos(col):
            period = j * periods + (col >> (head_bits + res_bits))
            return period * dil + ((col >> head_bits) & (residues - 1))

        accumulate(ks_ref[...].reshape(n_cols, HEAD_DIM), vs_ref[...].reshape(n_cols, HEAD_DIM), key_pos)

    @pl.when(j == nj - 1)
    def _():
        n_dense = kd_ref.shape[0]
        accumulate(kd_ref[...].reshape(n_dense * N_HEADS, HEAD_DIM),
                   vd_ref[...].reshape(n_dense * N_HEADS, HEAD_DIM),
                   lambda col: (n_buf - n_dense) + (col >> head_bits))
        accumulate(kn_ref[...], vn_ref[...], lambda col: n_buf + (col >> head_bits))
        o_ref[...] = (acc_ref[...] / l_ref[...]) * _silu(g_ref[...])


def _attn_sample(q2, g2, kn2, vn2, cache_k, cache_v, layer):
    b, n_rows, _ = q2.shape
    n_layers, _, n_buf = cache_k.shape[:3]
    n_new = n_rows // N_HEADS
    dil = DILATED_GROUPS[-1][1]
    n_dense = SAMPLE_DENSE_ROWS
    assert all(w <= n_dense for w, _ in DILATED_GROUPS[:-1]) and n_buf % dil == 0 and n_dense % dil == 0
    assert n_new <= SAMPLE_RESIDUES and dil % SAMPLE_RESIDUES == 0 and n_buf % n_dense == 0
    n_sparse_steps = (n_buf - n_dense) // dil // SAMPLE_PERIODS
    assert n_sparse_steps * SAMPLE_PERIODS * dil == n_buf - n_dense and n_sparse_steps >= 1
    by_period = (n_layers, cache_k.shape[1], n_buf // dil, dil, N_HEADS, HEAD_DIM)
    rblk = (None, n_rows, HEAD_DIM)
    sparse = pl.BlockSpec((None, None, SAMPLE_PERIODS, SAMPLE_RESIDUES, N_HEADS, HEAD_DIM),
                          lambda i, j: (layer, i, jnp.minimum(j, n_sparse_steps - 1), 0, 0, 0))
    dense = pl.BlockSpec((None, None, n_dense, N_HEADS, HEAD_DIM),
                         lambda i, j: (layer, i, n_buf // n_dense - 1, 0, 0))
    return pl.pallas_call(
        functools.partial(_attn_sample_kernel, n_buf=n_buf),
        grid=(b, n_sparse_steps + 1),
        in_specs=[pl.BlockSpec(rblk, lambda i, j: (i, 0, 0))] * 4 + [sparse, sparse, dense, dense],
        out_specs=pl.BlockSpec(rblk, lambda i, j: (i, 0, 0)),
        out_shape=jax.ShapeDtypeStruct((b, n_rows, HEAD_DIM), F32),
        scratch_shapes=[pltpu.VMEM((n_rows, 1), F32), pltpu.VMEM((n_rows, 1), F32),
                        pltpu.VMEM((n_rows, HEAD_DIM), F32)],
        compiler_params=_params("parallel", "arbitrary"),
        name="attn_sample",
    )(q2, g2, kn2, vn2, cache_k.reshape(by_period), cache_v.reshape(by_period), cache_k, cache_v)


def _write_new_rows_kernel(rolled_ref, new_ref, out_ref, sem):
    del rolled_ref
    n_layers, n_batch, n_new = new_ref.shape[:3]
    n_buf = out_ref.shape[2]
    copies = [pltpu.make_async_copy(new_ref.at[l, b], out_ref.at[l, b, pl.ds(n_buf - n_new, n_new)],
                                    sem.at[l * n_batch + b])
              for l in range(n_layers) for b in range(n_batch)]
    for copy in copies:
        copy.start()
    for copy in copies:
        copy.wait()


def _write_new_rows(rolled, new):
    n_layers, n_batch = new.shape[:2]
    return pl.pallas_call(
        _write_new_rows_kernel,
        in_specs=[pl.BlockSpec(memory_space=pl.ANY), pl.BlockSpec(memory_space=pltpu.VMEM)],
        out_specs=pl.BlockSpec(memory_space=pl.ANY),
        out_shape=jax.ShapeDtypeStruct(rolled.shape, rolled.dtype),
        scratch_shapes=[pltpu.SemaphoreType.DMA((n_layers * n_batch,))],
        input_output_aliases={0: 0},
        name="write_new_rows",
    )(rolled, new)


def _head_major_kernel(*refs, n_layers):
    ko_ref, vo_ref = refs[2 * n_layers:]
    tq = refs[0].shape[0]
    for layer in range(n_layers):
        @pl.when(pl.program_id(0) == layer)
        def _(layer=layer):
            for src, dst in ((refs[2 * layer], ko_ref), (refs[2 * layer + 1], vo_ref)):
                for h in range(N_HEADS):
                    dst[pl.ds(h, tq, stride=N_HEADS), :] = src[:, h * HEAD_DIM:(h + 1) * HEAD_DIM]


def _head_major_kv(projs, n_keep, tq):
    n_layers = len(projs)
    b, t, _ = projs[0].shape
    first = (t - n_keep) // tq
    nj = n_keep // tq
    in_specs = []
    for layer in range(n_layers):
        for col in (1, 2):
            in_specs.append(pl.BlockSpec(
                (None, tq, D_ATTN),
                lambda l, i, j, layer=layer, col=col: (jnp.where(l == layer, i, 0),
                                                       jnp.where(l == layer, first + j, 0), col)))
    out_spec = pl.BlockSpec((None, None, tq * N_HEADS, HEAD_DIM), lambda l, i, j: (l, i, j, 0))
    out = jax.ShapeDtypeStruct((n_layers, b, n_keep * N_HEADS, HEAD_DIM), F32)
    k_out, v_out = pl.pallas_call(
        functools.partial(_head_major_kernel, n_layers=n_layers),
        grid=(n_layers, b, nj),
        in_specs=in_specs,
        out_specs=[out_spec, out_spec],
        out_shape=[out, out],
        compiler_params=_params("arbitrary", "arbitrary", "arbitrary"),
        name="head_major_kv",
    )(*[p for p in projs for _ in (1, 2)])
    shape = (n_layers, b, n_keep, N_HEADS, HEAD_DIM)
    return k_out.reshape(shape), v_out.reshape(shape)


def _cast_kernel(x_ref, o_ref):
    o_ref[...] = x_ref[...].astype(o_ref.dtype)


def _cast_bf16(w, rows):
    n_layers, r, c = w.shape
    return pl.pallas_call(
        _cast_kernel,
        grid=(n_layers, r // rows),
        in_specs=[pl.BlockSpec((1, rows, c), lambda l, i: (l, i, 0))],
        out_specs=pl.BlockSpec((1, rows, c), lambda l, i: (l, i, 0)),
        out_shape=jax.ShapeDtypeStruct(w.shape, BF16),
        compiler_params=_params("parallel", "parallel"),
        name="cast_bf16",
    )(w)


def kernel(x_prompt, x_sample, state_pool, state_conv, cache_k, cache_v, norm_w, final_norm_w,
           w_in_ab, pool_lin, pool_scale, conv_w, w_out_ab, w_in_c, w_out_c):
    bp, t, d = x_prompt.shape
    bs, ts, _ = x_sample.shape
    ts_pad = -(-ts // SUBLANES) * SUBLANES
    n_buf = cache_k.shape[2]
    n_keep_p = min(MAX_WINDOW, t)

    hp = x_prompt.reshape(bp * t, d)
    hs = jnp.pad(x_sample, ((0, 0), (0, ts_pad - ts), (0, 0))).reshape(bs * ts_pad, d)
    zero_pool = jnp.zeros((bp, POOL_HIST, D_POOL), F32)
    zero_conv = jnp.zeros((bp, CONV_HIST, D_CONV), F32)

    tm_p, tn = 1024, 1024
    tm_s = bs * ts_pad
    cast_rows = 256

    w_in_ab = _cast_bf16(w_in_ab, cast_rows)
    w_out_ab = _cast_bf16(w_out_ab, cast_rows)
    w_in_c = _cast_bf16(w_in_c, cast_rows)
    w_out_c = _cast_bf16(w_out_c, cast_rows)
    pool_lin = _cast_bf16(pool_lin.reshape(pool_lin.shape[0], -1, D_POOL_GROUP), cast_rows).reshape(pool_lin.shape)

    def head_rows(a):
        return a.reshape(a.shape[0], a.shape[1] * N_HEADS, HEAD_DIM)

    assert DEPTH == 4
    assert min(MAX_WINDOW, n_buf + ts) == n_buf
    rolled = []
    fw = final_norm_w.reshape(1, d)
    pool_p, pool_s, conv_p, conv_s = [], [], [], []
    projs_c, k_new, v_new = [], [], []
    for l in range(DEPTH):
        i = l // 2
        nw = norm_w[l].reshape(1, d)
        if l % 2 == 0:
            pscale = pool_scale[i].reshape(1, D_POOL)
            hp, pp, cp = _ab_layer(hp.reshape(bp, t, d), nw, w_in_ab, w_out_ab, i, zero_pool, zero_conv,
                                   pool_lin[i], pscale, conv_w[i], AB_ROW_TILE, 0, t)
            hp = hp.reshape(bp * t, d)
            proj_s = _norm_matmul(hs, nw, w_in_ab, i, tm_s, tn).reshape(bs, ts_pad, -1)
            mix_s, ps, cs = _ab_mix(proj_s, state_pool[i], state_conv[i], pool_lin[i], pscale, conv_w[i],
                                    ts_pad, PAST_LEN, ts, F32)
            hs = _matmul_residual(mix_s.reshape(bs * ts_pad, -1), w_out_ab, i, hs, tm_s, tn)
            pool_p.append(pp)
            pool_s.append(ps)
            conv_p.append(cp)
            conv_s.append(cs)
        else:
            proj_p = _norm_matmul(hp, nw, w_in_c, i, tm_p, tn).reshape(bp, t, -1)
            proj_s = _norm_matmul(hs, nw, w_in_c, i, tm_s, tn).reshape(bs, ts_pad, -1)[:, :ts]
            og_p, rolled_c = _attn_prompt(proj_p, (cache_k, cache_v)[i], ts, BF16)
            rolled.append(rolled_c)
            q_s, kn_s, vn_s, g_s = (head_rows(proj_s[..., c * D_ATTN:(c + 1) * D_ATTN]) for c in range(4))
            og_s = _attn_sample(q_s, g_s, kn_s, vn_s, cache_k, cache_v, i)
            og_s = jnp.pad(og_s.reshape(bs, ts, D_ATTN), ((0, 0), (0, ts_pad - ts), (0, 0)))
            if l == DEPTH - 1:
                hp = _matmul_residual_norm(og_p.reshape(bp * t, -1), w_out_c, i, hp, fw, 512)
                hs = _matmul_residual_norm(og_s.reshape(bs * ts_pad, -1), w_out_c, i, hs, fw, tm_s)
            else:
                hp = _matmul_residual(og_p.reshape(bp * t, -1), w_out_c, i, hp, tm_p, tn)
                hs = _matmul_residual(og_s.reshape(bs * ts_pad, -1), w_out_c, i, hs, tm_s, tn)
            projs_c.append(proj_p)
            k_new.append(kn_s.reshape(bs, ts, N_HEADS, HEAD_DIM))
            v_new.append(vn_s.reshape(bs, ts, N_HEADS, HEAD_DIM))

    k_s = _write_new_rows(rolled[0], jnp.stack(k_new))
    v_s = _write_new_rows(rolled[1], jnp.stack(v_new))
    k_p, v_p = _head_major_kv(projs_c, n_keep_p, 256)

    y_prompt = hp.reshape(bp, t, d)
    y_sample = hs.reshape(bs, ts_pad, d)[:, :ts]
    return (y_prompt, y_sample, jnp.stack(pool_p), jnp.stack(pool_s), jnp.stack(conv_p), jnp.stack(conv_s),
            k_p, k_s, v_p, v_s)
```

```python
import functools

import jax
import jax.numpy as jnp
from jax import lax
from jax.experimental import pallas as pl
from jax.experimental.pallas import tpu as pltpu

F32 = jnp.float32
BF16 = jnp.bfloat16

D_MODEL = 2048
DEPTH = 4
PAST_LEN = 16384
D_POOL = D_MODEL // 2
POOL_WINDOWS = (2, 4, 8, 16)
D_POOL_GROUP = D_POOL // len(POOL_WINDOWS)
POOL_HIST = max(POOL_WINDOWS) - 1
D_CONV = D_MODEL // 2
CONV_WIDTH = 3
CONV_HIST = CONV_WIDTH - 1
HEAD_DIM = 128
N_HEADS = D_MODEL // HEAD_DIM
D_ATTN = N_HEADS * HEAD_DIM
DILATED_GROUPS = ((128, 1), (512, 4), (2048, 16))
MAX_WINDOW = max(w for w, _ in DILATED_GROUPS)
Q_BLOCK = 128
ATTN_SCALE = HEAD_DIM ** -0.5
LOG2_E = 1.4426950408889634
RMS_EPS = 1e-6
NEG_INF = -1e30

SUBLANES = 8
POOL_PAD = 16
CONV_PAD = 8
DEINT_CHUNK = 2 * Q_BLOCK
SAMPLE_DENSE_ROWS = 512
SAMPLE_PERIODS = 32
SAMPLE_RESIDUES = 4
ROLL_CHUNK = 128
ROLL_RING = 4
ROLL_DMA_PRIORITY = 1
SCORE_BLOCKS_PER_TRIP = 16
VALUE_BLOCKS_PER_TRIP = 16
MERGE_BLOCKS_PER_TRIP = 8
VMEM_LIMIT = 48 * 1024 * 1024
AB_LAYER_VMEM_LIMIT = 56 * 1024 * 1024
ATTN_VMEM_LIMIT = 54 * 1024 * 1024
AB_ROW_TILE = 256
AB_OUT_CHUNK = 512
AB_PROJ_CHUNK = 1024


def _silu(x):
    return x * jax.nn.sigmoid(x)


def _params(*semantics):
    return pltpu.CompilerParams(dimension_semantics=semantics, vmem_limit_bytes=VMEM_LIMIT)


def _norm_matmul_kernel(x_ref, nw_ref, w_ref, o_ref, xn_ref):
    @pl.when(pl.program_id(1) == 0)
    def _():
        x = x_ref[...]
        ms = jnp.mean(x * x, axis=-1, keepdims=True)
        xn_ref[...] = (x * lax.rsqrt(ms + RMS_EPS) * nw_ref[...]).astype(BF16)

    o_ref[...] = jnp.dot(xn_ref[...], w_ref[...], preferred_element_type=F32)


def _norm_matmul(x, nw, w, layer, tm, tn):
    m, d = x.shape
    n = w.shape[2]
    return pl.pallas_call(
        _norm_matmul_kernel,
        grid=(m // tm, n // tn),
        in_specs=[
            pl.BlockSpec((tm, d), lambda i, j: (i, 0)),
            pl.BlockSpec((1, d), lambda i, j: (0, 0)),
            pl.BlockSpec((None, d, tn), lambda i, j: (layer, 0, j)),
        ],
        out_specs=pl.BlockSpec((tm, tn), lambda i, j: (i, j)),
        out_shape=jax.ShapeDtypeStruct((m, n), F32),
        scratch_shapes=[pltpu.VMEM((tm, d), BF16)],
        compiler_params=_params("parallel", "arbitrary"),
        name="norm_matmul",
    )(x, nw, w)


def _matmul_residual_kernel(a_ref, w_ref, h_ref, o_ref):
    o_ref[...] = h_ref[...] + jnp.dot(a_ref[...].astype(BF16), w_ref[...], preferred_element_type=F32)


def _matmul_residual(a, w, layer, h, tm, tn):
    m, k = a.shape
    n = w.shape[2]
    return pl.pallas_call(
        _matmul_residual_kernel,
        grid=(m // tm, n // tn),
        in_specs=[
            pl.BlockSpec((tm, k), lambda i, j: (i, 0)),
            pl.BlockSpec((None, k, tn), lambda i, j: (layer, 0, j)),
            pl.BlockSpec((tm, tn), lambda i, j: (i, j)),
        ],
        out_specs=pl.BlockSpec((tm, tn), lambda i, j: (i, j)),
        out_shape=jax.ShapeDtypeStruct((m, n), F32),
        compiler_params=_params("parallel", "arbitrary"),
        name="matmul_residual",
    )(a, w, h)


def _matmul_residual_norm_kernel(a_ref, w_ref, h_ref, nw_ref, o_ref):
    y = h_ref[...] + jnp.dot(a_ref[...].astype(BF16), w_ref[...], preferred_element_type=F32)
    ms = jnp.mean(y * y, axis=-1, keepdims=True)
    o_ref[...] = y * lax.rsqrt(ms + RMS_EPS) * nw_ref[...]


def _matmul_residual_norm(a, w, layer, h, nw, tm):
    m, k = a.shape
    n = w.shape[2]
    return pl.pallas_call(
        _matmul_residual_norm_kernel,
        grid=(m // tm,),
        in_specs=[
            pl.BlockSpec((tm, k), lambda i: (i, 0)),
            pl.BlockSpec((None, k, n), lambda i: (layer, 0, 0)),
            pl.BlockSpec((tm, n), lambda i: (i, 0)),
            pl.BlockSpec((1, n), lambda i: (0, 0)),
        ],
        out_specs=pl.BlockSpec((tm, n), lambda i: (i, 0)),
        out_shape=jax.ShapeDtypeStruct((m, n), F32),
        compiler_params=_params("parallel"),
        name="matmul_residual_norm",
    )(a, w, h, nw)


def _mix_history(ph_ref, ch_ref, uext_ref, zext_ref, tm):
    j = pl.program_id(1)

    @pl.when(j == 0)
    def _():
        uext_ref[POOL_PAD - POOL_HIST:POOL_PAD, :] = ph_ref[0]
        zext_ref[CONV_PAD - CONV_HIST:CONV_PAD, :] = ch_ref[0]

    @pl.when(j > 0)
    def _():
        uext_ref[0:POOL_PAD, :] = uext_ref[tm:tm + POOL_PAD, :]
        zext_ref[0:CONV_PAD, :] = zext_ref[tm:tm + CONV_PAD, :]


def _mix_tails(ptail_ref, ctail_ref, uext_ref, zext_ref, t_last):
    @pl.when(pl.program_id(1) == pl.num_programs(1) - 1)
    def _():
        ptail_ref[0] = uext_ref[POOL_PAD + t_last - POOL_HIST:POOL_PAD + t_last, :]
        ctail_ref[0] = zext_ref[CONV_PAD + t_last - CONV_HIST:CONV_PAD + t_last, :]


def _mix_body(proj_ref, plin_ref, pscale_ref, cw_ref, mix_ref, uext_ref, zext_ref, pos0):
    j = pl.program_id(1)
    tm = proj_ref.shape[1]
    gw = D_POOL_GROUP

    uext_ref[POOL_PAD:POOL_PAD + tm, :] = proj_ref[0, :, 0:D_POOL]
    c_off = 2 * D_POOL + D_CONV
    zext_ref[CONV_PAD:CONV_PAD + tm, :] = (
        proj_ref[0, :, c_off:c_off + D_CONV] * proj_ref[0, :, c_off + D_CONV:c_off + 2 * D_CONV])

    pos = (pos0 + j * tm + lax.broadcasted_iota(jnp.int32, (tm, 1), 0)).astype(F32)

    for g, k in enumerate(POOL_WINDOWS):
        c0 = g * gw
        u_g = uext_ref[POOL_PAD:POOL_PAD + tm, c0:c0 + gw]
        s = u_g
        for i in range(1, k):
            s = s + uext_ref[POOL_PAD - i:POOL_PAD - i + tm, c0:c0 + gw]
        cnt = jnp.minimum(float(k), pos + 1.0)
        pooled = s / cnt - u_g
        a = jnp.dot(pooled.astype(BF16), plin_ref[g], preferred_element_type=F32)
        a = a * pscale_ref[:, c0:c0 + gw]
        gate = proj_ref[0, :, D_POOL + c0:D_POOL + c0 + gw]
        mix_ref[0, :, c0:c0 + gw] = (a * _silu(gate)).astype(mix_ref.dtype)

    for c in range(D_CONV // gw):
        c0 = c * gw
        conv = zext_ref[CONV_PAD - 2:CONV_PAD - 2 + tm, c0:c0 + gw] * cw_ref[0:1, c0:c0 + gw]
        conv = conv + zext_ref[CONV_PAD - 1:CONV_PAD - 1 + tm, c0:c0 + gw] * cw_ref[1:2, c0:c0 + gw]
        conv = conv + zext_ref[CONV_PAD:CONV_PAD + tm, c0:c0 + gw] * cw_ref[2:3, c0:c0 + gw]
        b_gate = proj_ref[0, :, 2 * D_POOL + c0:2 * D_POOL + c0 + gw]
        gate = proj_ref[0, :, 2 * D_POOL + 3 * D_CONV + c0:2 * D_POOL + 3 * D_CONV + c0 + gw]
        mix_ref[0, :, D_POOL + c0:D_POOL + c0 + gw] = (b_gate * conv * _silu(gate)).astype(mix_ref.dtype)


def _ab_mix_kernel(proj_ref, ph_ref, ch_ref, plin_ref, pscale_ref, cw_ref,
                   mix_ref, ptail_ref, ctail_ref, uext_ref, zext_ref, *, pos0, t_last):
    _mix_history(ph_ref, ch_ref, uext_ref, zext_ref, proj_ref.shape[1])
    _mix_body(proj_ref, plin_ref, pscale_ref, cw_ref, mix_ref, uext_ref, zext_ref, pos0)
    _mix_tails(ptail_ref, ctail_ref, uext_ref, zext_ref, t_last)


def _ab_mix(proj, pool_hist, conv_hist, plin, pscale, cw, tm, pos0, t_valid, mix_dtype):
    b, t, n = proj.shape
    nj = t // tm
    t_last = t_valid - (nj - 1) * tm
    kern = functools.partial(_ab_mix_kernel, pos0=pos0, t_last=t_last)
    return pl.pallas_call(
        kern,
        grid=(b, nj),
        in_specs=[
            pl.BlockSpec((1, tm, n), lambda i, j: (i, j, 0)),
            pl.BlockSpec((1, POOL_HIST, D_POOL), lambda i, j: (i, 0, 0)),
            pl.BlockSpec((1, CONV_HIST, D_CONV), lambda i, j: (i, 0, 0)),
            pl.BlockSpec(plin.shape, lambda i, j: (0, 0, 0)),
            pl.BlockSpec((1, D_POOL), lambda i, j: (0, 0)),
            pl.BlockSpec((CONV_WIDTH, D_CONV), lambda i, j: (0, 0)),
        ],
        out_specs=[
            pl.BlockSpec((1, tm, D_POOL + D_CONV), lambda i, j: (i, j, 0)),
            pl.BlockSpec((1, POOL_HIST, D_POOL), lambda i, j: (i, 0, 0)),
            pl.BlockSpec((1, CONV_HIST, D_CONV), lambda i, j: (i, 0, 0)),
        ],
        out_shape=[
            jax.ShapeDtypeStruct((b, t, D_POOL + D_CONV), mix_dtype),
            jax.ShapeDtypeStruct((b, POOL_HIST, D_POOL), F32),
            jax.ShapeDtypeStruct((b, CONV_HIST, D_CONV), F32),
        ],
        scratch_shapes=[pltpu.VMEM((tm + POOL_PAD, D_POOL), F32), pltpu.VMEM((tm + CONV_PAD, D_CONV), F32)],
        compiler_params=_params("parallel", "arbitrary"),
        name="ab_mix",
    )(proj, pool_hist, conv_hist, plin, pscale, cw)


def _ab_layer_kernel(x_ref, nw_ref, win_ref, ph_ref, ch_ref, plin_ref, pscale_ref, cw_ref, wout_ref,
                     o_ref, ptail_ref, ctail_ref, proj_ref, mix_ref, uext_ref, zext_ref, *, pos0, t_last):
    _mix_history(ph_ref, ch_ref, uext_ref, zext_ref, x_ref.shape[1])
    x = x_ref[0]
    ms = jnp.mean(x * x, axis=-1, keepdims=True)
    xn = (x * lax.rsqrt(ms + RMS_EPS) * nw_ref[...]).astype(BF16)
    n_in = win_ref.shape[1]
    for c0 in range(0, n_in, AB_PROJ_CHUNK):
        proj_ref[0, :, c0:c0 + AB_PROJ_CHUNK] = jnp.dot(xn, win_ref[:, c0:c0 + AB_PROJ_CHUNK],
                                                        preferred_element_type=F32)
    _mix_body(proj_ref, plin_ref, pscale_ref, cw_ref, mix_ref, uext_ref, zext_ref, pos0)
    y = x
    for c0 in range(0, wout_ref.shape[0], AB_OUT_CHUNK):
        y = y + jnp.dot(mix_ref[0, :, c0:c0 + AB_OUT_CHUNK].astype(BF16), wout_ref[c0:c0 + AB_OUT_CHUNK, :],
                        preferred_element_type=F32)
    o_ref[0] = y
    _mix_tails(ptail_ref, ctail_ref, uext_ref, zext_ref, t_last)


def _ab_layer(x, nw, w_in, w_out, layer, pool_hist, conv_hist, plin, pscale, cw, tm, pos0, t_valid):
    b, t, d = x.shape
    n_in = w_in.shape[2]
    nj = t // tm
    t_last = t_valid - (nj - 1) * tm
    resident = pl.Buffered(1)
    return pl.pallas_call(
        functools.partial(_ab_layer_kernel, pos0=pos0, t_last=t_last),
        grid=(b, nj),
        in_specs=[
            pl.BlockSpec((1, tm, d), lambda i, j: (i, j, 0)),
            pl.BlockSpec((1, d), lambda i, j: (0, 0)),
            pl.BlockSpec((None, d, n_in), lambda i, j: (layer, 0, 0), pipeline_mode=resident),
            pl.BlockSpec((1, POOL_HIST, D_POOL), lambda i, j: (i, 0, 0)),
            pl.BlockSpec((1, CONV_HIST, D_CONV), lambda i, j: (i, 0, 0)),
            pl.BlockSpec(plin.shape, lambda i, j: (0, 0, 0)),
            pl.BlockSpec((1, D_POOL), lambda i, j: (0, 0)),
            pl.BlockSpec((CONV_WIDTH, D_CONV), lambda i, j: (0, 0)),
            pl.BlockSpec((None, D_POOL + D_CONV, d), lambda i, j: (layer, 0, 0), pipeline_mode=resident),
        ],
        out_specs=[
            pl.BlockSpec((1, tm, d), lambda i, j: (i, j, 0)),
            pl.BlockSpec((1, POOL_HIST, D_POOL), lambda i, j: (i, 0, 0)),
            pl.BlockSpec((1, CONV_HIST, D_CONV), lambda i, j: (i, 0, 0)),
        ],
        out_shape=[
            jax.ShapeDtypeStruct((b, t, d), F32),
            jax.ShapeDtypeStruct((b, POOL_HIST, D_POOL), F32),
            jax.ShapeDtypeStruct((b, CONV_HIST, D_CONV), F32),
        ],
        scratch_shapes=[
            pltpu.VMEM((1, tm, n_in), F32),
            pltpu.VMEM((1, tm, D_POOL + D_CONV), BF16 if tm % (2 * SUBLANES) == 0 else F32),
            pltpu.VMEM((tm + POOL_PAD, D_POOL), F32),
            pltpu.VMEM((tm + CONV_PAD, D_CONV), F32),
        ],
        compiler_params=pltpu.CompilerParams(dimension_semantics=("parallel", "arbitrary"),
                                             vmem_limit_bytes=AB_LAYER_VMEM_LIMIT),
        name="ab_layer",
    )(x, nw, w_in, pool_hist, conv_hist, plin, pscale, cw, w_out)


def _rows(start, size, stride):
    if stride == 1:
        return pl.ds(start, size)
    return pl.ds(start, size, stride=stride)


def _attn_prompt_kernel(*refs, n_new, kv_layers, n_aliased):
    q_ref, k_blk_ref, v_blk_ref, g_ref, cache_ref = refs[:5]
    (o_ref, roll_ref, kp_ref, vp_ref, kd_ref, vd_ref, bias_ref, s_ref, ring_ref, sem_ref, kv_sem_ref,
     *state_refs) = refs[5 + n_aliased:]
    k_ref, v_ref = k_blk_ref.at[0], v_blk_ref.at[0]
    t = q_ref.shape[0]
    n_keep = kp_ref.shape[2]

    def kv_copies():
        copies = []
        for n, layer in enumerate(kv_layers):
            for m, (src, dst) in enumerate(((k_blk_ref, kp_ref), (v_blk_ref, vp_ref))):
                copies.append(pltpu.make_async_copy(
                    src.at[0, pl.ds(t - n_keep, n_keep), :],
                    dst.at[layer, pl.program_id(0), :, pl.program_id(1), :], kv_sem_ref.at[m, n]))
        return copies

    for copy in kv_copies():
        copy.start(priority=ROLL_DMA_PRIORITY)
    n_cb, n_buf = cache_ref.shape[1], cache_ref.shape[2]
    step = pl.program_id(0) * pl.num_programs(1) + pl.program_id(1)
    slab, slab_half = step // 2, step % 2
    lyr, cb = slab // n_cb, slab % n_cb
    dst0 = slab_half * (n_buf // 2 - n_new)
    n_ring = ring_ref.shape[0]

    def roll_read(c):
        return pltpu.make_async_copy(cache_ref.at[lyr, cb, pl.ds(dst0 + c * ROLL_CHUNK + n_new, ROLL_CHUNK)],
                                     ring_ref.at[c % n_ring], sem_ref.at[0, c % n_ring])

    def roll_write(c):
        return pltpu.make_async_copy(ring_ref.at[c % n_ring],
                                     roll_ref.at[lyr, cb, pl.ds(dst0 + c * ROLL_CHUNK, ROLL_CHUNK)],
                                     sem_ref.at[1, c % n_ring])

    def roll_tail():
        return pltpu.make_async_copy(ring_ref.at[0, pl.ds(0, n_new)],
                                     roll_ref.at[lyr, cb, pl.ds(n_buf - n_new, n_new)], sem_ref.at[1, 0])

    for c in range(n_ring):
        roll_read(c).start(priority=ROLL_DMA_PRIORITY)
    n_groups = len(DILATED_GROUPS)
    qf_ref, kf_ref, vf_ref = state_refs[0:3]
    acc_refs = (None,) + tuple(state_refs[3:3 + n_groups - 1])
    stat_refs = (None,) + tuple(state_refs[3 + n_groups - 1:])
    mid_dil = DILATED_GROUPS[n_groups - 2][1]
    nt_dims = (((1,), (1,)), ((), ()))
    half = HEAD_DIM // 2
    lane = lax.broadcasted_iota(jnp.int32, (Q_BLOCK, HEAD_DIM), 1)

    @pl.when((pl.program_id(0) == 0) & (pl.program_id(1) == 0))
    def _():
        delta = (lax.broadcasted_iota(jnp.int32, (Q_BLOCK, 2 * Q_BLOCK), 0)
                 - lax.broadcasted_iota(jnp.int32, (Q_BLOCK, 2 * Q_BLOCK), 1))
        for sel in range(2):
            dist = delta + sel * Q_BLOCK
            bias_ref[sel] = jnp.where((dist >= 0) & (dist <= Q_BLOCK), 0.0, NEG_INF)
        vd_ref[:, :, HEAD_DIM:] = jnp.ones((n_groups, t, HEAD_DIM), BF16)

    def deinterleave(gi, d, d_src, ksrc_ref, vsrc_ref, keep_f32):
        stream_len = t // d
        ratio = d // d_src

        def body(c, carry):
            dst = pl.multiple_of(c * DEINT_CHUNK, DEINT_CHUNK)
            r = dst // stream_len
            m0 = dst % stream_len
            src = _rows((r % d_src) * (t // d_src) + r // d_src + ratio * m0, DEINT_CHUNK, ratio)
            k = ksrc_ref[src, :]
            v = vsrc_ref[src, :]
            kd_ref[gi, pl.ds(dst, DEINT_CHUNK), :] = k.astype(BF16)
            vd_ref[gi, pl.ds(dst, DEINT_CHUNK), 0:HEAD_DIM] = v.astype(BF16)
            if keep_f32:
                qf_ref[pl.ds(dst, DEINT_CHUNK), :] = q_ref[src, :]
                kf_ref[pl.ds(dst, DEINT_CHUNK), :] = k
                vf_ref[pl.ds(dst, DEINT_CHUNK), :] = v
            return carry

        lax.fori_loop(0, t // DEINT_CHUNK, body, 0, unroll=2)

    def merged_output(rows, m0, acc0, l0):
        ms, ls, accs = [m0], [l0], [acc0]
        for gi in range(1, n_groups):
            st = stat_refs[gi][rows, :]
            sw = pltpu.roll(st, half, axis=1)
            ms.append(jnp.where(lane < half, st, sw))
            ls.append(jnp.where(lane < half, sw, st))
            accs.append(acc_refs[gi][rows, :])
        m_all = functools.reduce(jnp.maximum, ms)
        num = None
        den = None
        for m_g, l_g, acc_g in zip(ms, ls, accs):
            w = jnp.exp2(m_g - m_all)
            num = w * acc_g if num is None else num + w * acc_g
            den = w * l_g if den is None else den + w * l_g
        g = g_ref[rows, :]
        return ((num * g) / (den * (1.0 + jnp.exp(-g)))).astype(o_ref.dtype)

    def run_group(gi, d, after_scores, after_values):
        stream_len = t // d
        nb = stream_len // Q_BLOCK

        def block_rows(idx):
            r = idx // nb
            mb = idx % nb
            kb = jnp.maximum(mb - 1, 0)
            q_start = r + d * Q_BLOCK * mb
            q_rows = _rows(pl.multiple_of(q_start, Q_BLOCK) if d == 1 else q_start, Q_BLOCK, d)
            k_rows = pl.ds(pl.multiple_of(r * stream_len + kb * Q_BLOCK, Q_BLOCK), 2 * Q_BLOCK)
            return q_rows, k_rows, mb - kb

        def load_q(idx, q_rows):
            if gi < n_groups - 1:
                return q_ref[q_rows, :]
            r = idx // nb
            mb = idx % nb
            ratio = d // mid_dil
            start = (r % mid_dil) * (t // mid_dil) + r // mid_dil + ratio * Q_BLOCK * mb
            return qf_ref[_rows(start, Q_BLOCK, ratio), :]

        def scores(it, carry):
            s = []
            for u in range(SCORE_BLOCKS_PER_TRIP):
                idx = it * SCORE_BLOCKS_PER_TRIP + u
                q_rows, k_rows, sel = block_rows(idx)
                q = (load_q(idx, q_rows) * (ATTN_SCALE * LOG2_E)).astype(BF16)
                s.append(lax.dot_general(q, kd_ref[gi, k_rows, :], nt_dims, preferred_element_type=F32)
                         + bias_ref[sel])
            for u in range(SCORE_BLOCKS_PER_TRIP):
                s_ref[it * SCORE_BLOCKS_PER_TRIP + u] = s[u]
            return carry

        n_val = MERGE_BLOCKS_PER_TRIP if gi == 0 else VALUE_BLOCKS_PER_TRIP

        def values(it, carry):
            rows, s, v = [], [], []
            for u in range(n_val):
                q_rows, k_rows, _ = block_rows(it * n_val + u)
                rows.append(q_rows)
                s.append(s_ref[it * n_val + u])
                v.append(vd_ref[gi, k_rows, :])
            outs = []
            for u in range(n_val):
                m_blk = jnp.max(s[u], axis=-1, keepdims=True)
                p = jnp.exp2(s[u] - m_blk).astype(BF16)
                pv = jnp.dot(p, v[u], preferred_element_type=F32)
                acc, l_blk = pv[:, 0:HEAD_DIM], pv[:, HEAD_DIM:]
                if gi == 0:
                    outs.append((merged_output(rows[u], jnp.broadcast_to(m_blk, acc.shape), acc, l_blk),))
                else:
                    outs.append((acc, jnp.where(lane < half, m_blk, l_blk)))
            for u in range(n_val):
                if gi == 0:
                    o_ref[rows[u], :] = outs[u][0]
                else:
                    acc_refs[gi][rows[u], :] = outs[u][0]
                    stat_refs[gi][rows[u], :] = outs[u][1]
            return carry

        lax.fori_loop(0, d * nb // SCORE_BLOCKS_PER_TRIP, scores, 0)
        after_scores()
        lax.fori_loop(0, d * nb // n_val, values, 0)
        after_values()

    ksrc_ref, vsrc_ref, d_src = k_ref, v_ref, 1
    for gi, (_, d) in enumerate(DILATED_GROUPS):
        keep_f32 = 0 < gi < n_groups - 1
        deinterleave(gi, d, d_src, ksrc_ref, vsrc_ref, keep_f32)
        if keep_f32:
            ksrc_ref, vsrc_ref, d_src = kf_ref, vf_ref, d
    def forward_first():
        for c in range(n_ring):
            roll_read(c).wait()
            roll_write(c).start(priority=ROLL_DMA_PRIORITY)

    def refill():
        for c in range(n_ring):
            roll_write(c).wait()
            roll_read(n_ring + c).start(priority=ROLL_DMA_PRIORITY)

    def forward_second():
        for c in range(n_ring, 2 * n_ring):
            roll_read(c).wait()
            roll_write(c).start(priority=ROLL_DMA_PRIORITY)

    def drain():
        for c in range(n_ring, 2 * n_ring):
            roll_write(c).wait()
        for copy in kv_copies():
            copy.wait()

        @pl.when(slab_half == 1)
        def _():
            roll_tail().start()
            roll_tail().wait()

    def nothing():
        pass

    hooks = [(forward_first, refill), (nothing, forward_second), (nothing, drain)]
    for (after_scores, after_values), gi in zip(hooks, reversed(range(n_groups))):
        run_group(gi, DILATED_GROUPS[gi][1], after_scores, after_values)


def _attn_prompt(proj, cache, n_new, kv_layer, n_kv_layers, n_keep, kv_prev, out_dtype):
    b, t, _ = proj.shape
    first = kv_prev is None
    assert first == (kv_layer == 0)
    kv_layers = tuple(range(n_kv_layers)) if first else (kv_layer,)
    kv_shape = jax.ShapeDtypeStruct((n_kv_layers, b, n_keep, N_HEADS, HEAD_DIM), F32)
    any_spec = pl.BlockSpec(memory_space=pl.ANY)
    aliased = () if first else tuple(kv_prev)
    n_groups = len(DILATED_GROUPS)
    n_layers, n_cb, n_buf = cache.shape[:3]
    assert b * N_HEADS == 2 * n_layers * n_cb and n_buf // 2 == 2 * ROLL_RING * ROLL_CHUNK
    dils = [dil for _, dil in DILATED_GROUPS]
    assert n_groups == 3 and dils[0] == 1 and all(hi % lo == 0 for lo, hi in zip(dils, dils[1:]))
    for window, dil in DILATED_GROUPS:
        assert window // dil == Q_BLOCK and (t // dil) % DEINT_CHUNK == 0
        assert (t // Q_BLOCK) % SCORE_BLOCKS_PER_TRIP == 0 and (t // Q_BLOCK) % VALUE_BLOCKS_PER_TRIP == 0
    blk = (None, t, HEAD_DIM)
    og, rolled, kp, vp = pl.pallas_call(
        functools.partial(_attn_prompt_kernel, n_new=n_new, kv_layers=kv_layers, n_aliased=len(aliased)),
        grid=(b, N_HEADS),
        in_specs=[
            pl.BlockSpec(blk, lambda i, h: (i, 0, h)),
            pl.BlockSpec((1, t, HEAD_DIM), lambda i, h: (i, 0, N_HEADS + h)),
            pl.BlockSpec((1, t, HEAD_DIM), lambda i, h: (i, 0, 2 * N_HEADS + h)),
            pl.BlockSpec(blk, lambda i, h: (i, 0, 3 * N_HEADS + h)),
            any_spec,
        ] + [any_spec] * len(aliased),
        out_specs=[pl.BlockSpec(blk, lambda i, h: (i, 0, h)), any_spec, any_spec, any_spec],
        out_shape=[jax.ShapeDtypeStruct((b, t, D_ATTN), out_dtype),
                   jax.ShapeDtypeStruct(cache.shape, cache.dtype), kv_shape, kv_shape],
        scratch_shapes=(
            [pltpu.VMEM((n_groups, t, HEAD_DIM), BF16), pltpu.VMEM((n_groups, t, 2 * HEAD_DIM), BF16)]
            + [pltpu.VMEM((2, Q_BLOCK, 2 * Q_BLOCK), F32)]
            + [pltpu.VMEM((t // Q_BLOCK, Q_BLOCK, 2 * Q_BLOCK), F32)]
            + [pltpu.VMEM((ROLL_RING, ROLL_CHUNK) + cache.shape[3:], cache.dtype)]
            + [pltpu.SemaphoreType.DMA((2, ROLL_RING))]
            + [pltpu.SemaphoreType.DMA((2, len(kv_layers)))]
            + [pltpu.VMEM((t, HEAD_DIM), F32)] * (3 + 2 * (n_groups - 1))),
        input_output_aliases={5 + n: 2 + n for n in range(len(aliased))},
        compiler_params=pltpu.CompilerParams(dimension_semantics=("arbitrary", "arbitrary"),
                                             vmem_limit_bytes=ATTN_VMEM_LIMIT),
        name="attn_prompt",
    )(proj, proj, proj, proj, cache, *aliased)
    return og, rolled, (kp, vp)


def _group_count(dist):
    cnt = jnp.zeros(dist.shape, F32)
    for window, dil in DILATED_GROUPS:
        hit = (dist >= 0) & (dist <= window) & ((dist & (dil - 1)) == 0)
        cnt = cnt + hit.astype(F32)
    return cnt


def _attn_sample_kernel(q_ref, g_ref, kn_ref, vn_ref, ks_ref, vs_ref, kd_ref, vd_ref, o_ref,
                        m_ref, l_ref, acc_ref, *, n_buf):
    j = pl.program_id(1)
    nj = pl.num_programs(1)
    n_rows = q_ref.shape[0]
    nt_dims = (((1,), (1,)), ((), ()))
    head_bits = N_HEADS.bit_length() - 1
    dil = DILATED_GROUPS[-1][1]

    @pl.when(j == 0)
    def _():
        m_ref[...] = jnp.full(m_ref.shape, NEG_INF, F32)
        l_ref[...] = jnp.zeros(l_ref.shape, F32)
        acc_ref[...] = jnp.zeros(acc_ref.shape, F32)

    q = (q_ref[...] * ATTN_SCALE).astype(BF16)

    def accumulate(k2, v2, key_pos):
        n_cols = k2.shape[0]
        row = lax.broadcasted_iota(jnp.int32, (n_rows, n_cols), 0)
        col = lax.broadcasted_iota(jnp.int32, (n_rows, n_cols), 1)
        same_head = (row & (N_HEADS - 1)) == (col & (N_HEADS - 1))
        dist = (n_buf + (row >> head_bits)) - key_pos(col)
        cnt = jnp.where(same_head, _group_count(dist), 0.0)
        s = lax.dot_general(q, k2.astype(BF16), nt_dims, preferred_element_type=F32)
        s = jnp.where(cnt > 0, s, NEG_INF)
        m_prev = m_ref[...]
        m_new = jnp.maximum(m_prev, jnp.max(s, axis=-1, keepdims=True))
        alpha = jnp.exp(m_prev - m_new)
        p = cnt * jnp.exp(s - m_new)
        l_ref[...] = alpha * l_ref[...] + jnp.sum(p, axis=-1, keepdims=True)
        acc_ref[...] = alpha * acc_ref[...] + jnp.dot(p.astype(BF16), v2.astype(BF16),
                                                      preferred_element_type=F32)
        m_ref[...] = m_new

    @pl.when(j < nj - 1)
    def _():
        periods, residues = ks_ref.shape[0], ks_ref.shape[1]
        n_cols = periods * residues * N_HEADS
        res_bits = residues.bit_length() - 1

        def key_pos(col):
            period = j * periods + (col >> (head_bits + res_bits))
            return period * dil + ((col >> head_bits) & (residues - 1))

        accumulate(ks_ref[...].reshape(n_cols, HEAD_DIM), vs_ref[...].reshape(n_cols, HEAD_DIM), key_pos)

    @pl.when(j == nj - 1)
    def _():
        n_dense = kd_ref.shape[0]
        accumulate(kd_ref[...].reshape(n_dense * N_HEADS, HEAD_DIM),
                   vd_ref[...].reshape(n_dense * N_HEADS, HEAD_DIM),
                   lambda col: (n_buf - n_dense) + (col >> head_bits))
        accumulate(kn_ref[...], vn_ref[...], lambda col: n_buf + (col >> head_bits))
        o_ref[...] = (acc_ref[...] / l_ref[...]) * _silu(g_ref[...])


def _attn_sample(q2, g2, kn2, vn2, cache_k, cache_v, layer):
    b, n_rows, _ = q2.shape
    n_layers, _, n_buf = cache_k.shape[:3]
    n_new = n_rows // N_HEADS
    dil = DILATED_GROUPS[-1][1]
    n_dense = SAMPLE_DENSE_ROWS
    assert all(w <= n_dense for w, _ in DILATED_GROUPS[:-1]) and n_buf % dil == 0 and n_dense % dil == 0
    assert n_new <= SAMPLE_RESIDUES and dil % SAMPLE_RESIDUES == 0 and n_buf % n_dense == 0
    n_sparse_steps = (n_buf - n_dense) // dil // SAMPLE_PERIODS
    assert n_sparse_steps * SAMPLE_PERIODS * dil == n_buf - n_dense and n_sparse_steps >= 1
    by_period = (n_layers, cache_k.shape[1], n_buf // dil, dil, N_HEADS, HEAD_DIM)
    rblk = (None, n_rows, HEAD_DIM)
    sparse = pl.BlockSpec((None, None, SAMPLE_PERIODS, SAMPLE_RESIDUES, N_HEADS, HEAD_DIM),
                          lambda i, j: (layer, i, jnp.minimum(j, n_sparse_steps - 1), 0, 0, 0))
    dense = pl.BlockSpec((None, None, n_dense, N_HEADS, HEAD_DIM),
                         lambda i, j: (layer, i, n_buf // n_dense - 1, 0, 0))
    return pl.pallas_call(
        functools.partial(_attn_sample_kernel, n_buf=n_buf),
        grid=(b, n_sparse_steps + 1),
        in_specs=[pl.BlockSpec(rblk, lambda i, j: (i, 0, 0))] * 4 + [sparse, sparse, dense, dense],
        out_specs=pl.BlockSpec(rblk, lambda i, j: (i, 0, 0)),
        out_shape=jax.ShapeDtypeStruct((b, n_rows, HEAD_DIM), F32),
        scratch_shapes=[pltpu.VMEM((n_rows, 1), F32), pltpu.VMEM((n_rows, 1), F32),
                        pltpu.VMEM((n_rows, HEAD_DIM), F32)],
        compiler_params=_params("parallel", "arbitrary"),
        name="attn_sample",
    )(q2, g2, kn2, vn2, cache_k.reshape(by_period), cache_v.reshape(by_period), cache_k, cache_v)


def _write_new_rows_kernel(rolled_ref, new_ref, out_ref, sem):
    del rolled_ref
    n_layers, n_batch, n_new = new_ref.shape[:3]
    n_buf = out_ref.shape[2]
    copies = [pltpu.make_async_copy(new_ref.at[l, b], out_ref.at[l, b, pl.ds(n_buf - n_new, n_new)],
                                    sem.at[l * n_batch + b])
              for l in range(n_layers) for b in range(n_batch)]
    for copy in copies:
        copy.start()
    for copy in copies:
        copy.wait()


def _write_new_rows(rolled, new):
    n_layers, n_batch = new.shape[:2]
    return pl.pallas_call(
        _write_new_rows_kernel,
        in_specs=[pl.BlockSpec(memory_space=pl.ANY), pl.BlockSpec(memory_space=pltpu.VMEM)],
        out_specs=pl.BlockSpec(memory_space=pl.ANY),
        out_shape=jax.ShapeDtypeStruct(rolled.shape, rolled.dtype),
        scratch_shapes=[pltpu.SemaphoreType.DMA((n_layers * n_batch,))],
        input_output_aliases={0: 0},
        name="write_new_rows",
    )(rolled, new)


def _cast_kernel(x_ref, o_ref):
    o_ref[...] = x_ref[...].astype(o_ref.dtype)


def _cast_bf16(w, rows):
    n_layers, r, c = w.shape
    return pl.pallas_call(
        _cast_kernel,
        grid=(n_layers, r // rows),
        in_specs=[pl.BlockSpec((1, rows, c), lambda l, i: (l, i, 0))],
        out_specs=pl.BlockSpec((1, rows, c), lambda l, i: (l, i, 0)),
        out_shape=jax.ShapeDtypeStruct(w.shape, BF16),
        compiler_params=_params("parallel", "parallel"),
        name="cast_bf16",
    )(w)


def kernel(x_prompt, x_sample, state_pool, state_conv, cache_k, cache_v, norm_w, final_norm_w,
           w_in_ab, pool_lin, pool_scale, conv_w, w_out_ab, w_in_c, w_out_c):
    bp, t, d = x_prompt.shape
    bs, ts, _ = x_sample.shape
    ts_pad = -(-ts // SUBLANES) * SUBLANES
    n_buf = cache_k.shape[2]
    n_keep_p = min(MAX_WINDOW, t)

    hp = x_prompt.reshape(bp * t, d)
    hs = jnp.pad(x_sample, ((0, 0), (0, ts_pad - ts), (0, 0))).reshape(bs * ts_pad, d)
    zero_pool = jnp.zeros((bp, POOL_HIST, D_POOL), F32)
    zero_conv = jnp.zeros((bp, CONV_HIST, D_CONV), F32)

    tm_p, tn = 1024, 1024
    tm_s = bs * ts_pad
    cast_rows = 256

    w_in_ab = _cast_bf16(w_in_ab, cast_rows)
    w_out_ab = _cast_bf16(w_out_ab, cast_rows)
    w_in_c = _cast_bf16(w_in_c, cast_rows)
    w_out_c = _cast_bf16(w_out_c, cast_rows)
    pool_lin = _cast_bf16(pool_lin.reshape(pool_lin.shape[0], -1, D_POOL_GROUP), cast_rows).reshape(pool_lin.shape)

    def head_rows(a):
        return a.reshape(a.shape[0], a.shape[1] * N_HEADS, HEAD_DIM)

    assert DEPTH == 4
    assert min(MAX_WINDOW, n_buf + ts) == n_buf
    rolled = []
    kv_p = None
    fw = final_norm_w.reshape(1, d)
    pool_p, pool_s, conv_p, conv_s = [], [], [], []
    k_new, v_new = [], []
    for l in range(DEPTH):
        i = l // 2
        nw = norm_w[l].reshape(1, d)
        if l % 2 == 0:
            pscale = pool_scale[i].reshape(1, D_POOL)
            hp, pp, cp = _ab_layer(hp.reshape(bp, t, d), nw, w_in_ab, w_out_ab, i, zero_pool, zero_conv,
                                   pool_lin[i], pscale, conv_w[i], AB_ROW_TILE, 0, t)
            hp = hp.reshape(bp * t, d)
            proj_s = _norm_matmul(hs, nw, w_in_ab, i, tm_s, tn).reshape(bs, ts_pad, -1)
            mix_s, ps, cs = _ab_mix(proj_s, state_pool[i], state_conv[i], pool_lin[i], pscale, conv_w[i],
                                    ts_pad, PAST_LEN, ts, F32)
            hs = _matmul_residual(mix_s.reshape(bs * ts_pad, -1), w_out_ab, i, hs, tm_s, tn)
            pool_p.append(pp)
            pool_s.append(ps)
            conv_p.append(cp)
            conv_s.append(cs)
        else:
            proj_p = _norm_matmul(hp, nw, w_in_c, i, tm_p, tn).reshape(bp, t, -1)
            proj_s = _norm_matmul(hs, nw, w_in_c, i, tm_s, tn).reshape(bs, ts_pad, -1)[:, :ts]
            og_p, rolled_c, kv_p = _attn_prompt(proj_p, (cache_k, cache_v)[i], ts, i, DEPTH // 2, n_keep_p,
                                                kv_p, BF16)
            rolled.append(rolled_c)
            q_s, kn_s, vn_s, g_s = (head_rows(proj_s[..., c * D_ATTN:(c + 1) * D_ATTN]) for c in range(4))
            og_s = _attn_sample(q_s, g_s, kn_s, vn_s, cache_k, cache_v, i)
            og_s = jnp.pad(og_s.reshape(bs, ts, D_ATTN), ((0, 0), (0, ts_pad - ts), (0, 0)))
            if l == DEPTH - 1:
                hp = _matmul_residual_norm(og_p.reshape(bp * t, -1), w_out_c, i, hp, fw, 512)
                hs = _matmul_residual_norm(og_s.reshape(bs * ts_pad, -1), w_out_c, i, hs, fw, tm_s)
            else:
                hp = _matmul_residual(og_p.reshape(bp * t, -1), w_out_c, i, hp, tm_p, tn)
                hs = _matmul_residual(og_s.reshape(bs * ts_pad, -1), w_out_c, i, hs, tm_s, tn)
            k_new.append(kn_s.reshape(bs, ts, N_HEADS, HEAD_DIM))
            v_new.append(vn_s.reshape(bs, ts, N_HEADS, HEAD_DIM))

    k_s = _write_new_rows(rolled[0], jnp.stack(k_new))
    v_s = _write_new_rows(rolled[1], jnp.stack(v_new))
    k_p, v_p = kv_p

    y_prompt = hp.reshape(bp, t, d)
    y_sample = hs.reshape(bs, ts_pad, d)[:, :ts]
    return (y_prompt, y_sample, jnp.stack(pool_p), jnp.stack(pool_s), jnp.stack(conv_p), jnp.stack(conv_s),
            k_p, k_s, v_p, v_s)
```

```python
import functools

import jax
import jax.numpy as jnp
from jax import lax
from jax.experimental import pallas as pl
from jax.experimental.pallas import tpu as pltpu

F32 = jnp.float32
BF16 = jnp.bfloat16

D_MODEL = 2048
DEPTH = 4
PAST_LEN = 16384
D_POOL = D_MODEL // 2
POOL_WINDOWS = (2, 4, 8, 16)
D_POOL_GROUP = D_POOL // len(POOL_WINDOWS)
POOL_HIST = max(POOL_WINDOWS) - 1
D_CONV = D_MODEL // 2
CONV_WIDTH = 3
CONV_HIST = CONV_WIDTH - 1
HEAD_DIM = 128
N_HEADS = D_MODEL // HEAD_DIM
D_ATTN = N_HEADS * HEAD_DIM
DILATED_GROUPS = ((128, 1), (512, 4), (2048, 16))
MAX_WINDOW = max(w for w, _ in DILATED_GROUPS)
Q_BLOCK = 128
ATTN_SCALE = HEAD_DIM ** -0.5
LOG2_E = 1.4426950408889634
RMS_EPS = 1e-6
NEG_INF = -1e30

SUBLANES = 8
POOL_PAD = 16
CONV_PAD = 8
DEINT_CHUNK = 2 * Q_BLOCK
SAMPLE_DENSE_ROWS = 512
SAMPLE_PERIODS = 32
SAMPLE_RESIDUES = 4
ROLL_CHUNK = 128
ROLL_RING = 8
ROLL_DMA_PRIORITY = 1
SCORE_BLOCKS_PER_TRIP = 16
VALUE_BLOCKS_PER_TRIP = 16
MERGE_BLOCKS_PER_TRIP = 8
VMEM_LIMIT = 48 * 1024 * 1024
AB_LAYER_VMEM_LIMIT = 56 * 1024 * 1024
ATTN_VMEM_LIMIT = 57 * 1024 * 1024
AB_ROW_TILE = 256
AB_OUT_CHUNK = 512
AB_PROJ_CHUNK = 1024


def _silu(x):
    return x * jax.nn.sigmoid(x)


def _params(*semantics):
    return pltpu.CompilerParams(dimension_semantics=semantics, vmem_limit_bytes=VMEM_LIMIT)


def _norm_matmul_kernel(x_ref, nw_ref, w_ref, o_ref, xn_ref):
    @pl.when(pl.program_id(1) == 0)
    def _():
        x = x_ref[...]
        ms = jnp.mean(x * x, axis=-1, keepdims=True)
        xn_ref[...] = (x * lax.rsqrt(ms + RMS_EPS) * nw_ref[...]).astype(BF16)

    o_ref[...] = jnp.dot(xn_ref[...], w_ref[...], preferred_element_type=F32)


def _norm_matmul(x, nw, w, layer, tm, tn):
    m, d = x.shape
    n = w.shape[2]
    return pl.pallas_call(
        _norm_matmul_kernel,
        grid=(m // tm, n // tn),
        in_specs=[
            pl.BlockSpec((tm, d), lambda i, j: (i, 0)),
            pl.BlockSpec((1, d), lambda i, j: (0, 0)),
            pl.BlockSpec((None, d, tn), lambda i, j: (layer, 0, j)),
        ],
        out_specs=pl.BlockSpec((tm, tn), lambda i, j: (i, j)),
        out_shape=jax.ShapeDtypeStruct((m, n), F32),
        scratch_shapes=[pltpu.VMEM((tm, d), BF16)],
        compiler_params=_params("parallel", "arbitrary"),
        name="norm_matmul",
    )(x, nw, w)


def _matmul_residual_kernel(a_ref, w_ref, h_ref, o_ref):
    o_ref[...] = h_ref[...] + jnp.dot(a_ref[...].astype(BF16), w_ref[...], preferred_element_type=F32)


def _matmul_residual(a, w, layer, h, tm, tn):
    m, k = a.shape
    n = w.shape[2]
    return pl.pallas_call(
        _matmul_residual_kernel,
        grid=(m // tm, n // tn),
        in_specs=[
            pl.BlockSpec((tm, k), lambda i, j: (i, 0)),
            pl.BlockSpec((None, k, tn), lambda i, j: (layer, 0, j)),
            pl.BlockSpec((tm, tn), lambda i, j: (i, j)),
        ],
        out_specs=pl.BlockSpec((tm, tn), lambda i, j: (i, j)),
        out_shape=jax.ShapeDtypeStruct((m, n), F32),
        compiler_params=_params("parallel", "arbitrary"),
        name="matmul_residual",
    )(a, w, h)


def _matmul_residual_norm_kernel(a_ref, w_ref, h_ref, nw_ref, o_ref):
    y = h_ref[...] + jnp.dot(a_ref[...].astype(BF16), w_ref[...], preferred_element_type=F32)
    ms = jnp.mean(y * y, axis=-1, keepdims=True)
    o_ref[...] = y * lax.rsqrt(ms + RMS_EPS) * nw_ref[...]


def _matmul_residual_norm(a, w, layer, h, nw, tm):
    m, k = a.shape
    n = w.shape[2]
    return pl.pallas_call(
        _matmul_residual_norm_kernel,
        grid=(m // tm,),
        in_specs=[
            pl.BlockSpec((tm, k), lambda i: (i, 0)),
            pl.BlockSpec((None, k, n), lambda i: (layer, 0, 0)),
            pl.BlockSpec((tm, n), lambda i: (i, 0)),
            pl.BlockSpec((1, n), lambda i: (0, 0)),
        ],
        out_specs=pl.BlockSpec((tm, n), lambda i: (i, 0)),
        out_shape=jax.ShapeDtypeStruct((m, n), F32),
        compiler_params=_params("parallel"),
        name="matmul_residual_norm",
    )(a, w, h, nw)


def _mix_history(ph_ref, ch_ref, uext_ref, zext_ref, tm):
    j = pl.program_id(1)

    @pl.when(j == 0)
    def _():
        uext_ref[POOL_PAD - POOL_HIST:POOL_PAD, :] = ph_ref[0]
        zext_ref[CONV_PAD - CONV_HIST:CONV_PAD, :] = ch_ref[0]

    @pl.when(j > 0)
    def _():
        uext_ref[0:POOL_PAD, :] = uext_ref[tm:tm + POOL_PAD, :]
        zext_ref[0:CONV_PAD, :] = zext_ref[tm:tm + CONV_PAD, :]


def _mix_tails(ptail_ref, ctail_ref, uext_ref, zext_ref, t_last):
    @pl.when(pl.program_id(1) == pl.num_programs(1) - 1)
    def _():
        ptail_ref[0] = uext_ref[POOL_PAD + t_last - POOL_HIST:POOL_PAD + t_last, :]
        ctail_ref[0] = zext_ref[CONV_PAD + t_last - CONV_HIST:CONV_PAD + t_last, :]


def _mix_body(proj_ref, plin_ref, pscale_ref, cw_ref, mix_ref, uext_ref, zext_ref, pos0):
    j = pl.program_id(1)
    tm = proj_ref.shape[1]
    gw = D_POOL_GROUP

    uext_ref[POOL_PAD:POOL_PAD + tm, :] = proj_ref[0, :, 0:D_POOL]
    c_off = 2 * D_POOL + D_CONV
    zext_ref[CONV_PAD:CONV_PAD + tm, :] = (
        proj_ref[0, :, c_off:c_off + D_CONV] * proj_ref[0, :, c_off + D_CONV:c_off + 2 * D_CONV])

    pos = (pos0 + j * tm + lax.broadcasted_iota(jnp.int32, (tm, 1), 0)).astype(F32)

    for g, k in enumerate(POOL_WINDOWS):
        c0 = g * gw
        u_g = uext_ref[POOL_PAD:POOL_PAD + tm, c0:c0 + gw]
        s = u_g
        for i in range(1, k):
            s = s + uext_ref[POOL_PAD - i:POOL_PAD - i + tm, c0:c0 + gw]
        cnt = jnp.minimum(float(k), pos + 1.0)
        pooled = s / cnt - u_g
        a = jnp.dot(pooled.astype(BF16), plin_ref[g], preferred_element_type=F32)
        a = a * pscale_ref[:, c0:c0 + gw]
        gate = proj_ref[0, :, D_POOL + c0:D_POOL + c0 + gw]
        mix_ref[0, :, c0:c0 + gw] = (a * _silu(gate)).astype(mix_ref.dtype)

    for c in range(D_CONV // gw):
        c0 = c * gw
        conv = zext_ref[CONV_PAD - 2:CONV_PAD - 2 + tm, c0:c0 + gw] * cw_ref[0:1, c0:c0 + gw]
        conv = conv + zext_ref[CONV_PAD - 1:CONV_PAD - 1 + tm, c0:c0 + gw] * cw_ref[1:2, c0:c0 + gw]
        conv = conv + zext_ref[CONV_PAD:CONV_PAD + tm, c0:c0 + gw] * cw_ref[2:3, c0:c0 + gw]
        b_gate = proj_ref[0, :, 2 * D_POOL + c0:2 * D_POOL + c0 + gw]
        gate = proj_ref[0, :, 2 * D_POOL + 3 * D_CONV + c0:2 * D_POOL + 3 * D_CONV + c0 + gw]
        mix_ref[0, :, D_POOL + c0:D_POOL + c0 + gw] = (b_gate * conv * _silu(gate)).astype(mix_ref.dtype)


def _ab_mix_kernel(proj_ref, ph_ref, ch_ref, plin_ref, pscale_ref, cw_ref,
                   mix_ref, ptail_ref, ctail_ref, uext_ref, zext_ref, *, pos0, t_last):
    _mix_history(ph_ref, ch_ref, uext_ref, zext_ref, proj_ref.shape[1])
    _mix_body(proj_ref, plin_ref, pscale_ref, cw_ref, mix_ref, uext_ref, zext_ref, pos0)
    _mix_tails(ptail_ref, ctail_ref, uext_ref, zext_ref, t_last)


def _ab_mix(proj, pool_hist, conv_hist, plin, pscale, cw, tm, pos0, t_valid, mix_dtype):
    b, t, n = proj.shape
    nj = t // tm
    t_last = t_valid - (nj - 1) * tm
    kern = functools.partial(_ab_mix_kernel, pos0=pos0, t_last=t_last)
    return pl.pallas_call(
        kern,
        grid=(b, nj),
        in_specs=[
            pl.BlockSpec((1, tm, n), lambda i, j: (i, j, 0)),
            pl.BlockSpec((1, POOL_HIST, D_POOL), lambda i, j: (i, 0, 0)),
            pl.BlockSpec((1, CONV_HIST, D_CONV), lambda i, j: (i, 0, 0)),
            pl.BlockSpec(plin.shape, lambda i, j: (0, 0, 0)),
            pl.BlockSpec((1, D_POOL), lambda i, j: (0, 0)),
            pl.BlockSpec((CONV_WIDTH, D_CONV), lambda i, j: (0, 0)),
        ],
        out_specs=[
            pl.BlockSpec((1, tm, D_POOL + D_CONV), lambda i, j: (i, j, 0)),
            pl.BlockSpec((1, POOL_HIST, D_POOL), lambda i, j: (i, 0, 0)),
            pl.BlockSpec((1, CONV_HIST, D_CONV), lambda i, j: (i, 0, 0)),
        ],
        out_shape=[
            jax.ShapeDtypeStruct((b, t, D_POOL + D_CONV), mix_dtype),
            jax.ShapeDtypeStruct((b, POOL_HIST, D_POOL), F32),
            jax.ShapeDtypeStruct((b, CONV_HIST, D_CONV), F32),
        ],
        scratch_shapes=[pltpu.VMEM((tm + POOL_PAD, D_POOL), F32), pltpu.VMEM((tm + CONV_PAD, D_CONV), F32)],
        compiler_params=_params("parallel", "arbitrary"),
        name="ab_mix",
    )(proj, pool_hist, conv_hist, plin, pscale, cw)


def _ab_layer_kernel(x_ref, nw_ref, win_ref, ph_ref, ch_ref, plin_ref, pscale_ref, cw_ref, wout_ref,
                     o_ref, ptail_ref, ctail_ref, proj_ref, mix_ref, uext_ref, zext_ref, *, pos0, t_last):
    _mix_history(ph_ref, ch_ref, uext_ref, zext_ref, x_ref.shape[1])
    x = x_ref[0]
    ms = jnp.mean(x * x, axis=-1, keepdims=True)
    xn = (x * lax.rsqrt(ms + RMS_EPS) * nw_ref[...]).astype(BF16)
    n_in = win_ref.shape[1]
    for c0 in range(0, n_in, AB_PROJ_CHUNK):
        proj_ref[0, :, c0:c0 + AB_PROJ_CHUNK] = jnp.dot(xn, win_ref[:, c0:c0 + AB_PROJ_CHUNK],
                                                        preferred_element_type=F32)
    _mix_body(proj_ref, plin_ref, pscale_ref, cw_ref, mix_ref, uext_ref, zext_ref, pos0)
    y = x
    for c0 in range(0, wout_ref.shape[0], AB_OUT_CHUNK):
        y = y + jnp.dot(mix_ref[0, :, c0:c0 + AB_OUT_CHUNK].astype(BF16), wout_ref[c0:c0 + AB_OUT_CHUNK, :],
                        preferred_element_type=F32)
    o_ref[0] = y
    _mix_tails(ptail_ref, ctail_ref, uext_ref, zext_ref, t_last)


def _ab_layer(x, nw, w_in, w_out, layer, pool_hist, conv_hist, plin, pscale, cw, tm, pos0, t_valid):
    b, t, d = x.shape
    n_in = w_in.shape[2]
    nj = t // tm
    t_last = t_valid - (nj - 1) * tm
    resident = pl.Buffered(1)
    return pl.pallas_call(
        functools.partial(_ab_layer_kernel, pos0=pos0, t_last=t_last),
        grid=(b, nj),
        in_specs=[
            pl.BlockSpec((1, tm, d), lambda i, j: (i, j, 0)),
            pl.BlockSpec((1, d), lambda i, j: (0, 0)),
            pl.BlockSpec((None, d, n_in), lambda i, j: (layer, 0, 0), pipeline_mode=resident),
            pl.BlockSpec((1, POOL_HIST, D_POOL), lambda i, j: (i, 0, 0)),
            pl.BlockSpec((1, CONV_HIST, D_CONV), lambda i, j: (i, 0, 0)),
            pl.BlockSpec(plin.shape, lambda i, j: (0, 0, 0)),
            pl.BlockSpec((1, D_POOL), lambda i, j: (0, 0)),
            pl.BlockSpec((CONV_WIDTH, D_CONV), lambda i, j: (0, 0)),
            pl.BlockSpec((None, D_POOL + D_CONV, d), lambda i, j: (layer, 0, 0), pipeline_mode=resident),
        ],
        out_specs=[
            pl.BlockSpec((1, tm, d), lambda i, j: (i, j, 0)),
            pl.BlockSpec((1, POOL_HIST, D_POOL), lambda i, j: (i, 0, 0)),
            pl.BlockSpec((1, CONV_HIST, D_CONV), lambda i, j: (i, 0, 0)),
        ],
        out_shape=[
            jax.ShapeDtypeStruct((b, t, d), F32),
            jax.ShapeDtypeStruct((b, POOL_HIST, D_POOL), F32),
            jax.ShapeDtypeStruct((b, CONV_HIST, D_CONV), F32),
        ],
        scratch_shapes=[
            pltpu.VMEM((1, tm, n_in), F32),
            pltpu.VMEM((1, tm, D_POOL + D_CONV), BF16 if tm % (2 * SUBLANES) == 0 else F32),
            pltpu.VMEM((tm + POOL_PAD, D_POOL), F32),
            pltpu.VMEM((tm + CONV_PAD, D_CONV), F32),
        ],
        compiler_params=pltpu.CompilerParams(dimension_semantics=("parallel", "arbitrary"),
                                             vmem_limit_bytes=AB_LAYER_VMEM_LIMIT),
        name="ab_layer",
    )(x, nw, w_in, pool_hist, conv_hist, plin, pscale, cw, w_out)


def _rows(start, size, stride):
    if stride == 1:
        return pl.ds(start, size)
    return pl.ds(start, size, stride=stride)


def _attn_prompt_kernel(*refs, n_new, kv_layers, n_aliased):
    q_ref, k_blk_ref, v_blk_ref, g_ref, cache_ref = refs[:5]
    (o_ref, roll_ref, kp_ref, vp_ref, kd_ref, vd_ref, bias_ref, s_ref, ring_ref, sem_ref, kv_sem_ref,
     *state_refs) = refs[5 + n_aliased:]
    k_ref, v_ref = k_blk_ref.at[0], v_blk_ref.at[0]
    t = q_ref.shape[0]
    n_keep = kp_ref.shape[2]

    def kv_copies():
        copies = []
        for n, layer in enumerate(kv_layers):
            for m, (src, dst) in enumerate(((k_blk_ref, kp_ref), (v_blk_ref, vp_ref))):
                copies.append(pltpu.make_async_copy(
                    src.at[0, pl.ds(t - n_keep, n_keep), :],
                    dst.at[layer, pl.program_id(0), :, pl.program_id(1), :], kv_sem_ref.at[m, n]))
        return copies

    for copy in kv_copies():
        copy.start(priority=ROLL_DMA_PRIORITY)
    n_cb, n_buf = cache_ref.shape[1], cache_ref.shape[2]
    step = pl.program_id(0) * pl.num_programs(1) + pl.program_id(1)
    slab, slab_half = step // 2, step % 2
    lyr, cb = slab // n_cb, slab % n_cb
    dst0 = slab_half * (n_buf // 2 - n_new)
    n_ring = ring_ref.shape[0]

    def roll_read(c):
        return pltpu.make_async_copy(cache_ref.at[lyr, cb, pl.ds(dst0 + c * ROLL_CHUNK + n_new, ROLL_CHUNK)],
                                     ring_ref.at[c % n_ring], sem_ref.at[0, c % n_ring])

    def roll_write(c):
        return pltpu.make_async_copy(ring_ref.at[c % n_ring],
                                     roll_ref.at[lyr, cb, pl.ds(dst0 + c * ROLL_CHUNK, ROLL_CHUNK)],
                                     sem_ref.at[1, c % n_ring])

    def roll_tail():
        return pltpu.make_async_copy(ring_ref.at[0, pl.ds(0, n_new)],
                                     roll_ref.at[lyr, cb, pl.ds(n_buf - n_new, n_new)], sem_ref.at[1, 0])

    for c in range(n_ring):
        roll_read(c).start(priority=ROLL_DMA_PRIORITY)
    n_groups = len(DILATED_GROUPS)
    qf_ref, kf_ref, vf_ref = state_refs[0:3]
    acc_refs = (None,) + tuple(state_refs[3:3 + n_groups - 1])
    stat_refs = (None,) + tuple(state_refs[3 + n_groups - 1:])
    mid_dil = DILATED_GROUPS[n_groups - 2][1]
    nt_dims = (((1,), (1,)), ((), ()))
    half = HEAD_DIM // 2
    lane = lax.broadcasted_iota(jnp.int32, (Q_BLOCK, HEAD_DIM), 1)

    @pl.when((pl.program_id(0) == 0) & (pl.program_id(1) == 0))
    def _():
        delta = (lax.broadcasted_iota(jnp.int32, (Q_BLOCK, 2 * Q_BLOCK), 0)
                 - lax.broadcasted_iota(jnp.int32, (Q_BLOCK, 2 * Q_BLOCK), 1))
        for sel in range(2):
            dist = delta + sel * Q_BLOCK
            bias_ref[sel] = jnp.where((dist >= 0) & (dist <= Q_BLOCK), 0.0, NEG_INF)
        vd_ref[:, :, HEAD_DIM:] = jnp.ones((n_groups, t, HEAD_DIM), BF16)

    def deinterleave(gi, d, d_src, ksrc_ref, vsrc_ref, keep_f32):
        stream_len = t // d
        ratio = d // d_src

        def body(c, carry):
            dst = pl.multiple_of(c * DEINT_CHUNK, DEINT_CHUNK)
            r = dst // stream_len
            m0 = dst % stream_len
            src = _rows((r % d_src) * (t // d_src) + r // d_src + ratio * m0, DEINT_CHUNK, ratio)
            k = ksrc_ref[src, :]
            v = vsrc_ref[src, :]
            kd_ref[gi, pl.ds(dst, DEINT_CHUNK), :] = k.astype(BF16)
            vd_ref[gi, pl.ds(dst, DEINT_CHUNK), 0:HEAD_DIM] = v.astype(BF16)
            if keep_f32:
                qf_ref[pl.ds(dst, DEINT_CHUNK), :] = q_ref[src, :]
                kf_ref[pl.ds(dst, DEINT_CHUNK), :] = k
                vf_ref[pl.ds(dst, DEINT_CHUNK), :] = v
            return carry

        lax.fori_loop(0, t // DEINT_CHUNK, body, 0, unroll=2)

    def merged_output(rows, m0, acc0, l0):
        ms, ls, accs = [m0], [l0], [acc0]
        for gi in range(1, n_groups):
            st = stat_refs[gi][rows, :]
            sw = pltpu.roll(st, half, axis=1)
            ms.append(jnp.where(lane < half, st, sw))
            ls.append(jnp.where(lane < half, sw, st))
            accs.append(acc_refs[gi][rows, :])
        m_all = functools.reduce(jnp.maximum, ms)
        num = None
        den = None
        for m_g, l_g, acc_g in zip(ms, ls, accs):
            w = jnp.exp2(m_g - m_all)
            num = w * acc_g if num is None else num + w * acc_g
            den = w * l_g if den is None else den + w * l_g
        g = g_ref[rows, :]
        return ((num * g) / (den * (1.0 + jnp.exp(-g)))).astype(o_ref.dtype)

    def run_group(gi, d, after_scores, after_values):
        stream_len = t // d
        nb = stream_len // Q_BLOCK

        def block_rows(idx):
            r = idx // nb
            mb = idx % nb
            kb = jnp.maximum(mb - 1, 0)
            q_start = r + d * Q_BLOCK * mb
            q_rows = _rows(pl.multiple_of(q_start, Q_BLOCK) if d == 1 else q_start, Q_BLOCK, d)
            k_rows = pl.ds(pl.multiple_of(r * stream_len + kb * Q_BLOCK, Q_BLOCK), 2 * Q_BLOCK)
            return q_rows, k_rows, mb - kb

        def load_q(idx, q_rows):
            if gi < n_groups - 1:
                return q_ref[q_rows, :]
            r = idx // nb
            mb = idx % nb
            ratio = d // mid_dil
            start = (r % mid_dil) * (t // mid_dil) + r // mid_dil + ratio * Q_BLOCK * mb
            return qf_ref[_rows(start, Q_BLOCK, ratio), :]

        def scores(it, carry):
            s = []
            for u in range(SCORE_BLOCKS_PER_TRIP):
                idx = it * SCORE_BLOCKS_PER_TRIP + u
                q_rows, k_rows, sel = block_rows(idx)
                q = (load_q(idx, q_rows) * (ATTN_SCALE * LOG2_E)).astype(BF16)
                s.append(lax.dot_general(q, kd_ref[gi, k_rows, :], nt_dims, preferred_element_type=F32)
                         + bias_ref[sel])
            for u in range(SCORE_BLOCKS_PER_TRIP):
                s_ref[it * SCORE_BLOCKS_PER_TRIP + u] = s[u]
            return carry

        n_val = MERGE_BLOCKS_PER_TRIP if gi == 0 else VALUE_BLOCKS_PER_TRIP

        def values(it, carry):
            rows, s, v = [], [], []
            for u in range(n_val):
                q_rows, k_rows, _ = block_rows(it * n_val + u)
                rows.append(q_rows)
                s.append(s_ref[it * n_val + u])
                v.append(vd_ref[gi, k_rows, :])
            outs = []
            for u in range(n_val):
                m_blk = jnp.max(s[u], axis=-1, keepdims=True)
                p = jnp.exp2(s[u] - m_blk).astype(BF16)
                pv = jnp.dot(p, v[u], preferred_element_type=F32)
                acc, l_blk = pv[:, 0:HEAD_DIM], pv[:, HEAD_DIM:]
                if gi == 0:
                    outs.append((merged_output(rows[u], jnp.broadcast_to(m_blk, acc.shape), acc, l_blk),))
                else:
                    outs.append((acc, jnp.where(lane < half, m_blk, l_blk)))
            for u in range(n_val):
                if gi == 0:
                    o_ref[rows[u], :] = outs[u][0]
                else:
                    acc_refs[gi][rows[u], :] = outs[u][0]
                    stat_refs[gi][rows[u], :] = outs[u][1]
            return carry

        lax.fori_loop(0, d * nb // SCORE_BLOCKS_PER_TRIP, scores, 0)
        after_scores()
        lax.fori_loop(0, d * nb // n_val, values, 0)
        after_values()

    ksrc_ref, vsrc_ref, d_src = k_ref, v_ref, 1
    for gi, (_, d) in enumerate(DILATED_GROUPS):
        keep_f32 = 0 < gi < n_groups - 1
        deinterleave(gi, d, d_src, ksrc_ref, vsrc_ref, keep_f32)
        if keep_f32:
            ksrc_ref, vsrc_ref, d_src = kf_ref, vf_ref, d
    def forward():
        for c in range(n_ring):
            roll_read(c).wait()
            roll_write(c).start(priority=ROLL_DMA_PRIORITY)

    def drain():
        for c in range(n_ring):
            roll_write(c).wait()
        for copy in kv_copies():
            copy.wait()

        @pl.when(slab_half == 1)
        def _():
            roll_tail().start()
            roll_tail().wait()

    def nothing():
        pass

    hooks = [(nothing, forward), (nothing, nothing), (nothing, drain)]
    for (after_scores, after_values), gi in zip(hooks, reversed(range(n_groups))):
        run_group(gi, DILATED_GROUPS[gi][1], after_scores, after_values)


def _attn_prompt(proj, cache, n_new, kv_layer, n_kv_layers, n_keep, kv_prev, out_dtype):
    b, t, _ = proj.shape
    first = kv_prev is None
    assert first == (kv_layer == 0)
    kv_layers = tuple(range(n_kv_layers)) if first else (kv_layer,)
    kv_shape = jax.ShapeDtypeStruct((n_kv_layers, b, n_keep, N_HEADS, HEAD_DIM), F32)
    any_spec = pl.BlockSpec(memory_space=pl.ANY)
    aliased = () if first else tuple(kv_prev)
    n_groups = len(DILATED_GROUPS)
    n_layers, n_cb, n_buf = cache.shape[:3]
    assert b * N_HEADS == 2 * n_layers * n_cb and n_buf // 2 == ROLL_RING * ROLL_CHUNK
    dils = [dil for _, dil in DILATED_GROUPS]
    assert n_groups == 3 and dils[0] == 1 and all(hi % lo == 0 for lo, hi in zip(dils, dils[1:]))
    for window, dil in DILATED_GROUPS:
        assert window // dil == Q_BLOCK and (t // dil) % DEINT_CHUNK == 0
        assert (t // Q_BLOCK) % SCORE_BLOCKS_PER_TRIP == 0 and (t // Q_BLOCK) % VALUE_BLOCKS_PER_TRIP == 0
    blk = (None, t, HEAD_DIM)
    og, rolled, kp, vp = pl.pallas_call(
        functools.partial(_attn_prompt_kernel, n_new=n_new, kv_layers=kv_layers, n_aliased=len(aliased)),
        grid=(b, N_HEADS),
        in_specs=[
            pl.BlockSpec(blk, lambda i, h: (i, 0, h)),
            pl.BlockSpec((1, t, HEAD_DIM), lambda i, h: (i, 0, N_HEADS + h)),
            pl.BlockSpec((1, t, HEAD_DIM), lambda i, h: (i, 0, 2 * N_HEADS + h)),
            pl.BlockSpec(blk, lambda i, h: (i, 0, 3 * N_HEADS + h)),
            any_spec,
        ] + [any_spec] * len(aliased),
        out_specs=[pl.BlockSpec(blk, lambda i, h: (i, 0, h)), any_spec, any_spec, any_spec],
        out_shape=[jax.ShapeDtypeStruct((b, t, D_ATTN), out_dtype),
                   jax.ShapeDtypeStruct(cache.shape, cache.dtype), kv_shape, kv_shape],
        scratch_shapes=(
            [pltpu.VMEM((n_groups, t, HEAD_DIM), BF16), pltpu.VMEM((n_groups, t, 2 * HEAD_DIM), BF16)]
            + [pltpu.VMEM((2, Q_BLOCK, 2 * Q_BLOCK), F32)]
            + [pltpu.VMEM((t // Q_BLOCK, Q_BLOCK, 2 * Q_BLOCK), F32)]
            + [pltpu.VMEM((ROLL_RING, ROLL_CHUNK) + cache.shape[3:], cache.dtype)]
            + [pltpu.SemaphoreType.DMA((2, ROLL_RING))]
            + [pltpu.SemaphoreType.DMA((2, len(kv_layers)))]
            + [pltpu.VMEM((t, HEAD_DIM), F32)] * (3 + 2 * (n_groups - 1))),
        input_output_aliases={5 + n: 2 + n for n in range(len(aliased))},
        compiler_params=pltpu.CompilerParams(dimension_semantics=("arbitrary", "arbitrary"),
                                             vmem_limit_bytes=ATTN_VMEM_LIMIT),
        name="attn_prompt",
    )(proj, proj, proj, proj, cache, *aliased)
    return og, rolled, (kp, vp)


def _group_count(dist):
    cnt = jnp.zeros(dist.shape, F32)
    for window, dil in DILATED_GROUPS:
        hit = (dist >= 0) & (dist <= window) & ((dist & (dil - 1)) == 0)
        cnt = cnt + hit.astype(F32)
    return cnt


def _attn_sample_kernel(q_ref, g_ref, kn_ref, vn_ref, ks_ref, vs_ref, kd_ref, vd_ref, o_ref,
                        m_ref, l_ref, acc_ref, *, n_buf):
    j = pl.program_id(1)
    nj = pl.num_programs(1)
    n_rows = q_ref.shape[0]
    nt_dims = (((1,), (1,)), ((), ()))
    head_bits = N_HEADS.bit_length() - 1
    dil = DILATED_GROUPS[-1][1]

    @pl.when(j == 0)
    def _():
        m_ref[...] = jnp.full(m_ref.shape, NEG_INF, F32)
        l_ref[...] = jnp.zeros(l_ref.shape, F32)
        acc_ref[...] = jnp.zeros(acc_ref.shape, F32)

    q = (q_ref[...] * ATTN_SCALE).astype(BF16)

    def accumulate(k2, v2, key_pos):
        n_cols = k2.shape[0]
        row = lax.broadcasted_iota(jnp.int32, (n_rows, n_cols), 0)
        col = lax.broadcasted_iota(jnp.int32, (n_rows, n_cols), 1)
        same_head = (row & (N_HEADS - 1)) == (col & (N_HEADS - 1))
        dist = (n_buf + (row >> head_bits)) - key_pos(col)
        cnt = jnp.where(same_head, _group_count(dist), 0.0)
        s = lax.dot_general(q, k2.astype(BF16), nt_dims, preferred_element_type=F32)
        s = jnp.where(cnt > 0, s, NEG_INF)
        m_prev = m_ref[...]
        m_new = jnp.maximum(m_prev, jnp.max(s, axis=-1, keepdims=True))
        alpha = jnp.exp(m_prev - m_new)
        p = cnt * jnp.exp(s - m_new)
        l_ref[...] = alpha * l_ref[...] + jnp.sum(p, axis=-1, keepdims=True)
        acc_ref[...] = alpha * acc_ref[...] + jnp.dot(p.astype(BF16), v2.astype(BF16),
                                                      preferred_element_type=F32)
        m_ref[...] = m_new

    @pl.when(j < nj - 1)
    def _():
        periods, residues = ks_ref.shape[0], ks_ref.shape[1]
        n_cols = periods * residues * N_HEADS
        res_bits = residues.bit_length() - 1

        def key_pos(col):
            period = j * periods + (col >> (head_bits + res_bits))
            return period * dil + ((col >> head_bits) & (residues - 1))

        accumulate(ks_ref[...].reshape(n_cols, HEAD_DIM), vs_ref[...].reshape(n_cols, HEAD_DIM), key_pos)

    @pl.when(j == nj - 1)
    def _():
        n_dense = kd_ref.shape[0]
        accumulate(kd_ref[...].reshape(n_dense * N_HEADS, HEAD_DIM),
                   vd_ref[...].reshape(n_dense * N_HEADS, HEAD_DIM),
                   lambda col: (n_buf - n_dense) + (col >> head_bits))
        accumulate(kn_ref[...], vn_ref[...], lambda col: n_buf + (col >> head_bits))
        o_ref[...] = (acc_ref[...] / l_ref[...]) * _silu(g_ref[...])


def _attn_sample(q2, g2, kn2, vn2, cache_k, cache_v, layer):
    b, n_rows, _ = q2.shape
    n_layers, _, n_buf = cache_k.shape[:3]
    n_new = n_rows // N_HEADS
    dil = DILATED_GROUPS[-1][1]
    n_dense = SAMPLE_DENSE_ROWS
    assert all(w <= n_dense for w, _ in DILATED_GROUPS[:-1]) and n_buf % dil == 0 and n_dense % dil == 0
    assert n_new <= SAMPLE_RESIDUES and dil % SAMPLE_RESIDUES == 0 and n_buf % n_dense == 0
    n_sparse_steps = (n_buf - n_dense) // dil // SAMPLE_PERIODS
    assert n_sparse_steps * SAMPLE_PERIODS * dil == n_buf - n_dense and n_sparse_steps >= 1
    by_period = (n_layers, cache_k.shape[1], n_buf // dil, dil, N_HEADS, HEAD_DIM)
    rblk = (None, n_rows, HEAD_DIM)
    sparse = pl.BlockSpec((None, None, SAMPLE_PERIODS, SAMPLE_RESIDUES, N_HEADS, HEAD_DIM),
                          lambda i, j: (layer, i, jnp.minimum(j, n_sparse_steps - 1), 0, 0, 0))
    dense = pl.BlockSpec((None, None, n_dense, N_HEADS, HEAD_DIM),
                         lambda i, j: (layer, i, n_buf // n_dense - 1, 0, 0))
    return pl.pallas_call(
        functools.partial(_attn_sample_kernel, n_buf=n_buf),
        grid=(b, n_sparse_steps + 1),
        in_specs=[pl.BlockSpec(rblk, lambda i, j: (i, 0, 0))] * 4 + [sparse, sparse, dense, dense],
        out_specs=pl.BlockSpec(rblk, lambda i, j: (i, 0, 0)),
        out_shape=jax.ShapeDtypeStruct((b, n_rows, HEAD_DIM), F32),
        scratch_shapes=[pltpu.VMEM((n_rows, 1), F32), pltpu.VMEM((n_rows, 1), F32),
                        pltpu.VMEM((n_rows, HEAD_DIM), F32)],
        compiler_params=_params("parallel", "arbitrary"),
        name="attn_sample",
    )(q2, g2, kn2, vn2, cache_k.reshape(by_period), cache_v.reshape(by_period), cache_k, cache_v)


def _write_new_rows_kernel(rolled_ref, new_ref, out_ref, sem):
    del rolled_ref
    n_layers, n_batch, n_new = new_ref.shape[:3]
    n_buf = out_ref.shape[2]
    copies = [pltpu.make_async_copy(new_ref.at[l, b], out_ref.at[l, b, pl.ds(n_buf - n_new, n_new)],
                                    sem.at[l * n_batch + b])
              for l in range(n_layers) for b in range(n_batch)]
    for copy in copies:
        copy.start()
    for copy in copies:
        copy.wait()


def _write_new_rows(rolled, new):
    n_layers, n_batch = new.shape[:2]
    return pl.pallas_call(
        _write_new_rows_kernel,
        in_specs=[pl.BlockSpec(memory_space=pl.ANY), pl.BlockSpec(memory_space=pltpu.VMEM)],
        out_specs=pl.BlockSpec(memory_space=pl.ANY),
        out_shape=jax.ShapeDtypeStruct(rolled.shape, rolled.dtype),
        scratch_shapes=[pltpu.SemaphoreType.DMA((n_layers * n_batch,))],
        input_output_aliases={0: 0},
        name="write_new_rows",
    )(rolled, new)


def _cast_kernel(x_ref, o_ref):
    o_ref[...] = x_ref[...].astype(o_ref.dtype)


def _cast_bf16(w, rows):
    n_layers, r, c = w.shape
    return pl.pallas_call(
        _cast_kernel,
        grid=(n_layers, r // rows),
        in_specs=[pl.BlockSpec((1, rows, c), lambda l, i: (l, i, 0))],
        out_specs=pl.BlockSpec((1, rows, c), lambda l, i: (l, i, 0)),
        out_shape=jax.ShapeDtypeStruct(w.shape, BF16),
        compiler_params=_params("parallel", "parallel"),
        name="cast_bf16",
    )(w)


def kernel(x_prompt, x_sample, state_pool, state_conv, cache_k, cache_v, norm_w, final_norm_w,
           w_in_ab, pool_lin, pool_scale, conv_w, w_out_ab, w_in_c, w_out_c):
    bp, t, d = x_prompt.shape
    bs, ts, _ = x_sample.shape
    ts_pad = -(-ts // SUBLANES) * SUBLANES
    n_buf = cache_k.shape[2]
    n_keep_p = min(MAX_WINDOW, t)

    hp = x_prompt.reshape(bp * t, d)
    hs = jnp.pad(x_sample, ((0, 0), (0, ts_pad - ts), (0, 0))).reshape(bs * ts_pad, d)
    zero_pool = jnp.zeros((bp, POOL_HIST, D_POOL), F32)
    zero_conv = jnp.zeros((bp, CONV_HIST, D_CONV), F32)

    tm_p, tn = 1024, 1024
    tm_s = bs * ts_pad
    cast_rows = 256

    w_in_ab = _cast_bf16(w_in_ab, cast_rows)
    w_out_ab = _cast_bf16(w_out_ab, cast_rows)
    w_in_c = _cast_bf16(w_in_c, cast_rows)
    w_out_c = _cast_bf16(w_out_c, cast_rows)
    pool_lin = _cast_bf16(pool_lin.reshape(pool_lin.shape[0], -1, D_POOL_GROUP), cast_rows).reshape(pool_lin.shape)

    def head_rows(a):
        return a.reshape(a.shape[0], a.shape[1] * N_HEADS, HEAD_DIM)

    assert DEPTH == 4
    assert min(MAX_WINDOW, n_buf + ts) == n_buf
    rolled = []
    kv_p = None
    fw = final_norm_w.reshape(1, d)
    pool_p, pool_s, conv_p, conv_s = [], [], [], []
    k_new, v_new = [], []
    for l in range(DEPTH):
        i = l // 2
        nw = norm_w[l].reshape(1, d)
        if l % 2 == 0:
            pscale = pool_scale[i].reshape(1, D_POOL)
            hp, pp, cp = _ab_layer(hp.reshape(bp, t, d), nw, w_in_ab, w_out_ab, i, zero_pool, zero_conv,
                                   pool_lin[i], pscale, conv_w[i], AB_ROW_TILE, 0, t)
            hp = hp.reshape(bp * t, d)
            proj_s = _norm_matmul(hs, nw, w_in_ab, i, tm_s, tn).reshape(bs, ts_pad, -1)
            mix_s, ps, cs = _ab_mix(proj_s, state_pool[i], state_conv[i], pool_lin[i], pscale, conv_w[i],
                                    ts_pad, PAST_LEN, ts, F32)
            hs = _matmul_residual(mix_s.reshape(bs * ts_pad, -1), w_out_ab, i, hs, tm_s, tn)
            pool_p.append(pp)
            pool_s.append(ps)
            conv_p.append(cp)
            conv_s.append(cs)
        else:
            proj_p = _norm_matmul(hp, nw, w_in_c, i, tm_p, tn).reshape(bp, t, -1)
            proj_s = _norm_matmul(hs, nw, w_in_c, i, tm_s, tn).reshape(bs, ts_pad, -1)[:, :ts]
            og_p, rolled_c, kv_p = _attn_prompt(proj_p, (cache_k, cache_v)[i], ts, i, DEPTH // 2, n_keep_p,
                                                kv_p, BF16)
            rolled.append(rolled_c)
            q_s, kn_s, vn_s, g_s = (head_rows(proj_s[..., c * D_ATTN:(c + 1) * D_ATTN]) for c in range(4))
            og_s = _attn_sample(q_s, g_s, kn_s, vn_s, cache_k, cache_v, i)
            og_s = jnp.pad(og_s.reshape(bs, ts, D_ATTN), ((0, 0), (0, ts_pad - ts), (0, 0)))
            if l == DEPTH - 1:
                hp = _matmul_residual_norm(og_p.reshape(bp * t, -1), w_out_c, i, hp, fw, 512)
                hs = _matmul_residual_norm(og_s.reshape(bs * ts_pad, -1), w_out_c, i, hs, fw, tm_s)
            else:
                hp = _matmul_residual(og_p.reshape(bp * t, -1), w_out_c, i, hp, tm_p, tn)
                hs = _matmul_residual(og_s.reshape(bs * ts_pad, -1), w_out_c, i, hs, tm_s, tn)
            k_new.append(kn_s.reshape(bs, ts, N_HEADS, HEAD_DIM))
            v_new.append(vn_s.reshape(bs, ts, N_HEADS, HEAD_DIM))

    k_s = _write_new_rows(rolled[0], jnp.stack(k_new))
    v_s = _write_new_rows(rolled[1], jnp.stack(v_new))
    k_p, v_p = kv_p

    y_prompt = hp.reshape(bp, t, d)
    y_sample = hs.reshape(bs, ts_pad, d)[:, :ts]
    return (y_prompt, y_sample, jnp.stack(pool_p), jnp.stack(pool_s), jnp.stack(conv_p), jnp.stack(conv_s),
            k_p, k_s, v_p, v_s)
```

```python
import functools

import jax
import jax.numpy as jnp
from jax import lax
from jax.experimental import pallas as pl
from jax.experimental.pallas import tpu as pltpu

F32 = jnp.float32
BF16 = jnp.bfloat16

D_MODEL = 2048
DEPTH = 4
PAST_LEN = 16384
D_POOL = D_MODEL // 2
POOL_WINDOWS = (2, 4, 8, 16)
D_POOL_GROUP = D_POOL // len(POOL_WINDOWS)
POOL_HIST = max(POOL_WINDOWS) - 1
D_CONV = D_MODEL // 2
CONV_WIDTH = 3
CONV_HIST = CONV_WIDTH - 1
HEAD_DIM = 128
N_HEADS = D_MODEL // HEAD_DIM
D_ATTN = N_HEADS * HEAD_DIM
DILATED_GROUPS = ((128, 1), (512, 4), (2048, 16))
MAX_WINDOW = max(w for w, _ in DILATED_GROUPS)
Q_BLOCK = 128
ATTN_SCALE = HEAD_DIM ** -0.5
LOG2_E = 1.4426950408889634
RMS_EPS = 1e-6
NEG_INF = -1e30

SUBLANES = 8
POOL_PAD = 16
CONV_PAD = 8
DEINT_CHUNK = 2 * Q_BLOCK
SAMPLE_DENSE_ROWS = 512
SAMPLE_PERIODS = 32
SAMPLE_RESIDUES = 4
ROLL_CHUNK = 128
ROLL_RING = 8
ROLL_DMA_PRIORITY = 1
SCORE_BLOCKS_PER_TRIP = 16
VALUE_BLOCKS_PER_TRIP = 16
MERGE_BLOCKS_PER_TRIP = 8
VMEM_LIMIT = 48 * 1024 * 1024
AB_LAYER_VMEM_LIMIT = 56 * 1024 * 1024
ATTN_VMEM_LIMIT = 57 * 1024 * 1024
AB_ROW_TILE = 256
AB_OUT_CHUNK = 512
AB_PROJ_CHUNK = 1024


def _silu(x):
    return x * jax.nn.sigmoid(x)


def _params(*semantics):
    return pltpu.CompilerParams(dimension_semantics=semantics, vmem_limit_bytes=VMEM_LIMIT)


def _norm_matmul_kernel(x_ref, nw_ref, w_ref, o_ref, xn_ref):
    @pl.when(pl.program_id(1) == 0)
    def _():
        x = x_ref[...]
        ms = jnp.mean(x * x, axis=-1, keepdims=True)
        xn_ref[...] = (x * lax.rsqrt(ms + RMS_EPS) * nw_ref[...]).astype(BF16)

    o_ref[...] = jnp.dot(xn_ref[...], w_ref[...], preferred_element_type=F32)


def _norm_matmul(x, nw, w, layer, tm, tn):
    m, d = x.shape
    n = w.shape[2]
    return pl.pallas_call(
        _norm_matmul_kernel,
        grid=(m // tm, n // tn),
        in_specs=[
            pl.BlockSpec((tm, d), lambda i, j: (i, 0)),
            pl.BlockSpec((1, d), lambda i, j: (0, 0)),
            pl.BlockSpec((None, d, tn), lambda i, j: (layer, 0, j)),
        ],
        out_specs=pl.BlockSpec((tm, tn), lambda i, j: (i, j)),
        out_shape=jax.ShapeDtypeStruct((m, n), F32),
        scratch_shapes=[pltpu.VMEM((tm, d), BF16)],
        compiler_params=_params("parallel", "arbitrary"),
        name="norm_matmul",
    )(x, nw, w)


def _matmul_residual_kernel(a_ref, w_ref, h_ref, o_ref):
    o_ref[...] = h_ref[...] + jnp.dot(a_ref[...].astype(BF16), w_ref[...], preferred_element_type=F32)


def _matmul_residual(a, w, layer, h, tm, tn):
    m, k = a.shape
    n = w.shape[2]
    return pl.pallas_call(
        _matmul_residual_kernel,
        grid=(m // tm, n // tn),
        in_specs=[
            pl.BlockSpec((tm, k), lambda i, j: (i, 0)),
            pl.BlockSpec((None, k, tn), lambda i, j: (layer, 0, j)),
            pl.BlockSpec((tm, tn), lambda i, j: (i, j)),
        ],
        out_specs=pl.BlockSpec((tm, tn), lambda i, j: (i, j)),
        out_shape=jax.ShapeDtypeStruct((m, n), F32),
        compiler_params=_params("parallel", "arbitrary"),
        name="matmul_residual",
    )(a, w, h)


def _matmul_residual_norm_kernel(a_ref, w_ref, h_ref, nw_ref, o_ref):
    y = h_ref[...] + jnp.dot(a_ref[...].astype(BF16), w_ref[...], preferred_element_type=F32)
    ms = jnp.mean(y * y, axis=-1, keepdims=True)
    o_ref[...] = y * lax.rsqrt(ms + RMS_EPS) * nw_ref[...]


def _matmul_residual_norm(a, w, layer, h, nw, tm):
    m, k = a.shape
    n = w.shape[2]
    return pl.pallas_call(
        _matmul_residual_norm_kernel,
        grid=(m // tm,),
        in_specs=[
            pl.BlockSpec((tm, k), lambda i: (i, 0)),
            pl.BlockSpec((None, k, n), lambda i: (layer, 0, 0)),
            pl.BlockSpec((tm, n), lambda i: (i, 0)),
            pl.BlockSpec((1, n), lambda i: (0, 0)),
        ],
        out_specs=pl.BlockSpec((tm, n), lambda i: (i, 0)),
        out_shape=jax.ShapeDtypeStruct((m, n), F32),
        compiler_params=_params("parallel"),
        name="matmul_residual_norm",
    )(a, w, h, nw)


def _mix_history(ph_ref, ch_ref, uext_ref, zext_ref, tm):
    j = pl.program_id(1)

    @pl.when(j == 0)
    def _():
        uext_ref[POOL_PAD - POOL_HIST:POOL_PAD, :] = ph_ref[0]
        zext_ref[CONV_PAD - CONV_HIST:CONV_PAD, :] = ch_ref[0]

    @pl.when(j > 0)
    def _():
        uext_ref[0:POOL_PAD, :] = uext_ref[tm:tm + POOL_PAD, :]
        zext_ref[0:CONV_PAD, :] = zext_ref[tm:tm + CONV_PAD, :]


def _mix_tails(ptail_ref, ctail_ref, uext_ref, zext_ref, t_last):
    @pl.when(pl.program_id(1) == pl.num_programs(1) - 1)
    def _():
        ptail_ref[0] = uext_ref[POOL_PAD + t_last - POOL_HIST:POOL_PAD + t_last, :]
        ctail_ref[0] = zext_ref[CONV_PAD + t_last - CONV_HIST:CONV_PAD + t_last, :]


def _mix_body(proj_ref, plin_ref, pscale_ref, cw_ref, mix_ref, uext_ref, zext_ref, pos0):
    j = pl.program_id(1)
    tm = proj_ref.shape[1]
    gw = D_POOL_GROUP

    uext_ref[POOL_PAD:POOL_PAD + tm, :] = proj_ref[0, :, 0:D_POOL]
    c_off = 2 * D_POOL + D_CONV
    zext_ref[CONV_PAD:CONV_PAD + tm, :] = (
        proj_ref[0, :, c_off:c_off + D_CONV] * proj_ref[0, :, c_off + D_CONV:c_off + 2 * D_CONV])

    pos = (pos0 + j * tm + lax.broadcasted_iota(jnp.int32, (tm, 1), 0)).astype(F32)

    for g, k in enumerate(POOL_WINDOWS):
        c0 = g * gw
        u_g = uext_ref[POOL_PAD:POOL_PAD + tm, c0:c0 + gw]
        s = u_g
        for i in range(1, k):
            s = s + uext_ref[POOL_PAD - i:POOL_PAD - i + tm, c0:c0 + gw]
        cnt = jnp.minimum(float(k), pos + 1.0)
        pooled = s / cnt - u_g
        a = jnp.dot(pooled.astype(BF16), plin_ref[g], preferred_element_type=F32)
        a = a * pscale_ref[:, c0:c0 + gw]
        gate = proj_ref[0, :, D_POOL + c0:D_POOL + c0 + gw]
        mix_ref[0, :, c0:c0 + gw] = (a * _silu(gate)).astype(mix_ref.dtype)

    for c in range(D_CONV // gw):
        c0 = c * gw
        conv = zext_ref[CONV_PAD - 2:CONV_PAD - 2 + tm, c0:c0 + gw] * cw_ref[0:1, c0:c0 + gw]
        conv = conv + zext_ref[CONV_PAD - 1:CONV_PAD - 1 + tm, c0:c0 + gw] * cw_ref[1:2, c0:c0 + gw]
        conv = conv + zext_ref[CONV_PAD:CONV_PAD + tm, c0:c0 + gw] * cw_ref[2:3, c0:c0 + gw]
        b_gate = proj_ref[0, :, 2 * D_POOL + c0:2 * D_POOL + c0 + gw]
        gate = proj_ref[0, :, 2 * D_POOL + 3 * D_CONV + c0:2 * D_POOL + 3 * D_CONV + c0 + gw]
        mix_ref[0, :, D_POOL + c0:D_POOL + c0 + gw] = (b_gate * conv * _silu(gate)).astype(mix_ref.dtype)


def _ab_mix_kernel(proj_ref, ph_ref, ch_ref, plin_ref, pscale_ref, cw_ref,
                   mix_ref, ptail_ref, ctail_ref, uext_ref, zext_ref, *, pos0, t_last):
    _mix_history(ph_ref, ch_ref, uext_ref, zext_ref, proj_ref.shape[1])
    _mix_body(proj_ref, plin_ref, pscale_ref, cw_ref, mix_ref, uext_ref, zext_ref, pos0)
    _mix_tails(ptail_ref, ctail_ref, uext_ref, zext_ref, t_last)


def _ab_mix(proj, pool_hist, conv_hist, plin, pscale, cw, tm, pos0, t_valid, mix_dtype):
    b, t, n = proj.shape
    nj = t // tm
    t_last = t_valid - (nj - 1) * tm
    kern = functools.partial(_ab_mix_kernel, pos0=pos0, t_last=t_last)
    return pl.pallas_call(
        kern,
        grid=(b, nj),
        in_specs=[
            pl.BlockSpec((1, tm, n), lambda i, j: (i, j, 0)),
            pl.BlockSpec((1, POOL_HIST, D_POOL), lambda i, j: (i, 0, 0)),
            pl.BlockSpec((1, CONV_HIST, D_CONV), lambda i, j: (i, 0, 0)),
            pl.BlockSpec(plin.shape, lambda i, j: (0, 0, 0)),
            pl.BlockSpec((1, D_POOL), lambda i, j: (0, 0)),
            pl.BlockSpec((CONV_WIDTH, D_CONV), lambda i, j: (0, 0)),
        ],
        out_specs=[
            pl.BlockSpec((1, tm, D_POOL + D_CONV), lambda i, j: (i, j, 0)),
            pl.BlockSpec((1, POOL_HIST, D_POOL), lambda i, j: (i, 0, 0)),
            pl.BlockSpec((1, CONV_HIST, D_CONV), lambda i, j: (i, 0, 0)),
        ],
        out_shape=[
            jax.ShapeDtypeStruct((b, t, D_POOL + D_CONV), mix_dtype),
            jax.ShapeDtypeStruct((b, POOL_HIST, D_POOL), F32),
            jax.ShapeDtypeStruct((b, CONV_HIST, D_CONV), F32),
        ],
        scratch_shapes=[pltpu.VMEM((tm + POOL_PAD, D_POOL), F32), pltpu.VMEM((tm + CONV_PAD, D_CONV), F32)],
        compiler_params=_params("parallel", "arbitrary"),
        name="ab_mix",
    )(proj, pool_hist, conv_hist, plin, pscale, cw)


def _ab_layer_kernel(x_ref, nw_ref, win_ref, ph_ref, ch_ref, plin_ref, pscale_ref, cw_ref, wout_ref,
                     o_ref, ptail_ref, ctail_ref, proj_ref, mix_ref, uext_ref, zext_ref, *, pos0, t_last):
    _mix_history(ph_ref, ch_ref, uext_ref, zext_ref, x_ref.shape[1])
    x = x_ref[0]
    ms = jnp.mean(x * x, axis=-1, keepdims=True)
    xn = (x * lax.rsqrt(ms + RMS_EPS) * nw_ref[...]).astype(BF16)
    n_in = win_ref.shape[1]
    for c0 in range(0, n_in, AB_PROJ_CHUNK):
        proj_ref[0, :, c0:c0 + AB_PROJ_CHUNK] = jnp.dot(xn, win_ref[:, c0:c0 + AB_PROJ_CHUNK],
                                                        preferred_element_type=F32)
    _mix_body(proj_ref, plin_ref, pscale_ref, cw_ref, mix_ref, uext_ref, zext_ref, pos0)
    y = x
    for c0 in range(0, wout_ref.shape[0], AB_OUT_CHUNK):
        y = y + jnp.dot(mix_ref[0, :, c0:c0 + AB_OUT_CHUNK].astype(BF16), wout_ref[c0:c0 + AB_OUT_CHUNK, :],
                        preferred_element_type=F32)
    o_ref[0] = y
    _mix_tails(ptail_ref, ctail_ref, uext_ref, zext_ref, t_last)


def _ab_layer(x, nw, w_in, w_out, layer, pool_hist, conv_hist, plin, pscale, cw, tm, pos0, t_valid):
    b, t, d = x.shape
    n_in = w_in.shape[2]
    nj = t // tm
    t_last = t_valid - (nj - 1) * tm
    resident = pl.Buffered(1)
    return pl.pallas_call(
        functools.partial(_ab_layer_kernel, pos0=pos0, t_last=t_last),
        grid=(b, nj),
        in_specs=[
            pl.BlockSpec((1, tm, d), lambda i, j: (i, j, 0)),
            pl.BlockSpec((1, d), lambda i, j: (0, 0)),
            pl.BlockSpec((None, d, n_in), lambda i, j: (layer, 0, 0), pipeline_mode=resident),
            pl.BlockSpec((1, POOL_HIST, D_POOL), lambda i, j: (i, 0, 0)),
            pl.BlockSpec((1, CONV_HIST, D_CONV), lambda i, j: (i, 0, 0)),
            pl.BlockSpec(plin.shape, lambda i, j: (0, 0, 0)),
            pl.BlockSpec((1, D_POOL), lambda i, j: (0, 0)),
            pl.BlockSpec((CONV_WIDTH, D_CONV), lambda i, j: (0, 0)),
            pl.BlockSpec((None, D_POOL + D_CONV, d), lambda i, j: (layer, 0, 0), pipeline_mode=resident),
        ],
        out_specs=[
            pl.BlockSpec((1, tm, d), lambda i, j: (i, j, 0)),
            pl.BlockSpec((1, POOL_HIST, D_POOL), lambda i, j: (i, 0, 0)),
            pl.BlockSpec((1, CONV_HIST, D_CONV), lambda i, j: (i, 0, 0)),
        ],
        out_shape=[
            jax.ShapeDtypeStruct((b, t, d), F32),
            jax.ShapeDtypeStruct((b, POOL_HIST, D_POOL), F32),
            jax.ShapeDtypeStruct((b, CONV_HIST, D_CONV), F32),
        ],
        scratch_shapes=[
            pltpu.VMEM((1, tm, n_in), F32),
            pltpu.VMEM((1, tm, D_POOL + D_CONV), BF16 if tm % (2 * SUBLANES) == 0 else F32),
            pltpu.VMEM((tm + POOL_PAD, D_POOL), F32),
            pltpu.VMEM((tm + CONV_PAD, D_CONV), F32),
        ],
        compiler_params=pltpu.CompilerParams(dimension_semantics=("parallel", "arbitrary"),
                                             vmem_limit_bytes=AB_LAYER_VMEM_LIMIT),
        name="ab_layer",
    )(x, nw, w_in, pool_hist, conv_hist, plin, pscale, cw, w_out)


def _rows(start, size, stride):
    if stride == 1:
        return pl.ds(start, size)
    return pl.ds(start, size, stride=stride)


def _attn_prompt_kernel(*refs, n_new, kv_layer, fill_layers, n_aliased):
    q_ref, k_blk_ref, v_blk_ref, g_ref, cache_ref = refs[:5]
    (o_ref, roll_ref, kp_ref, vp_ref, kd_ref, vd_ref, bias_ref, s_ref, ring_ref, sem_ref, kv_sem_ref,
     *state_refs) = refs[5 + n_aliased:]
    k_ref, v_ref = k_blk_ref.at[0], v_blk_ref.at[0]
    t = q_ref.shape[0]
    n_keep = kp_ref.shape[2]

    step = pl.program_id(0) * pl.num_programs(1) + pl.program_id(1)

    def kv_copies():
        return [pltpu.make_async_copy(src.at[0, pl.ds(t - n_keep, n_keep), :],
                                      dst.at[kv_layer, pl.program_id(0), :, pl.program_id(1), :],
                                      kv_sem_ref.at[m, 0])
                for m, (src, dst) in enumerate(((k_blk_ref, kp_ref), (v_blk_ref, vp_ref)))]

    def kv_fill_copies():
        pieces = n_keep // ROLL_CHUNK
        return [pltpu.make_async_copy(ring_ref.at[m],
                                      dst.at[layer, step // pieces, pl.ds((step % pieces) * ROLL_CHUNK, ROLL_CHUNK)],
                                      kv_sem_ref.at[m, 1 + n])
                for n, layer in enumerate(fill_layers) for m, dst in enumerate((kp_ref, vp_ref))]

    for copy in kv_copies():
        copy.start(priority=ROLL_DMA_PRIORITY)
    n_cb, n_buf = cache_ref.shape[1], cache_ref.shape[2]
    slab, slab_half = step // 2, step % 2
    lyr, cb = slab // n_cb, slab % n_cb
    dst0 = slab_half * (n_buf // 2 - n_new)
    n_ring = ring_ref.shape[0]

    def roll_read(c):
        return pltpu.make_async_copy(cache_ref.at[lyr, cb, pl.ds(dst0 + c * ROLL_CHUNK + n_new, ROLL_CHUNK)],
                                     ring_ref.at[c % n_ring], sem_ref.at[0, c % n_ring])

    def roll_write(c):
        return pltpu.make_async_copy(ring_ref.at[c % n_ring],
                                     roll_ref.at[lyr, cb, pl.ds(dst0 + c * ROLL_CHUNK, ROLL_CHUNK)],
                                     sem_ref.at[1, c % n_ring])

    def roll_tail():
        return pltpu.make_async_copy(ring_ref.at[0, pl.ds(0, n_new)],
                                     roll_ref.at[lyr, cb, pl.ds(n_buf - n_new, n_new)], sem_ref.at[1, 0])

    for c in range(n_ring):
        roll_read(c).start(priority=ROLL_DMA_PRIORITY)
    n_groups = len(DILATED_GROUPS)
    qf_ref, kf_ref, vf_ref = state_refs[0:3]
    acc_refs = (None,) + tuple(state_refs[3:3 + n_groups - 1])
    stat_refs = (None,) + tuple(state_refs[3 + n_groups - 1:])
    mid_dil = DILATED_GROUPS[n_groups - 2][1]
    nt_dims = (((1,), (1,)), ((), ()))
    half = HEAD_DIM // 2
    lane = lax.broadcasted_iota(jnp.int32, (Q_BLOCK, HEAD_DIM), 1)

    @pl.when((pl.program_id(0) == 0) & (pl.program_id(1) == 0))
    def _():
        delta = (lax.broadcasted_iota(jnp.int32, (Q_BLOCK, 2 * Q_BLOCK), 0)
                 - lax.broadcasted_iota(jnp.int32, (Q_BLOCK, 2 * Q_BLOCK), 1))
        for sel in range(2):
            dist = delta + sel * Q_BLOCK
            bias_ref[sel] = jnp.where((dist >= 0) & (dist <= Q_BLOCK), 0.0, NEG_INF)
        vd_ref[:, :, HEAD_DIM:] = jnp.ones((n_groups, t, HEAD_DIM), BF16)

    def deinterleave(gi, d, d_src, ksrc_ref, vsrc_ref, keep_f32):
        stream_len = t // d
        ratio = d // d_src

        def body(c, carry):
            dst = pl.multiple_of(c * DEINT_CHUNK, DEINT_CHUNK)
            r = dst // stream_len
            m0 = dst % stream_len
            src = _rows((r % d_src) * (t // d_src) + r // d_src + ratio * m0, DEINT_CHUNK, ratio)
            k = ksrc_ref[src, :]
            v = vsrc_ref[src, :]
            kd_ref[gi, pl.ds(dst, DEINT_CHUNK), :] = k.astype(BF16)
            vd_ref[gi, pl.ds(dst, DEINT_CHUNK), 0:HEAD_DIM] = v.astype(BF16)
            if keep_f32:
                qf_ref[pl.ds(dst, DEINT_CHUNK), :] = q_ref[src, :]
                kf_ref[pl.ds(dst, DEINT_CHUNK), :] = k
                vf_ref[pl.ds(dst, DEINT_CHUNK), :] = v
            return carry

        lax.fori_loop(0, t // DEINT_CHUNK, body, 0, unroll=2)

    def merged_output(rows, m0, acc0, l0):
        ms, ls, accs = [m0], [l0], [acc0]
        for gi in range(1, n_groups):
            st = stat_refs[gi][rows, :]
            sw = pltpu.roll(st, half, axis=1)
            ms.append(jnp.where(lane < half, st, sw))
            ls.append(jnp.where(lane < half, sw, st))
            accs.append(acc_refs[gi][rows, :])
        m_all = functools.reduce(jnp.maximum, ms)
        num = None
        den = None
        for m_g, l_g, acc_g in zip(ms, ls, accs):
            w = jnp.exp2(m_g - m_all)
            num = w * acc_g if num is None else num + w * acc_g
            den = w * l_g if den is None else den + w * l_g
        g = g_ref[rows, :]
        return ((num * g) / (den * (1.0 + jnp.exp(-g)))).astype(o_ref.dtype)

    def run_group(gi, d, after_scores, after_values):
        stream_len = t // d
        nb = stream_len // Q_BLOCK

        def block_rows(idx):
            r = idx // nb
            mb = idx % nb
            kb = jnp.maximum(mb - 1, 0)
            q_start = r + d * Q_BLOCK * mb
            q_rows = _rows(pl.multiple_of(q_start, Q_BLOCK) if d == 1 else q_start, Q_BLOCK, d)
            k_rows = pl.ds(pl.multiple_of(r * stream_len + kb * Q_BLOCK, Q_BLOCK), 2 * Q_BLOCK)
            return q_rows, k_rows, mb - kb

        def load_q(idx, q_rows):
            if gi < n_groups - 1:
                return q_ref[q_rows, :]
            r = idx // nb
            mb = idx % nb
            ratio = d // mid_dil
            start = (r % mid_dil) * (t // mid_dil) + r // mid_dil + ratio * Q_BLOCK * mb
            return qf_ref[_rows(start, Q_BLOCK, ratio), :]

        def scores(it, carry):
            s = []
            for u in range(SCORE_BLOCKS_PER_TRIP):
                idx = it * SCORE_BLOCKS_PER_TRIP + u
                q_rows, k_rows, sel = block_rows(idx)
                q = (load_q(idx, q_rows) * (ATTN_SCALE * LOG2_E)).astype(BF16)
                s.append(lax.dot_general(q, kd_ref[gi, k_rows, :], nt_dims, preferred_element_type=F32)
                         + bias_ref[sel])
            for u in range(SCORE_BLOCKS_PER_TRIP):
                s_ref[it * SCORE_BLOCKS_PER_TRIP + u] = s[u]
            return carry

        n_val = MERGE_BLOCKS_PER_TRIP if gi == 0 else VALUE_BLOCKS_PER_TRIP

        def values(it, carry):
            rows, s, v = [], [], []
            for u in range(n_val):
                q_rows, k_rows, _ = block_rows(it * n_val + u)
                rows.append(q_rows)
                s.append(s_ref[it * n_val + u])
                v.append(vd_ref[gi, k_rows, :])
            outs = []
            for u in range(n_val):
                m_blk = jnp.max(s[u], axis=-1, keepdims=True)
                p = jnp.exp2(s[u] - m_blk).astype(BF16)
                pv = jnp.dot(p, v[u], preferred_element_type=F32)
                acc, l_blk = pv[:, 0:HEAD_DIM], pv[:, HEAD_DIM:]
                if gi == 0:
                    outs.append((merged_output(rows[u], jnp.broadcast_to(m_blk, acc.shape), acc, l_blk),))
                else:
                    outs.append((acc, jnp.where(lane < half, m_blk, l_blk)))
            for u in range(n_val):
                if gi == 0:
                    o_ref[rows[u], :] = outs[u][0]
                else:
                    acc_refs[gi][rows[u], :] = outs[u][0]
                    stat_refs[gi][rows[u], :] = outs[u][1]
            return carry

        lax.fori_loop(0, d * nb // SCORE_BLOCKS_PER_TRIP, scores, 0)
        after_scores()
        lax.fori_loop(0, d * nb // n_val, values, 0)
        after_values()

    ksrc_ref, vsrc_ref, d_src = k_ref, v_ref, 1
    for gi, (_, d) in enumerate(DILATED_GROUPS):
        keep_f32 = 0 < gi < n_groups - 1
        deinterleave(gi, d, d_src, ksrc_ref, vsrc_ref, keep_f32)
        if keep_f32:
            ksrc_ref, vsrc_ref, d_src = kf_ref, vf_ref, d
    def forward():
        for c in range(n_ring):
            roll_read(c).wait()
            roll_write(c).start(priority=ROLL_DMA_PRIORITY)
        for copy in kv_fill_copies():
            copy.start(priority=ROLL_DMA_PRIORITY)

    def drain():
        for c in range(n_ring):
            roll_write(c).wait()
        for copy in kv_copies() + kv_fill_copies():
            copy.wait()

        @pl.when(slab_half == 1)
        def _():
            roll_tail().start()
            roll_tail().wait()

    def nothing():
        pass

    hooks = [(nothing, forward), (nothing, nothing), (nothing, drain)]
    for (after_scores, after_values), gi in zip(hooks, reversed(range(n_groups))):
        run_group(gi, DILATED_GROUPS[gi][1], after_scores, after_values)


def _attn_prompt(proj, cache, n_new, kv_layer, n_kv_layers, n_keep, kv_prev, out_dtype):
    b, t, _ = proj.shape
    first = kv_prev is None
    assert first == (kv_layer == 0)
    fill_layers = tuple(range(1, n_kv_layers)) if first else ()
    assert b * n_keep == b * N_HEADS * ROLL_CHUNK
    kv_shape = jax.ShapeDtypeStruct((n_kv_layers, b, n_keep, N_HEADS, HEAD_DIM), F32)
    any_spec = pl.BlockSpec(memory_space=pl.ANY)
    aliased = () if first else tuple(kv_prev)
    n_groups = len(DILATED_GROUPS)
    n_layers, n_cb, n_buf = cache.shape[:3]
    assert b * N_HEADS == 2 * n_layers * n_cb and n_buf // 2 == ROLL_RING * ROLL_CHUNK
    dils = [dil for _, dil in DILATED_GROUPS]
    assert n_groups == 3 and dils[0] == 1 and all(hi % lo == 0 for lo, hi in zip(dils, dils[1:]))
    for window, dil in DILATED_GROUPS:
        assert window // dil == Q_BLOCK and (t // dil) % DEINT_CHUNK == 0
        assert (t // Q_BLOCK) % SCORE_BLOCKS_PER_TRIP == 0 and (t // Q_BLOCK) % VALUE_BLOCKS_PER_TRIP == 0
    blk = (None, t, HEAD_DIM)
    og, rolled, kp, vp = pl.pallas_call(
        functools.partial(_attn_prompt_kernel, n_new=n_new, kv_layer=kv_layer, fill_layers=fill_layers,
                          n_aliased=len(aliased)),
        grid=(b, N_HEADS),
        in_specs=[
            pl.BlockSpec(blk, lambda i, h: (i, 0, h)),
            pl.BlockSpec((1, t, HEAD_DIM), lambda i, h: (i, 0, N_HEADS + h)),
            pl.BlockSpec((1, t, HEAD_DIM), lambda i, h: (i, 0, 2 * N_HEADS + h)),
            pl.BlockSpec(blk, lambda i, h: (i, 0, 3 * N_HEADS + h)),
            any_spec,
        ] + [any_spec] * len(aliased),
        out_specs=[pl.BlockSpec(blk, lambda i, h: (i, 0, h)), any_spec, any_spec, any_spec],
        out_shape=[jax.ShapeDtypeStruct((b, t, D_ATTN), out_dtype),
                   jax.ShapeDtypeStruct(cache.shape, cache.dtype), kv_shape, kv_shape],
        scratch_shapes=(
            [pltpu.VMEM((n_groups, t, HEAD_DIM), BF16), pltpu.VMEM((n_groups, t, 2 * HEAD_DIM), BF16)]
            + [pltpu.VMEM((2, Q_BLOCK, 2 * Q_BLOCK), F32)]
            + [pltpu.VMEM((t // Q_BLOCK, Q_BLOCK, 2 * Q_BLOCK), F32)]
            + [pltpu.VMEM((ROLL_RING, ROLL_CHUNK) + cache.shape[3:], cache.dtype)]
            + [pltpu.SemaphoreType.DMA((2, ROLL_RING))]
            + [pltpu.SemaphoreType.DMA((2, 1 + len(fill_layers)))]
            + [pltpu.VMEM((t, HEAD_DIM), F32)] * (3 + 2 * (n_groups - 1))),
        input_output_aliases={5 + n: 2 + n for n in range(len(aliased))},
        compiler_params=pltpu.CompilerParams(dimension_semantics=("arbitrary", "arbitrary"),
                                             vmem_limit_bytes=ATTN_VMEM_LIMIT),
        name="attn_prompt",
    )(proj, proj, proj, proj, cache, *aliased)
    return og, rolled, (kp, vp)


def _group_count(dist):
    cnt = jnp.zeros(dist.shape, F32)
    for window, dil in DILATED_GROUPS:
        hit = (dist >= 0) & (dist <= window) & ((dist & (dil - 1)) == 0)
        cnt = cnt + hit.astype(F32)
    return cnt


def _attn_sample_kernel(q_ref, g_ref, kn_ref, vn_ref, ks_ref, vs_ref, kd_ref, vd_ref, o_ref,
                        m_ref, l_ref, acc_ref, *, n_buf):
    j = pl.program_id(1)
    nj = pl.num_programs(1)
    n_rows = q_ref.shape[0]
    nt_dims = (((1,), (1,)), ((), ()))
    head_bits = N_HEADS.bit_length() - 1
    dil = DILATED_GROUPS[-1][1]

    @pl.when(j == 0)
    def _():
        m_ref[...] = jnp.full(m_ref.shape, NEG_INF, F32)
        l_ref[...] = jnp.zeros(l_ref.shape, F32)
        acc_ref[...] = jnp.zeros(acc_ref.shape, F32)

    q = (q_ref[...] * ATTN_SCALE).astype(BF16)

    def accumulate(k2, v2, key_pos):
        n_cols = k2.shape[0]
        row = lax.broadcasted_iota(jnp.int32, (n_rows, n_cols), 0)
        col = lax.broadcasted_iota(jnp.int32, (n_rows, n_cols), 1)
        same_head = (row & (N_HEADS - 1)) == (col & (N_HEADS - 1))
        dist = (n_buf + (row >> head_bits)) - key_pos(col)
        cnt = jnp.where(same_head, _group_count(dist), 0.0)
        s = lax.dot_general(q, k2.astype(BF16), nt_dims, preferred_element_type=F32)
        s = jnp.where(cnt > 0, s, NEG_INF)
        m_prev = m_ref[...]
        m_new = jnp.maximum(m_prev, jnp.max(s, axis=-1, keepdims=True))
        alpha = jnp.exp(m_prev - m_new)
        p = cnt * jnp.exp(s - m_new)
        l_ref[...] = alpha * l_ref[...] + jnp.sum(p, axis=-1, keepdims=True)
        acc_ref[...] = alpha * acc_ref[...] + jnp.dot(p.astype(BF16), v2.astype(BF16),
                                                      preferred_element_type=F32)
        m_ref[...] = m_new

    @pl.when(j < nj - 1)
    def _():
        periods, residues = ks_ref.shape[0], ks_ref.shape[1]
        n_cols = periods * residues * N_HEADS
        res_bits = residues.bit_length() - 1

        def key_pos(col):
            period = j * periods + (col >> (head_bits + res_bits))
            return period * dil + ((col >> head_bits) & (residues - 1))

        accumulate(ks_ref[...].reshape(n_cols, HEAD_DIM), vs_ref[...].reshape(n_cols, HEAD_DIM), key_pos)

    @pl.when(j == nj - 1)
    def _():
        n_dense = kd_ref.shape[0]
        accumulate(kd_ref[...].reshape(n_dense * N_HEADS, HEAD_DIM),
                   vd_ref[...].reshape(n_dense * N_HEADS, HEAD_DIM),
                   lambda col: (n_buf - n_dense) + (col >> head_bits))
        accumulate(kn_ref[...], vn_ref[...], lambda col: n_buf + (col >> head_bits))
        o_ref[...] = (acc_ref[...] / l_ref[...]) * _silu(g_ref[...])


def _attn_sample(q2, g2, kn2, vn2, cache_k, cache_v, layer):
    b, n_rows, _ = q2.shape
    n_layers, _, n_buf = cache_k.shape[:3]
    n_new = n_rows // N_HEADS
    dil = DILATED_GROUPS[-1][1]
    n_dense = SAMPLE_DENSE_ROWS
    assert all(w <= n_dense for w, _ in DILATED_GROUPS[:-1]) and n_buf % dil == 0 and n_dense % dil == 0
    assert n_new <= SAMPLE_RESIDUES and dil % SAMPLE_RESIDUES == 0 and n_buf % n_dense == 0
    n_sparse_steps = (n_buf - n_dense) // dil // SAMPLE_PERIODS
    assert n_sparse_steps * SAMPLE_PERIODS * dil == n_buf - n_dense and n_sparse_steps >= 1
    by_period = (n_layers, cache_k.shape[1], n_buf // dil, dil, N_HEADS, HEAD_DIM)
    rblk = (None, n_rows, HEAD_DIM)
    sparse = pl.BlockSpec((None, None, SAMPLE_PERIODS, SAMPLE_RESIDUES, N_HEADS, HEAD_DIM),
                          lambda i, j: (layer, i, jnp.minimum(j, n_sparse_steps - 1), 0, 0, 0))
    dense = pl.BlockSpec((None, None, n_dense, N_HEADS, HEAD_DIM),
                         lambda i, j: (layer, i, n_buf // n_dense - 1, 0, 0))
    return pl.pallas_call(
        functools.partial(_attn_sample_kernel, n_buf=n_buf),
        grid=(b, n_sparse_steps + 1),
        in_specs=[pl.BlockSpec(rblk, lambda i, j: (i, 0, 0))] * 4 + [sparse, sparse, dense, dense],
        out_specs=pl.BlockSpec(rblk, lambda i, j: (i, 0, 0)),
        out_shape=jax.ShapeDtypeStruct((b, n_rows, HEAD_DIM), F32),
        scratch_shapes=[pltpu.VMEM((n_rows, 1), F32), pltpu.VMEM((n_rows, 1), F32),
                        pltpu.VMEM((n_rows, HEAD_DIM), F32)],
        compiler_params=_params("parallel", "arbitrary"),
        name="attn_sample",
    )(q2, g2, kn2, vn2, cache_k.reshape(by_period), cache_v.reshape(by_period), cache_k, cache_v)


def _write_new_rows_kernel(rolled_ref, new_ref, out_ref, sem):
    del rolled_ref
    n_layers, n_batch, n_new = new_ref.shape[:3]
    n_buf = out_ref.shape[2]
    copies = [pltpu.make_async_copy(new_ref.at[l, b], out_ref.at[l, b, pl.ds(n_buf - n_new, n_new)],
                                    sem.at[l * n_batch + b])
              for l in range(n_layers) for b in range(n_batch)]
    for copy in copies:
        copy.start()
    for copy in copies:
        copy.wait()


def _write_new_rows(rolled, new):
    n_layers, n_batch = new.shape[:2]
    return pl.pallas_call(
        _write_new_rows_kernel,
        in_specs=[pl.BlockSpec(memory_space=pl.ANY), pl.BlockSpec(memory_space=pltpu.VMEM)],
        out_specs=pl.BlockSpec(memory_space=pl.ANY),
        out_shape=jax.ShapeDtypeStruct(rolled.shape, rolled.dtype),
        scratch_shapes=[pltpu.SemaphoreType.DMA((n_layers * n_batch,))],
        input_output_aliases={0: 0},
        name="write_new_rows",
    )(rolled, new)


def _cast_kernel(x_ref, o_ref):
    o_ref[...] = x_ref[...].astype(o_ref.dtype)


def _cast_bf16(w, rows):
    n_layers, r, c = w.shape
    return pl.pallas_call(
        _cast_kernel,
        grid=(n_layers, r // rows),
        in_specs=[pl.BlockSpec((1, rows, c), lambda l, i: (l, i, 0))],
        out_specs=pl.BlockSpec((1, rows, c), lambda l, i: (l, i, 0)),
        out_shape=jax.ShapeDtypeStruct(w.shape, BF16),
        compiler_params=_params("parallel", "parallel"),
        name="cast_bf16",
    )(w)


def kernel(x_prompt, x_sample, state_pool, state_conv, cache_k, cache_v, norm_w, final_norm_w,
           w_in_ab, pool_lin, pool_scale, conv_w, w_out_ab, w_in_c, w_out_c):
    bp, t, d = x_prompt.shape
    bs, ts, _ = x_sample.shape
    ts_pad = -(-ts // SUBLANES) * SUBLANES
    n_buf = cache_k.shape[2]
    n_keep_p = min(MAX_WINDOW, t)

    hp = x_prompt.reshape(bp * t, d)
    hs = jnp.pad(x_sample, ((0, 0), (0, ts_pad - ts), (0, 0))).reshape(bs * ts_pad, d)
    zero_pool = jnp.zeros((bp, POOL_HIST, D_POOL), F32)
    zero_conv = jnp.zeros((bp, CONV_HIST, D_CONV), F32)

    tm_p, tn = 1024, 1024
    tm_s = bs * ts_pad
    cast_rows = 256

    w_in_ab = _cast_bf16(w_in_ab, cast_rows)
    w_out_ab = _cast_bf16(w_out_ab, cast_rows)
    w_in_c = _cast_bf16(w_in_c, cast_rows)
    w_out_c = _cast_bf16(w_out_c, cast_rows)
    pool_lin = _cast_bf16(pool_lin.reshape(pool_lin.shape[0], -1, D_POOL_GROUP), cast_rows).reshape(pool_lin.shape)

    def head_rows(a):
        return a.reshape(a.shape[0], a.shape[1] * N_HEADS, HEAD_DIM)

    assert DEPTH == 4
    assert min(MAX_WINDOW, n_buf + ts) == n_buf
    rolled = []
    kv_p = None
    fw = final_norm_w.reshape(1, d)
    pool_p, pool_s, conv_p, conv_s = [], [], [], []
    k_new, v_new = [], []
    for l in range(DEPTH):
        i = l // 2
        nw = norm_w[l].reshape(1, d)
        if l % 2 == 0:
            pscale = pool_scale[i].reshape(1, D_POOL)
            hp, pp, cp = _ab_layer(hp.reshape(bp, t, d), nw, w_in_ab, w_out_ab, i, zero_pool, zero_conv,
                                   pool_lin[i], pscale, conv_w[i], AB_ROW_TILE, 0, t)
            hp = hp.reshape(bp * t, d)
            proj_s = _norm_matmul(hs, nw, w_in_ab, i, tm_s, tn).reshape(bs, ts_pad, -1)
            mix_s, ps, cs = _ab_mix(proj_s, state_pool[i], state_conv[i], pool_lin[i], pscale, conv_w[i],
                                    ts_pad, PAST_LEN, ts, F32)
            hs = _matmul_residual(mix_s.reshape(bs * ts_pad, -1), w_out_ab, i, hs, tm_s, tn)
            pool_p.append(pp)
            pool_s.append(ps)
            conv_p.append(cp)
            conv_s.append(cs)
        else:
            proj_p = _norm_matmul(hp, nw, w_in_c, i, tm_p, tn).reshape(bp, t, -1)
            proj_s = _norm_matmul(hs, nw, w_in_c, i, tm_s, tn).reshape(bs, ts_pad, -1)[:, :ts]
            og_p, rolled_c, kv_p = _attn_prompt(proj_p, (cache_k, cache_v)[i], ts, i, DEPTH // 2, n_keep_p,
                                                kv_p, BF16)
            rolled.append(rolled_c)
            q_s, kn_s, vn_s, g_s = (head_rows(proj_s[..., c * D_ATTN:(c + 1) * D_ATTN]) for c in range(4))
            og_s = _attn_sample(q_s, g_s, kn_s, vn_s, cache_k, cache_v, i)
            og_s = jnp.pad(og_s.reshape(bs, ts, D_ATTN), ((0, 0), (0, ts_pad - ts), (0, 0)))
            if l == DEPTH - 1:
                hp = _matmul_residual_norm(og_p.reshape(bp * t, -1), w_out_c, i, hp, fw, 512)
                hs = _matmul_residual_norm(og_s.reshape(bs * ts_pad, -1), w_out_c, i, hs, fw, tm_s)
            else:
                hp = _matmul_residual(og_p.reshape(bp * t, -1), w_out_c, i, hp, tm_p, tn)
                hs = _matmul_residual(og_s.reshape(bs * ts_pad, -1), w_out_c, i, hs, tm_s, tn)
            k_new.append(kn_s.reshape(bs, ts, N_HEADS, HEAD_DIM))
            v_new.append(vn_s.reshape(bs, ts, N_HEADS, HEAD_DIM))

    k_s = _write_new_rows(rolled[0], jnp.stack(k_new))
    v_s = _write_new_rows(rolled[1], jnp.stack(v_new))
    k_p, v_p = kv_p

    y_prompt = hp.reshape(bp, t, d)
    y_sample = hs.reshape(bs, ts_pad, d)[:, :ts]
    return (y_prompt, y_sample, jnp.stack(pool_p), jnp.stack(pool_s), jnp.stack(conv_p), jnp.stack(conv_s),
            k_p, k_s, v_p, v_s)
```

```python
import functools

import jax
import jax.numpy as jnp
from jax import lax
from jax.experimental import pallas as pl
from jax.experimental.pallas import tpu as pltpu

F32 = jnp.float32
BF16 = jnp.bfloat16

D_MODEL = 2048
DEPTH = 4
PAST_LEN = 16384
D_POOL = D_MODEL // 2
POOL_WINDOWS = (2, 4, 8, 16)
D_POOL_GROUP = D_POOL // len(POOL_WINDOWS)
POOL_HIST = max(POOL_WINDOWS) - 1
D_CONV = D_MODEL // 2
CONV_WIDTH = 3
CONV_HIST = CONV_WIDTH - 1
HEAD_DIM = 128
N_HEADS = D_MODEL // HEAD_DIM
D_ATTN = N_HEADS * HEAD_DIM
DILATED_GROUPS = ((128, 1), (512, 4), (2048, 16))
MAX_WINDOW = max(w for w, _ in DILATED_GROUPS)
Q_BLOCK = 128
ATTN_SCALE = HEAD_DIM ** -0.5
LOG2_E = 1.4426950408889634
RMS_EPS = 1e-6
NEG_INF = -1e30

SUBLANES = 8
POOL_PAD = 16
CONV_PAD = 8
DEINT_CHUNK = 2 * Q_BLOCK
SAMPLE_DENSE_ROWS = 512
SAMPLE_PERIODS = 32
SAMPLE_RESIDUES = 4
ROLL_CHUNK = 128
ROLL_RING = 8
ROLL_DMA_PRIORITY = 1
SCORE_BLOCKS_PER_TRIP = 16
VALUE_BLOCKS_PER_TRIP = 16
MERGE_BLOCKS_PER_TRIP = 8
VMEM_LIMIT = 48 * 1024 * 1024
AB_LAYER_VMEM_LIMIT = 56 * 1024 * 1024
ATTN_VMEM_LIMIT = 57 * 1024 * 1024
AB_ROW_TILE = 256
AB_OUT_CHUNK = 512
AB_PROJ_CHUNK = 1024


def _silu(x):
    return x * jax.nn.sigmoid(x)


def _params(*semantics):
    return pltpu.CompilerParams(dimension_semantics=semantics, vmem_limit_bytes=VMEM_LIMIT)


def _norm_matmul_kernel(x_ref, nw_ref, w_ref, o_ref, xn_ref):
    @pl.when(pl.program_id(1) == 0)
    def _():
        x = x_ref[...]
        ms = jnp.mean(x * x, axis=-1, keepdims=True)
        xn_ref[...] = (x * lax.rsqrt(ms + RMS_EPS) * nw_ref[...]).astype(BF16)

    o_ref[...] = jnp.dot(xn_ref[...], w_ref[...], preferred_element_type=F32)


def _norm_matmul(x, nw, w, layer, tm, tn):
    m, d = x.shape
    n = w.shape[2]
    return pl.pallas_call(
        _norm_matmul_kernel,
        grid=(m // tm, n // tn),
        in_specs=[
            pl.BlockSpec((tm, d), lambda i, j: (i, 0)),
            pl.BlockSpec((1, d), lambda i, j: (0, 0)),
            pl.BlockSpec((None, d, tn), lambda i, j: (layer, 0, j)),
        ],
        out_specs=pl.BlockSpec((tm, tn), lambda i, j: (i, j)),
        out_shape=jax.ShapeDtypeStruct((m, n), F32),
        scratch_shapes=[pltpu.VMEM((tm, d), BF16)],
        compiler_params=_params("parallel", "arbitrary"),
        name="norm_matmul",
    )(x, nw, w)


def _matmul_residual_kernel(a_ref, w_ref, h_ref, o_ref):
    o_ref[...] = h_ref[...] + jnp.dot(a_ref[...].astype(BF16), w_ref[...], preferred_element_type=F32)


def _matmul_residual(a, w, layer, h, tm, tn):
    m, k = a.shape
    n = w.shape[2]
    return pl.pallas_call(
        _matmul_residual_kernel,
        grid=(m // tm, n // tn),
        in_specs=[
            pl.BlockSpec((tm, k), lambda i, j: (i, 0)),
            pl.BlockSpec((None, k, tn), lambda i, j: (layer, 0, j)),
            pl.BlockSpec((tm, tn), lambda i, j: (i, j)),
        ],
        out_specs=pl.BlockSpec((tm, tn), lambda i, j: (i, j)),
        out_shape=jax.ShapeDtypeStruct((m, n), F32),
        compiler_params=_params("parallel", "arbitrary"),
        name="matmul_residual",
    )(a, w, h)


def _matmul_residual_norm_kernel(a_ref, w_ref, h_ref, nw_ref, o_ref):
    y = h_ref[...] + jnp.dot(a_ref[...].astype(BF16), w_ref[...], preferred_element_type=F32)
    ms = jnp.mean(y * y, axis=-1, keepdims=True)
    o_ref[...] = y * lax.rsqrt(ms + RMS_EPS) * nw_ref[...]


def _matmul_residual_norm(a, w, layer, h, nw, tm):
    m, k = a.shape
    n = w.shape[2]
    return pl.pallas_call(
        _matmul_residual_norm_kernel,
        grid=(m // tm,),
        in_specs=[
            pl.BlockSpec((tm, k), lambda i: (i, 0)),
            pl.BlockSpec((None, k, n), lambda i: (layer, 0, 0)),
            pl.BlockSpec((tm, n), lambda i: (i, 0)),
            pl.BlockSpec((1, n), lambda i: (0, 0)),
        ],
        out_specs=pl.BlockSpec((tm, n), lambda i: (i, 0)),
        out_shape=jax.ShapeDtypeStruct((m, n), F32),
        compiler_params=_params("parallel"),
        name="matmul_residual_norm",
    )(a, w, h, nw)


def _mix_history(ph_ref, ch_ref, uext_ref, zext_ref, tm):
    j = pl.program_id(1)

    @pl.when(j == 0)
    def _():
        uext_ref[POOL_PAD - POOL_HIST:POOL_PAD, :] = ph_ref[0]
        zext_ref[CONV_PAD - CONV_HIST:CONV_PAD, :] = ch_ref[0]

    @pl.when(j > 0)
    def _():
        uext_ref[0:POOL_PAD, :] = uext_ref[tm:tm + POOL_PAD, :]
        zext_ref[0:CONV_PAD, :] = zext_ref[tm:tm + CONV_PAD, :]


def _mix_tails(ptail_ref, ctail_ref, uext_ref, zext_ref, t_last):
    @pl.when(pl.program_id(1) == pl.num_programs(1) - 1)
    def _():
        ptail_ref[0] = uext_ref[POOL_PAD + t_last - POOL_HIST:POOL_PAD + t_last, :]
        ctail_ref[0] = zext_ref[CONV_PAD + t_last - CONV_HIST:CONV_PAD + t_last, :]


def _mix_body(proj_ref, plin_ref, pscale_ref, cw_ref, mix_ref, uext_ref, zext_ref, pos0):
    j = pl.program_id(1)
    tm = proj_ref.shape[1]
    gw = D_POOL_GROUP

    uext_ref[POOL_PAD:POOL_PAD + tm, :] = proj_ref[0, :, 0:D_POOL]
    c_off = 2 * D_POOL + D_CONV
    zext_ref[CONV_PAD:CONV_PAD + tm, :] = (
        proj_ref[0, :, c_off:c_off + D_CONV] * proj_ref[0, :, c_off + D_CONV:c_off + 2 * D_CONV])

    pos = (pos0 + j * tm + lax.broadcasted_iota(jnp.int32, (tm, 1), 0)).astype(F32)

    for g, k in enumerate(POOL_WINDOWS):
        c0 = g * gw
        u_g = uext_ref[POOL_PAD:POOL_PAD + tm, c0:c0 + gw]
        s = u_g
        for i in range(1, k):
            s = s + uext_ref[POOL_PAD - i:POOL_PAD - i + tm, c0:c0 + gw]
        cnt = jnp.minimum(float(k), pos + 1.0)
        pooled = s / cnt - u_g
        a = jnp.dot(pooled.astype(BF16), plin_ref[g], preferred_element_type=F32)
        a = a * pscale_ref[:, c0:c0 + gw]
        gate = proj_ref[0, :, D_POOL + c0:D_POOL + c0 + gw]
        mix_ref[0, :, c0:c0 + gw] = (a * _silu(gate)).astype(mix_ref.dtype)

    for c in range(D_CONV // gw):
        c0 = c * gw
        conv = zext_ref[CONV_PAD - 2:CONV_PAD - 2 + tm, c0:c0 + gw] * cw_ref[0:1, c0:c0 + gw]
        conv = conv + zext_ref[CONV_PAD - 1:CONV_PAD - 1 + tm, c0:c0 + gw] * cw_ref[1:2, c0:c0 + gw]
        conv = conv + zext_ref[CONV_PAD:CONV_PAD + tm, c0:c0 + gw] * cw_ref[2:3, c0:c0 + gw]
        b_gate = proj_ref[0, :, 2 * D_POOL + c0:2 * D_POOL + c0 + gw]
        gate = proj_ref[0, :, 2 * D_POOL + 3 * D_CONV + c0:2 * D_POOL + 3 * D_CONV + c0 + gw]
        mix_ref[0, :, D_POOL + c0:D_POOL + c0 + gw] = (b_gate * conv * _silu(gate)).astype(mix_ref.dtype)


def _ab_mix_kernel(proj_ref, ph_ref, ch_ref, plin_ref, pscale_ref, cw_ref,
                   mix_ref, ptail_ref, ctail_ref, uext_ref, zext_ref, *, pos0, t_last):
    _mix_history(ph_ref, ch_ref, uext_ref, zext_ref, proj_ref.shape[1])
    _mix_body(proj_ref, plin_ref, pscale_ref, cw_ref, mix_ref, uext_ref, zext_ref, pos0)
    _mix_tails(ptail_ref, ctail_ref, uext_ref, zext_ref, t_last)


def _ab_mix(proj, pool_hist, conv_hist, plin, pscale, cw, tm, pos0, t_valid, mix_dtype):
    b, t, n = proj.shape
    nj = t // tm
    t_last = t_valid - (nj - 1) * tm
    kern = functools.partial(_ab_mix_kernel, pos0=pos0, t_last=t_last)
    return pl.pallas_call(
        kern,
        grid=(b, nj),
        in_specs=[
            pl.BlockSpec((1, tm, n), lambda i, j: (i, j, 0)),
            pl.BlockSpec((1, POOL_HIST, D_POOL), lambda i, j: (i, 0, 0)),
            pl.BlockSpec((1, CONV_HIST, D_CONV), lambda i, j: (i, 0, 0)),
            pl.BlockSpec(plin.shape, lambda i, j: (0, 0, 0)),
            pl.BlockSpec((1, D_POOL), lambda i, j: (0, 0)),
            pl.BlockSpec((CONV_WIDTH, D_CONV), lambda i, j: (0, 0)),
        ],
        out_specs=[
            pl.BlockSpec((1, tm, D_POOL + D_CONV), lambda i, j: (i, j, 0)),
            pl.BlockSpec((1, POOL_HIST, D_POOL), lambda i, j: (i, 0, 0)),
            pl.BlockSpec((1, CONV_HIST, D_CONV), lambda i, j: (i, 0, 0)),
        ],
        out_shape=[
            jax.ShapeDtypeStruct((b, t, D_POOL + D_CONV), mix_dtype),
            jax.ShapeDtypeStruct((b, POOL_HIST, D_POOL), F32),
            jax.ShapeDtypeStruct((b, CONV_HIST, D_CONV), F32),
        ],
        scratch_shapes=[pltpu.VMEM((tm + POOL_PAD, D_POOL), F32), pltpu.VMEM((tm + CONV_PAD, D_CONV), F32)],
        compiler_params=_params("parallel", "arbitrary"),
        name="ab_mix",
    )(proj, pool_hist, conv_hist, plin, pscale, cw)


def _ab_layer_kernel(x_ref, nw_ref, win_ref, ph_ref, ch_ref, plin_ref, pscale_ref, cw_ref, wout_ref,
                     o_ref, ptail_ref, ctail_ref, proj_ref, mix_ref, uext_ref, zext_ref, *, pos0, t_last):
    _mix_history(ph_ref, ch_ref, uext_ref, zext_ref, x_ref.shape[1])
    x = x_ref[0]
    ms = jnp.mean(x * x, axis=-1, keepdims=True)
    xn = (x * lax.rsqrt(ms + RMS_EPS) * nw_ref[...]).astype(BF16)
    n_in = win_ref.shape[1]
    for c0 in range(0, n_in, AB_PROJ_CHUNK):
        proj_ref[0, :, c0:c0 + AB_PROJ_CHUNK] = jnp.dot(xn, win_ref[:, c0:c0 + AB_PROJ_CHUNK],
                                                        preferred_element_type=F32)
    _mix_body(proj_ref, plin_ref, pscale_ref, cw_ref, mix_ref, uext_ref, zext_ref, pos0)
    y = x
    for c0 in range(0, wout_ref.shape[0], AB_OUT_CHUNK):
        y = y + jnp.dot(mix_ref[0, :, c0:c0 + AB_OUT_CHUNK].astype(BF16), wout_ref[c0:c0 + AB_OUT_CHUNK, :],
                        preferred_element_type=F32)
    o_ref[0] = y
    _mix_tails(ptail_ref, ctail_ref, uext_ref, zext_ref, t_last)


def _ab_layer(x, nw, w_in, w_out, layer, pool_hist, conv_hist, plin, pscale, cw, tm, pos0, t_valid):
    b, t, d = x.shape
    n_in = w_in.shape[2]
    nj = t // tm
    t_last = t_valid - (nj - 1) * tm
    resident = pl.Buffered(1)
    return pl.pallas_call(
        functools.partial(_ab_layer_kernel, pos0=pos0, t_last=t_last),
        grid=(b, nj),
        in_specs=[
            pl.BlockSpec((1, tm, d), lambda i, j: (i, j, 0)),
            pl.BlockSpec((1, d), lambda i, j: (0, 0)),
            pl.BlockSpec((None, d, n_in), lambda i, j: (layer, 0, 0), pipeline_mode=resident),
            pl.BlockSpec((1, POOL_HIST, D_POOL), lambda i, j: (i, 0, 0)),
            pl.BlockSpec((1, CONV_HIST, D_CONV), lambda i, j: (i, 0, 0)),
            pl.BlockSpec(plin.shape, lambda i, j: (0, 0, 0)),
            pl.BlockSpec((1, D_POOL), lambda i, j: (0, 0)),
            pl.BlockSpec((CONV_WIDTH, D_CONV), lambda i, j: (0, 0)),
            pl.BlockSpec((None, D_POOL + D_CONV, d), lambda i, j: (layer, 0, 0), pipeline_mode=resident),
        ],
        out_specs=[
            pl.BlockSpec((1, tm, d), lambda i, j: (i, j, 0)),
            pl.BlockSpec((1, POOL_HIST, D_POOL), lambda i, j: (i, 0, 0)),
            pl.BlockSpec((1, CONV_HIST, D_CONV), lambda i, j: (i, 0, 0)),
        ],
        out_shape=[
            jax.ShapeDtypeStruct((b, t, d), F32),
            jax.ShapeDtypeStruct((b, POOL_HIST, D_POOL), F32),
            jax.ShapeDtypeStruct((b, CONV_HIST, D_CONV), F32),
        ],
        scratch_shapes=[
            pltpu.VMEM((1, tm, n_in), F32),
            pltpu.VMEM((1, tm, D_POOL + D_CONV), BF16 if tm % (2 * SUBLANES) == 0 else F32),
            pltpu.VMEM((tm + POOL_PAD, D_POOL), F32),
            pltpu.VMEM((tm + CONV_PAD, D_CONV), F32),
        ],
        compiler_params=pltpu.CompilerParams(dimension_semantics=("parallel", "arbitrary"),
                                             vmem_limit_bytes=AB_LAYER_VMEM_LIMIT),
        name="ab_layer",
    )(x, nw, w_in, pool_hist, conv_hist, plin, pscale, cw, w_out)


def _rows(start, size, stride):
    if stride == 1:
        return pl.ds(start, size)
    return pl.ds(start, size, stride=stride)


def _attn_prompt_kernel(*refs, n_new, kv_layer, fill_layers, n_aliased):
    q_ref, k_blk_ref, v_blk_ref, g_ref, cache_ref = refs[:5]
    (o_ref, roll_ref, kp_ref, vp_ref, kd_ref, vd_ref, bias_ref, s_ref, ring_ref, sem_ref, kv_sem_ref,
     *state_refs) = refs[5 + n_aliased:]
    k_ref, v_ref = k_blk_ref.at[0], v_blk_ref.at[0]
    t = q_ref.shape[0]
    n_keep = kp_ref.shape[2]

    step = pl.program_id(0) * pl.num_programs(1) + pl.program_id(1)

    def kv_copies():
        return [pltpu.make_async_copy(src.at[0, pl.ds(t - n_keep, n_keep), :],
                                      dst.at[kv_layer, pl.program_id(0), :, pl.program_id(1), :],
                                      kv_sem_ref.at[m, 0])
                for m, (src, dst) in enumerate(((k_blk_ref, kp_ref), (v_blk_ref, vp_ref)))]

    def kv_fill_copies():
        pieces = n_keep // ROLL_CHUNK
        return [pltpu.make_async_copy(ring_ref.at[m],
                                      dst.at[layer, step // pieces, pl.ds((step % pieces) * ROLL_CHUNK, ROLL_CHUNK)],
                                      kv_sem_ref.at[m, 1 + n])
                for n, layer in enumerate(fill_layers) for m, dst in enumerate((kp_ref, vp_ref))]

    for copy in kv_copies():
        copy.start(priority=ROLL_DMA_PRIORITY)
    n_cb, n_buf = cache_ref.shape[1], cache_ref.shape[2]
    slab, slab_half = step // 2, step % 2
    lyr, cb = slab // n_cb, slab % n_cb
    dst0 = slab_half * (n_buf // 2 - n_new)
    n_ring = ring_ref.shape[0]

    def roll_read(c):
        return pltpu.make_async_copy(cache_ref.at[lyr, cb, pl.ds(dst0 + c * ROLL_CHUNK + n_new, ROLL_CHUNK)],
                                     ring_ref.at[c % n_ring], sem_ref.at[0, c % n_ring])

    def roll_write(c):
        return pltpu.make_async_copy(ring_ref.at[c % n_ring],
                                     roll_ref.at[lyr, cb, pl.ds(dst0 + c * ROLL_CHUNK, ROLL_CHUNK)],
                                     sem_ref.at[1, c % n_ring])

    def roll_tail():
        return pltpu.make_async_copy(ring_ref.at[0, pl.ds(0, n_new)],
                                     roll_ref.at[lyr, cb, pl.ds(n_buf - n_new, n_new)], sem_ref.at[1, 0])

    for c in range(n_ring):
        roll_read(c).start(priority=c % 2)
    n_groups = len(DILATED_GROUPS)
    qf_ref, kf_ref, vf_ref = state_refs[0:3]
    acc_refs = (None,) + tuple(state_refs[3:3 + n_groups - 1])
    stat_refs = (None,) + tuple(state_refs[3 + n_groups - 1:])
    mid_dil = DILATED_GROUPS[n_groups - 2][1]
    nt_dims = (((1,), (1,)), ((), ()))
    half = HEAD_DIM // 2
    lane = lax.broadcasted_iota(jnp.int32, (Q_BLOCK, HEAD_DIM), 1)

    @pl.when((pl.program_id(0) == 0) & (pl.program_id(1) == 0))
    def _():
        delta = (lax.broadcasted_iota(jnp.int32, (Q_BLOCK, 2 * Q_BLOCK), 0)
                 - lax.broadcasted_iota(jnp.int32, (Q_BLOCK, 2 * Q_BLOCK), 1))
        for sel in range(2):
            dist = delta + sel * Q_BLOCK
            bias_ref[sel] = jnp.where((dist >= 0) & (dist <= Q_BLOCK), 0.0, NEG_INF)
        vd_ref[:, :, HEAD_DIM:] = jnp.ones((n_groups, t, HEAD_DIM), BF16)

    def deinterleave(gi, d, d_src, ksrc_ref, vsrc_ref, keep_f32):
        stream_len = t // d
        ratio = d // d_src

        def body(c, carry):
            dst = pl.multiple_of(c * DEINT_CHUNK, DEINT_CHUNK)
            r = dst // stream_len
            m0 = dst % stream_len
            src = _rows((r % d_src) * (t // d_src) + r // d_src + ratio * m0, DEINT_CHUNK, ratio)
            k = ksrc_ref[src, :]
            v = vsrc_ref[src, :]
            kd_ref[gi, pl.ds(dst, DEINT_CHUNK), :] = k.astype(BF16)
            vd_ref[gi, pl.ds(dst, DEINT_CHUNK), 0:HEAD_DIM] = v.astype(BF16)
            if keep_f32:
                qf_ref[pl.ds(dst, DEINT_CHUNK), :] = q_ref[src, :]
                kf_ref[pl.ds(dst, DEINT_CHUNK), :] = k
                vf_ref[pl.ds(dst, DEINT_CHUNK), :] = v
            return carry

        lax.fori_loop(0, t // DEINT_CHUNK, body, 0, unroll=2)

    def merged_output(rows, m0, acc0, l0):
        ms, ls, accs = [m0], [l0], [acc0]
        for gi in range(1, n_groups):
            st = stat_refs[gi][rows, :]
            sw = pltpu.roll(st, half, axis=1)
            ms.append(jnp.where(lane < half, st, sw))
            ls.append(jnp.where(lane < half, sw, st))
            accs.append(acc_refs[gi][rows, :])
        m_all = functools.reduce(jnp.maximum, ms)
        num = None
        den = None
        for m_g, l_g, acc_g in zip(ms, ls, accs):
            w = jnp.exp2(m_g - m_all)
            num = w * acc_g if num is None else num + w * acc_g
            den = w * l_g if den is None else den + w * l_g
        g = g_ref[rows, :]
        return ((num * g) / (den * (1.0 + jnp.exp(-g)))).astype(o_ref.dtype)

    def run_group(gi, d, after_scores, after_values):
        stream_len = t // d
        nb = stream_len // Q_BLOCK

        def block_rows(idx):
            r = idx // nb
            mb = idx % nb
            kb = jnp.maximum(mb - 1, 0)
            q_start = r + d * Q_BLOCK * mb
            q_rows = _rows(pl.multiple_of(q_start, Q_BLOCK) if d == 1 else q_start, Q_BLOCK, d)
            k_rows = pl.ds(pl.multiple_of(r * stream_len + kb * Q_BLOCK, Q_BLOCK), 2 * Q_BLOCK)
            return q_rows, k_rows, mb - kb

        def load_q(idx, q_rows):
            if gi < n_groups - 1:
                return q_ref[q_rows, :]
            r = idx // nb
            mb = idx % nb
            ratio = d // mid_dil
            start = (r % mid_dil) * (t // mid_dil) + r // mid_dil + ratio * Q_BLOCK * mb
            return qf_ref[_rows(start, Q_BLOCK, ratio), :]

        def scores(it, carry):
            s = []
            for u in range(SCORE_BLOCKS_PER_TRIP):
                idx = it * SCORE_BLOCKS_PER_TRIP + u
                q_rows, k_rows, sel = block_rows(idx)
                q = (load_q(idx, q_rows) * (ATTN_SCALE * LOG2_E)).astype(BF16)
                s.append(lax.dot_general(q, kd_ref[gi, k_rows, :], nt_dims, preferred_element_type=F32)
                         + bias_ref[sel])
            for u in range(SCORE_BLOCKS_PER_TRIP):
                s_ref[it * SCORE_BLOCKS_PER_TRIP + u] = s[u]
            return carry

        n_val = MERGE_BLOCKS_PER_TRIP if gi == 0 else VALUE_BLOCKS_PER_TRIP

        def values(it, carry):
            rows, s, v = [], [], []
            for u in range(n_val):
                q_rows, k_rows, _ = block_rows(it * n_val + u)
                rows.append(q_rows)
                s.append(s_ref[it * n_val + u])
                v.append(vd_ref[gi, k_rows, :])
            outs = []
            for u in range(n_val):
                m_blk = jnp.max(s[u], axis=-1, keepdims=True)
                p = jnp.exp2(s[u] - m_blk).astype(BF16)
                pv = jnp.dot(p, v[u], preferred_element_type=F32)
                acc, l_blk = pv[:, 0:HEAD_DIM], pv[:, HEAD_DIM:]
                if gi == 0:
                    outs.append((merged_output(rows[u], jnp.broadcast_to(m_blk, acc.shape), acc, l_blk),))
                else:
                    outs.append((acc, jnp.where(lane < half, m_blk, l_blk)))
            for u in range(n_val):
                if gi == 0:
                    o_ref[rows[u], :] = outs[u][0]
                else:
                    acc_refs[gi][rows[u], :] = outs[u][0]
                    stat_refs[gi][rows[u], :] = outs[u][1]
            return carry

        lax.fori_loop(0, d * nb // SCORE_BLOCKS_PER_TRIP, scores, 0)
        after_scores()
        lax.fori_loop(0, d * nb // n_val, values, 0)
        after_values()

    ksrc_ref, vsrc_ref, d_src = k_ref, v_ref, 1
    for gi, (_, d) in enumerate(DILATED_GROUPS):
        keep_f32 = 0 < gi < n_groups - 1
        deinterleave(gi, d, d_src, ksrc_ref, vsrc_ref, keep_f32)
        if keep_f32:
            ksrc_ref, vsrc_ref, d_src = kf_ref, vf_ref, d
    def forward():
        for c in range(n_ring):
            roll_read(c).wait()
            roll_write(c).start(priority=c % 2)
        for copy in kv_fill_copies():
            copy.start(priority=ROLL_DMA_PRIORITY)

    def drain():
        for c in range(n_ring):
            roll_write(c).wait()
        for copy in kv_copies() + kv_fill_copies():
            copy.wait()

        @pl.when(slab_half == 1)
        def _():
            roll_tail().start()
            roll_tail().wait()

    def nothing():
        pass

    hooks = [(nothing, forward), (nothing, nothing), (nothing, drain)]
    for (after_scores, after_values), gi in zip(hooks, reversed(range(n_groups))):
        run_group(gi, DILATED_GROUPS[gi][1], after_scores, after_values)


def _attn_prompt(proj, cache, n_new, kv_layer, n_kv_layers, n_keep, kv_prev, out_dtype):
    b, t, _ = proj.shape
    first = kv_prev is None
    assert first == (kv_layer == 0)
    fill_layers = tuple(range(1, n_kv_layers)) if first else ()
    assert b * n_keep == b * N_HEADS * ROLL_CHUNK
    kv_shape = jax.ShapeDtypeStruct((n_kv_layers, b, n_keep, N_HEADS, HEAD_DIM), F32)
    any_spec = pl.BlockSpec(memory_space=pl.ANY)
    aliased = () if first else tuple(kv_prev)
    n_groups = len(DILATED_GROUPS)
    n_layers, n_cb, n_buf = cache.shape[:3]
    assert b * N_HEADS == 2 * n_layers * n_cb and n_buf // 2 == ROLL_RING * ROLL_CHUNK
    dils = [dil for _, dil in DILATED_GROUPS]
    assert n_groups == 3 and dils[0] == 1 and all(hi % lo == 0 for lo, hi in zip(dils, dils[1:]))
    for window, dil in DILATED_GROUPS:
        assert window // dil == Q_BLOCK and (t // dil) % DEINT_CHUNK == 0
        assert (t // Q_BLOCK) % SCORE_BLOCKS_PER_TRIP == 0 and (t // Q_BLOCK) % VALUE_BLOCKS_PER_TRIP == 0
    blk = (None, t, HEAD_DIM)
    og, rolled, kp, vp = pl.pallas_call(
        functools.partial(_attn_prompt_kernel, n_new=n_new, kv_layer=kv_layer, fill_layers=fill_layers,
                          n_aliased=len(aliased)),
        grid=(b, N_HEADS),
        in_specs=[
            pl.BlockSpec(blk, lambda i, h: (i, 0, h)),
            pl.BlockSpec((1, t, HEAD_DIM), lambda i, h: (i, 0, N_HEADS + h)),
            pl.BlockSpec((1, t, HEAD_DIM), lambda i, h: (i, 0, 2 * N_HEADS + h)),
            pl.BlockSpec(blk, lambda i, h: (i, 0, 3 * N_HEADS + h)),
            any_spec,
        ] + [any_spec] * len(aliased),
        out_specs=[pl.BlockSpec(blk, lambda i, h: (i, 0, h)), any_spec, any_spec, any_spec],
        out_shape=[jax.ShapeDtypeStruct((b, t, D_ATTN), out_dtype),
                   jax.ShapeDtypeStruct(cache.shape, cache.dtype), kv_shape, kv_shape],
        scratch_shapes=(
            [pltpu.VMEM((n_groups, t, HEAD_DIM), BF16), pltpu.VMEM((n_groups, t, 2 * HEAD_DIM), BF16)]
            + [pltpu.VMEM((2, Q_BLOCK, 2 * Q_BLOCK), F32)]
            + [pltpu.VMEM((t // Q_BLOCK, Q_BLOCK, 2 * Q_BLOCK), F32)]
            + [pltpu.VMEM((ROLL_RING, ROLL_CHUNK) + cache.shape[3:], cache.dtype)]
            + [pltpu.SemaphoreType.DMA((2, ROLL_RING))]
            + [pltpu.SemaphoreType.DMA((2, 1 + len(fill_layers)))]
            + [pltpu.VMEM((t, HEAD_DIM), F32)] * (3 + 2 * (n_groups - 1))),
        input_output_aliases={5 + n: 2 + n for n in range(len(aliased))},
        compiler_params=pltpu.CompilerParams(dimension_semantics=("arbitrary", "arbitrary"),
                                             vmem_limit_bytes=ATTN_VMEM_LIMIT),
        name="attn_prompt",
    )(proj, proj, proj, proj, cache, *aliased)
    return og, rolled, (kp, vp)


def _group_count(dist):
    cnt = jnp.zeros(dist.shape, F32)
    for window, dil in DILATED_GROUPS:
        hit = (dist >= 0) & (dist <= window) & ((dist & (dil - 1)) == 0)
        cnt = cnt + hit.astype(F32)
    return cnt


def _attn_sample_kernel(q_ref, g_ref, kn_ref, vn_ref, ks_ref, vs_ref, kd_ref, vd_ref, o_ref,
                        m_ref, l_ref, acc_ref, *, n_buf):
    j = pl.program_id(1)
    nj = pl.num_programs(1)
    n_rows = q_ref.shape[0]
    nt_dims = (((1,), (1,)), ((), ()))
    head_bits = N_HEADS.bit_length() - 1
    dil = DILATED_GROUPS[-1][1]

    @pl.when(j == 0)
    def _():
        m_ref[...] = jnp.full(m_ref.shape, NEG_INF, F32)
        l_ref[...] = jnp.zeros(l_ref.shape, F32)
        acc_ref[...] = jnp.zeros(acc_ref.shape, F32)

    q = (q_ref[...] * ATTN_SCALE).astype(BF16)

    def accumulate(k2, v2, key_pos):
        n_cols = k2.shape[0]
        row = lax.broadcasted_iota(jnp.int32, (n_rows, n_cols), 0)
        col = lax.broadcasted_iota(jnp.int32, (n_rows, n_cols), 1)
        same_head = (row & (N_HEADS - 1)) == (col & (N_HEADS - 1))
        dist = (n_buf + (row >> head_bits)) - key_pos(col)
        cnt = jnp.where(same_head, _group_count(dist), 0.0)
        s = lax.dot_general(q, k2.astype(BF16), nt_dims, preferred_element_type=F32)
        s = jnp.where(cnt > 0, s, NEG_INF)
        m_prev = m_ref[...]
        m_new = jnp.maximum(m_prev, jnp.max(s, axis=-1, keepdims=True))
        alpha = jnp.exp(m_prev - m_new)
        p = cnt * jnp.exp(s - m_new)
        l_ref[...] = alpha * l_ref[...] + jnp.sum(p, axis=-1, keepdims=True)
        acc_ref[...] = alpha * acc_ref[...] + jnp.dot(p.astype(BF16), v2.astype(BF16),
                                                      preferred_element_type=F32)
        m_ref[...] = m_new

    @pl.when(j < nj - 1)
    def _():
        periods, residues = ks_ref.shape[0], ks_ref.shape[1]
        n_cols = periods * residues * N_HEADS
        res_bits = residues.bit_length() - 1

        def key_pos(col):
            period = j * periods + (col >> (head_bits + res_bits))
            return period * dil + ((col >> head_bits) & (residues - 1))

        accumulate(ks_ref[...].reshape(n_cols, HEAD_DIM), vs_ref[...].reshape(n_cols, HEAD_DIM), key_pos)

    @pl.when(j == nj - 1)
    def _():
        n_dense = kd_ref.shape[0]
        accumulate(kd_ref[...].reshape(n_dense * N_HEADS, HEAD_DIM),
                   vd_ref[...].reshape(n_dense * N_HEADS, HEAD_DIM),
                   lambda col: (n_buf - n_dense) + (col >> head_bits))
        accumulate(kn_ref[...], vn_ref[...], lambda col: n_buf + (col >> head_bits))
        o_ref[...] = (acc_ref[...] / l_ref[...]) * _silu(g_ref[...])


def _attn_sample(q2, g2, kn2, vn2, cache_k, cache_v, layer):
    b, n_rows, _ = q2.shape
    n_layers, _, n_buf = cache_k.shape[:3]
    n_new = n_rows // N_HEADS
    dil = DILATED_GROUPS[-1][1]
    n_dense = SAMPLE_DENSE_ROWS
    assert all(w <= n_dense for w, _ in DILATED_GROUPS[:-1]) and n_buf % dil == 0 and n_dense % dil == 0
    assert n_new <= SAMPLE_RESIDUES and dil % SAMPLE_RESIDUES == 0 and n_buf % n_dense == 0
    n_sparse_steps = (n_buf - n_dense) // dil // SAMPLE_PERIODS
    assert n_sparse_steps * SAMPLE_PERIODS * dil == n_buf - n_dense and n_sparse_steps >= 1
    by_period = (n_layers, cache_k.shape[1], n_buf // dil, dil, N_HEADS, HEAD_DIM)
    rblk = (None, n_rows, HEAD_DIM)
    sparse = pl.BlockSpec((None, None, SAMPLE_PERIODS, SAMPLE_RESIDUES, N_HEADS, HEAD_DIM),
                          lambda i, j: (layer, i, jnp.minimum(j, n_sparse_steps - 1), 0, 0, 0))
    dense = pl.BlockSpec((None, None, n_dense, N_HEADS, HEAD_DIM),
                         lambda i, j: (layer, i, n_buf // n_dense - 1, 0, 0))
    return pl.pallas_call(
        functools.partial(_attn_sample_kernel, n_buf=n_buf),
        grid=(b, n_sparse_steps + 1),
        in_specs=[pl.BlockSpec(rblk, lambda i, j: (i, 0, 0))] * 4 + [sparse, sparse, dense, dense],
        out_specs=pl.BlockSpec(rblk, lambda i, j: (i, 0, 0)),
        out_shape=jax.ShapeDtypeStruct((b, n_rows, HEAD_DIM), F32),
        scratch_shapes=[pltpu.VMEM((n_rows, 1), F32), pltpu.VMEM((n_rows, 1), F32),
                        pltpu.VMEM((n_rows, HEAD_DIM), F32)],
        compiler_params=_params("parallel", "arbitrary"),
        name="attn_sample",
    )(q2, g2, kn2, vn2, cache_k.reshape(by_period), cache_v.reshape(by_period), cache_k, cache_v)


def _write_new_rows_kernel(rolled_ref, new_ref, out_ref, sem):
    del rolled_ref
    n_layers, n_batch, n_new = new_ref.shape[:3]
    n_buf = out_ref.shape[2]
    copies = [pltpu.make_async_copy(new_ref.at[l, b], out_ref.at[l, b, pl.ds(n_buf - n_new, n_new)],
                                    sem.at[l * n_batch + b])
              for l in range(n_layers) for b in range(n_batch)]
    for copy in copies:
        copy.start()
    for copy in copies:
        copy.wait()


def _write_new_rows(rolled, new):
    n_layers, n_batch = new.shape[:2]
    return pl.pallas_call(
        _write_new_rows_kernel,
        in_specs=[pl.BlockSpec(memory_space=pl.ANY), pl.BlockSpec(memory_space=pltpu.VMEM)],
        out_specs=pl.BlockSpec(memory_space=pl.ANY),
        out_shape=jax.ShapeDtypeStruct(rolled.shape, rolled.dtype),
        scratch_shapes=[pltpu.SemaphoreType.DMA((n_layers * n_batch,))],
        input_output_aliases={0: 0},
        name="write_new_rows",
    )(rolled, new)


def _cast_kernel(x_ref, o_ref):
    o_ref[...] = x_ref[...].astype(o_ref.dtype)


def _cast_bf16(w, rows):
    n_layers, r, c = w.shape
    return pl.pallas_call(
        _cast_kernel,
        grid=(n_layers, r // rows),
        in_specs=[pl.BlockSpec((1, rows, c), lambda l, i: (l, i, 0))],
        out_specs=pl.BlockSpec((1, rows, c), lambda l, i: (l, i, 0)),
        out_shape=jax.ShapeDtypeStruct(w.shape, BF16),
        compiler_params=_params("parallel", "parallel"),
        name="cast_bf16",
    )(w)


def kernel(x_prompt, x_sample, state_pool, state_conv, cache_k, cache_v, norm_w, final_norm_w,
           w_in_ab, pool_lin, pool_scale, conv_w, w_out_ab, w_in_c, w_out_c):
    bp, t, d = x_prompt.shape
    bs, ts, _ = x_sample.shape
    ts_pad = -(-ts // SUBLANES) * SUBLANES
    n_buf = cache_k.shape[2]
    n_keep_p = min(MAX_WINDOW, t)

    hp = x_prompt.reshape(bp * t, d)
    hs = jnp.pad(x_sample, ((0, 0), (0, ts_pad - ts), (0, 0))).reshape(bs * ts_pad, d)
    zero_pool = jnp.zeros((bp, POOL_HIST, D_POOL), F32)
    zero_conv = jnp.zeros((bp, CONV_HIST, D_CONV), F32)

    tm_p, tn = 1024, 1024
    tm_s = bs * ts_pad
    cast_rows = 256

    w_in_ab = _cast_bf16(w_in_ab, cast_rows)
    w_out_ab = _cast_bf16(w_out_ab, cast_rows)
    w_in_c = _cast_bf16(w_in_c, cast_rows)
    w_out_c = _cast_bf16(w_out_c, cast_rows)
    pool_lin = _cast_bf16(pool_lin.reshape(pool_lin.shape[0], -1, D_POOL_GROUP), cast_rows).reshape(pool_lin.shape)

    def head_rows(a):
        return a.reshape(a.shape[0], a.shape[1] * N_HEADS, HEAD_DIM)

    assert DEPTH == 4
    assert min(MAX_WINDOW, n_buf + ts) == n_buf
    rolled = []
    kv_p = None
    fw = final_norm_w.reshape(1, d)
    pool_p, pool_s, conv_p, conv_s = [], [], [], []
    k_new, v_new = [], []
    for l in range(DEPTH):
        i = l // 2
        nw = norm_w[l].reshape(1, d)
        if l % 2 == 0:
            pscale = pool_scale[i].reshape(1, D_POOL)
            hp, pp, cp = _ab_layer(hp.reshape(bp, t, d), nw, w_in_ab, w_out_ab, i, zero_pool, zero_conv,
                                   pool_lin[i], pscale, conv_w[i], AB_ROW_TILE, 0, t)
            hp = hp.reshape(bp * t, d)
            proj_s = _norm_matmul(hs, nw, w_in_ab, i, tm_s, tn).reshape(bs, ts_pad, -1)
            mix_s, ps, cs = _ab_mix(proj_s, state_pool[i], state_conv[i], pool_lin[i], pscale, conv_w[i],
                                    ts_pad, PAST_LEN, ts, F32)
            hs = _matmul_residual(mix_s.reshape(bs * ts_pad, -1), w_out_ab, i, hs, tm_s, tn)
            pool_p.append(pp)
            pool_s.append(ps)
            conv_p.append(cp)
            conv_s.append(cs)
        else:
            proj_p = _norm_matmul(hp, nw, w_in_c, i, tm_p, tn).reshape(bp, t, -1)
            proj_s = _norm_matmul(hs, nw, w_in_c, i, tm_s, tn).reshape(bs, ts_pad, -1)[:, :ts]
            og_p, rolled_c, kv_p = _attn_prompt(proj_p, (cache_k, cache_v)[i], ts, i, DEPTH // 2, n_keep_p,
                                                kv_p, BF16)
            rolled.append(rolled_c)
            q_s, kn_s, vn_s, g_s = (head_rows(proj_s[..., c * D_ATTN:(c + 1) * D_ATTN]) for c in range(4))
            og_s = _attn_sample(q_s, g_s, kn_s, vn_s, cache_k, cache_v, i)
            og_s = jnp.pad(og_s.reshape(bs, ts, D_ATTN), ((0, 0), (0, ts_pad - ts), (0, 0)))
            if l == DEPTH - 1:
                hp = _matmul_residual_norm(og_p.reshape(bp * t, -1), w_out_c, i, hp, fw, 512)
                hs = _matmul_residual_norm(og_s.reshape(bs * ts_pad, -1), w_out_c, i, hs, fw, tm_s)
            else:
                hp = _matmul_residual(og_p.reshape(bp * t, -1), w_out_c, i, hp, tm_p, tn)
                hs = _matmul_residual(og_s.reshape(bs * ts_pad, -1), w_out_c, i, hs, tm_s, tn)
            k_new.append(kn_s.reshape(bs, ts, N_HEADS, HEAD_DIM))
            v_new.append(vn_s.reshape(bs, ts, N_HEADS, HEAD_DIM))

    k_s = _write_new_rows(rolled[0], jnp.stack(k_new))
    v_s = _write_new_rows(rolled[1], jnp.stack(v_new))
    k_p, v_p = kv_p

    y_prompt = hp.reshape(bp, t, d)
    y_sample = hs.reshape(bs, ts_pad, d)[:, :ts]
    return (y_prompt, y_sample, jnp.stack(pool_p), jnp.stack(pool_s), jnp.stack(conv_p), jnp.stack(conv_s),
            k_p, k_s, v_p, v_s)
```
